```python
import math
import jax, jax.numpy as jnp
from jax import lax
import numpy as np

D_MODEL = 1024
BATCH = 8
SEQ = 2048
DEPTH = 1
DEC_BATCH = 128
DEC_SEQ = 4
PAST_LEN = 16384
PAGE_SIZE = 128

H_A = 8
DK_A = 128
DV_A = 128
D_QK_A = H_A * DK_A
D_V_A = H_A * DV_A
D_QKV_A = 2 * D_QK_A + D_V_A
CHUNK_A = 64
H_B = 16
P_B = 64
N_B = 128
G_B = 2
D_B = H_B * P_B
D_XBC = D_B + 2 * G_B * N_B
CHUNK_B = 64
CONV_W = 4
D_MIX = D_V_A + D_B
D_IN = D_QKV_A + D_V_A + 2 * H_A + D_B + D_XBC + H_B
N_GROUPS_E = 4
EXPERTS_PER_GROUP = 4
TOP_K = 2
D_EXPERT = 512
EPS = 1e-6

kernel_name = 'hymba_gdn_ssd_hiermoe_step'


def _rmsnorm(x, w):
    xf = x.astype(jnp.float32)
    return xf * lax.rsqrt(jnp.mean(xf * xf, axis=-1, keepdims=True) + EPS) * w.astype(jnp.float32)


def _l2norm(x):
    return x * lax.rsqrt(jnp.sum(x * x, axis=-1, keepdims=True) + EPS)


def _causal_conv(u, buf, w, b):
    L = u.shape[1]
    full = jnp.concatenate([buf.astype(u.dtype), u], axis=1)
    y = sum(full[:, j:j + L] * w[j].astype(u.dtype) for j in range(CONV_W))
    if b is not None:
        y = y + b.astype(u.dtype)
    return y, full[:, L:]


def _gated_delta_chunked(q, k, v, beta, g, s0):
    Bsz, L, H, K = q.shape
    V = v.shape[-1]
    c = math.gcd(L, CHUNK_A)
    n = L // c

    def blk(t):
        t = jnp.moveaxis(t, 2, 1)
        return t.reshape(t.shape[:2] + (n, c) + t.shape[3:])

    q, k, v, beta, g = blk(q), blk(k), blk(v), blk(beta), blk(g)
    gc = jnp.cumsum(g, axis=-1)
    causal = jnp.tril(jnp.ones((c, c), bool))
    strict = jnp.tril(jnp.ones((c, c), bool), -1)
    decay = jnp.exp(jnp.where(causal, gc[..., :, None] - gc[..., None, :], -jnp.inf))
    kb = k * beta[..., None]
    a_strict = jnp.where(strict, jnp.einsum('bhnik,bhnjk->bhnij', kb, k) * decay, 0.0)
    m = a_strict + jnp.eye(c, dtype=q.dtype)
    rhs = jnp.concatenate([v * beta[..., None], kb * jnp.exp(gc)[..., None]], axis=-1)
    sol = lax.linalg.triangular_solve(m, rhs, left_side=True, lower=True, unit_diagonal=True)
    u, w = sol[..., :V], sol[..., V:]
    qk = jnp.where(causal, jnp.einsum('bhnik,bhnjk->bhnij', q, k) * decay, 0.0)
    tail = jnp.exp(gc[..., -1:] - gc)

    def step(s, xs):
        qi, ki, ui, wi, gci, qki, taili = xs
        v_new = ui - jnp.einsum('bhck,bhkv->bhcv', wi, s)
        o = (jnp.einsum('bhck,bhkv->bhcv', qi * jnp.exp(gci)[..., None], s)
             + jnp.einsum('bhij,bhjv->bhiv', qki, v_new))
        s = (s * jnp.exp(gci[..., -1])[..., None, None]
             + jnp.einsum('bhck,bhcv->bhkv', ki * taili[..., None], v_new))
        return s, o

    xs = tuple(jnp.moveaxis(t, 2, 0) for t in (q, k, u, w, gc, qk, tail))
    s_final, o = lax.scan(step, s0, xs)
    o = jnp.moveaxis(o, 0, 2).reshape(Bsz, H, L, V)
    return jnp.moveaxis(o, 1, 2), s_final


def _ssd_chunked(x, dt, a, bm, cm, h0):
    Bsz, L, H, P = x.shape
    G, N = bm.shape[2], bm.shape[3]
    R = H // G
    c = math.gcd(L, CHUNK_B)
    n = L // c
    x = x.reshape(Bsz, n, c, G, R, P)
    dt = dt.reshape(Bsz, n, c, G, R)
    bm = bm.reshape(Bsz, n, c, G, N)
    cm = cm.reshape(Bsz, n, c, G, N)
    dac = jnp.moveaxis(jnp.cumsum(dt * a.reshape(G, R), axis=2), 2, -1)
    causal = jnp.tril(jnp.ones((c, c), bool))
    seg = jnp.exp(jnp.where(causal, dac[..., :, None] - dac[..., None, :], -jnp.inf))
    xdt = x * dt[..., None]
    cb = jnp.einsum('bnigs,bnjgs->bngij', cm, bm)
    y_diag = jnp.einsum('bngij,bngrij,bnjgrp->bnigrp', cb, seg, xdt)
    tail = jnp.exp(dac[..., -1:] - dac)
    states = jnp.einsum('bnjgs,bngrj,bnjgrp->bngrps', bm, tail, xdt)
    chunk_decay = jnp.exp(dac[..., -1])

    def step(h, xs):
        st, dec = xs
        return h * dec[..., None, None] + st, h

    h_final, h_prev = lax.scan(step, h0.reshape(Bsz, G, R, P, N),
                               (jnp.moveaxis(states, 1, 0), jnp.moveaxis(chunk_decay, 1, 0)))
    h_prev = jnp.moveaxis(h_prev, 0, 1)
    y_off = jnp.einsum('bnigs,bngrps,bngri->bnigrp', cm, h_prev, jnp.exp(dac))
    y = (y_diag + y_off).reshape(Bsz, L, H, P)
    return y, h_final.reshape(Bsz, H, P, N)


def _hier_moe(h, w_rg, b_rg, w_re, b_re, w_gate, w_up, w_down):
    Bsz, L, D = h.shape
    t = h.reshape(-1, D)
    gl = t @ w_rg.astype(jnp.float32) + b_rg.astype(jnp.float32)
    gp = jax.nn.softmax(gl, axis=-1)
    gsel = jnp.argmax(gl, axis=-1)
    gw = jnp.take_along_axis(gp, gsel[:, None], axis=-1)
    el = jnp.einsum('td,dge->tge', t, w_re.astype(jnp.float32)) + b_re.astype(jnp.float32)
    el_sel = jnp.take_along_axis(el, gsel[:, None, None], axis=1)[:, 0]
    vals, idx = lax.top_k(el_sel, TOP_K)
    ew = jax.nn.softmax(vals, axis=-1) * gw
    comb = jnp.einsum('tk,tke->te', ew, jax.nn.one_hot(idx, EXPERTS_PER_GROUP, dtype=jnp.float32))
    comb_full = jax.nn.one_hot(gsel, N_GROUPS_E, dtype=jnp.float32)[:, :, None] * comb[:, None, :]
    out = jnp.zeros_like(t)
    for gi in range(N_GROUPS_E):
        hg = (jax.nn.silu(jnp.einsum('td,edf->tef', t, w_gate[gi].astype(jnp.float32)))
              * jnp.einsum('td,edf->tef', t, w_up[gi].astype(jnp.float32)))
        out = out + jnp.einsum('tef,efd,te->td', hg, w_down[gi].astype(jnp.float32), comb_full[:, gi])
    return out.reshape(Bsz, L, D)


def _layer(x, s_delta, s_dconv, s_ssm, s_sconv, norm_mix, w_in, conv_a_w, a_log_a, dt_bias_a, norm_a,
           conv_b_w, conv_b_b, a_log_b, dt_bias_b, d_skip_b, norm_b, w_out, norm_ffn,
           w_rg, b_rg, w_re, b_re, w_gate, w_up, w_down):
    f32 = jnp.float32
    Bsz, L, _ = x.shape
    h = _rmsnorm(x, norm_mix)
    proj = h @ w_in.astype(f32)
    cuts = list(np.cumsum([D_QKV_A, D_V_A, H_A, H_A, D_B, D_XBC]))
    qkv, gate_a, beta_raw, alpha_raw, z_b, xbc, dt_raw = jnp.split(proj, [int(i) for i in cuts], axis=-1)

    qkv, new_dconv = _causal_conv(qkv, s_dconv, conv_a_w, None)
    qkv = jax.nn.silu(qkv)
    q, k, v = jnp.split(qkv, [D_QK_A, 2 * D_QK_A], axis=-1)
    q = _l2norm(q.reshape(Bsz, L, H_A, DK_A)) * (DK_A ** -0.5)
    k = _l2norm(k.reshape(Bsz, L, H_A, DK_A))
    v = v.reshape(Bsz, L, H_A, DV_A)
    beta = jax.nn.sigmoid(beta_raw)
    g = -jnp.exp(a_log_a.astype(f32)) * jax.nn.softplus(alpha_raw + dt_bias_a.astype(f32))
    o_a, new_delta = _gated_delta_chunked(q, k, v, beta, g, s_delta.astype(f32))
    o_a = _rmsnorm(o_a, norm_a) * jax.nn.silu(gate_a.reshape(Bsz, L, H_A, DV_A))
    o_a = o_a.reshape(Bsz, L, D_V_A)

    xbc, new_sconv = _causal_conv(xbc, s_sconv, conv_b_w, conv_b_b)
    xbc = jax.nn.silu(xbc)
    xs, bm, cm = jnp.split(xbc, [D_B, D_B + G_B * N_B], axis=-1)
    xs = xs.reshape(Bsz, L, H_B, P_B)
    bm = bm.reshape(Bsz, L, G_B, N_B)
    cm = cm.reshape(Bsz, L, G_B, N_B)
    dt = jax.nn.softplus(dt_raw + dt_bias_b.astype(f32))
    a = -jnp.exp(a_log_b.astype(f32))
    y_b, new_ssm = _ssd_chunked(xs, dt, a, bm, cm, s_ssm.astype(f32))
    y_b = y_b + d_skip_b.astype(f32)[:, None] * xs
    y_b = y_b.reshape(Bsz, L, G_B, D_B // G_B) * jax.nn.silu(z_b).reshape(Bsz, L, G_B, D_B // G_B)
    y_b = _rmsnorm(y_b, norm_b.reshape(G_B, D_B // G_B)).reshape(Bsz, L, D_B)

    mix = jnp.concatenate([o_a, y_b], axis=-1) @ w_out.astype(f32)
    x = x + mix.astype(x.dtype)
    x = x + _hier_moe(_rmsnorm(x, norm_ffn), w_rg, b_rg, w_re, b_re, w_gate, w_up, w_down).astype(x.dtype)
    return x, new_delta, new_dconv, new_ssm, new_sconv


def _inv_softplus_dt(key, shape):
    dt = jnp.exp(jax.random.uniform(key, shape, jnp.float32, math.log(1e-3), math.log(1e-1)))
    return dt + jnp.log(-jnp.expm1(-dt))


def setup_inputs(seed: int = 0) -> dict:
    key = jax.random.key(seed)
    ks = jax.random.split(key, 32)
    f32 = jnp.float32

    def nrm(k, shape, scale):
        return jax.random.normal(k, shape, f32) * scale

    return {
        'x_prompt': nrm(ks[0], (BATCH, SEQ, D_MODEL), 1.0),
        'x_sample': nrm(ks[1], (DEC_BATCH, DEC_SEQ, D_MODEL), 1.0),
        'state_delta': nrm(ks[2], (DEPTH, DEC_BATCH, H_A, DK_A, DV_A), 0.1),
        'state_delta_conv': nrm(ks[3], (DEPTH, DEC_BATCH, CONV_W - 1, D_QKV_A), 1.0),
        'state_ssm': nrm(ks[4], (DEPTH, DEC_BATCH, H_B, P_B, N_B), 0.1),
        'state_ssm_conv': nrm(ks[5], (DEPTH, DEC_BATCH, CONV_W - 1, D_XBC), 1.0),
        'norm_mix': 1.0 + nrm(ks[6], (DEPTH, D_MODEL), 0.01),
        'w_in': nrm(ks[7], (DEPTH, D_MODEL, D_IN), D_MODEL ** -0.5),
        'conv_a_w': nrm(ks[8], (DEPTH, CONV_W, D_QKV_A), CONV_W ** -0.5),
        'a_log_a': jnp.log(jax.random.uniform(ks[9], (DEPTH, H_A), f32, 1.0, 16.0)),
        'dt_bias_a': _inv_softplus_dt(ks[10], (DEPTH, H_A)),
        'norm_a': 1.0 + nrm(ks[11], (DEPTH, DV_A), 0.01),
        'conv_b_w': nrm(ks[12], (DEPTH, CONV_W, D_XBC), CONV_W ** -0.5),
        'conv_b_b': nrm(ks[13], (DEPTH, D_XBC), 0.01),
        'a_log_b': jnp.log(jax.random.uniform(ks[14], (DEPTH, H_B), f32, 1.0, 16.0)),
        'dt_bias_b': _inv_softplus_dt(ks[15], (DEPTH, H_B)),
        'd_skip_b': 1.0 + nrm(ks[16], (DEPTH, H_B), 0.01),
        'norm_b': 1.0 + nrm(ks[17], (DEPTH, D_B), 0.01),
        'w_out': nrm(ks[18], (DEPTH, D_MIX, D_MODEL), D_MIX ** -0.5),
        'norm_ffn': 1.0 + nrm(ks[19], (DEPTH, D_MODEL), 0.01),
        'w_router_group': nrm(ks[20], (DEPTH, D_MODEL, N_GROUPS_E), D_MODEL ** -0.5),
        'b_router_group': nrm(ks[21], (DEPTH, N_GROUPS_E), 0.01),
        'w_router_expert': nrm(ks[22], (DEPTH, D_MODEL, N_GROUPS_E, EXPERTS_PER_GROUP), D_MODEL ** -0.5),
        'b_router_expert': nrm(ks[23], (DEPTH, N_GROUPS_E, EXPERTS_PER_GROUP), 0.01),
        'w_gate_e': nrm(ks[24], (DEPTH, N_GROUPS_E, EXPERTS_PER_GROUP, D_MODEL, D_EXPERT), D_MODEL ** -0.5),
        'w_up_e': nrm(ks[25], (DEPTH, N_GROUPS_E, EXPERTS_PER_GROUP, D_MODEL, D_EXPERT), D_MODEL ** -0.5),
        'w_down_e': nrm(ks[26], (DEPTH, N_GROUPS_E, EXPERTS_PER_GROUP, D_EXPERT, D_MODEL), D_EXPERT ** -0.5),
        'norm_final': 1.0 + nrm(ks[27], (D_MODEL,), 0.01),
    }


def reference(x_prompt, x_sample, state_delta, state_delta_conv, state_ssm, state_ssm_conv,
              norm_mix, w_in, conv_a_w, a_log_a, dt_bias_a, norm_a, conv_b_w, conv_b_b,
              a_log_b, dt_bias_b, d_skip_b, norm_b, w_out, norm_ffn, w_router_group, b_router_group,
              w_router_expert, b_router_expert, w_gate_e, w_up_e, w_down_e, norm_final):
    f32 = jnp.float32
    xp, xs = x_prompt, x_sample
    dp, dcp, sp, scp = [], [], [], []
    ds, dcs, ss, scs = [], [], [], []
    bp = x_prompt.shape[0]
    for l in range(DEPTH):
        weights = (norm_mix[l], w_in[l], conv_a_w[l], a_log_a[l], dt_bias_a[l], norm_a[l],
                   conv_b_w[l], conv_b_b[l], a_log_b[l], dt_bias_b[l], d_skip_b[l], norm_b[l],
                   w_out[l], norm_ffn[l], w_router_group[l], b_router_group[l],
                   w_router_expert[l], b_router_expert[l], w_gate_e[l], w_up_e[l], w_down_e[l])
        xp, d1, dc1, s1, sc1 = _layer(
            xp, jnp.zeros((bp, H_A, DK_A, DV_A), f32), jnp.zeros((bp, CONV_W - 1, D_QKV_A), f32),
            jnp.zeros((bp, H_B, P_B, N_B), f32), jnp.zeros((bp, CONV_W - 1, D_XBC), f32), *weights)
        xs, d2, dc2, s2, sc2 = _layer(
            xs, state_delta[l], state_delta_conv[l], state_ssm[l], state_ssm_conv[l], *weights)
        dp.append(d1); dcp.append(dc1); sp.append(s1); scp.append(sc1)
        ds.append(d2); dcs.append(dc2); ss.append(s2); scs.append(sc2)
    y_prompt = _rmsnorm(xp, norm_final).astype(x_prompt.dtype)
    y_sample = _rmsnorm(xs, norm_final).astype(x_sample.dtype)
    return (y_prompt, y_sample,
            jnp.stack(dp), jnp.stack(dcp), jnp.stack(sp), jnp.stack(scp),
            jnp.stack(ds), jnp.stack(dcs), jnp.stack(ss), jnp.stack(scs))
```

```python
import functools

import numpy as np
import jax
import jax.numpy as jnp
from jax import lax
from jax.experimental import pallas as pl
from jax.experimental.pallas import tpu as pltpu

F32 = jnp.float32
BF16 = jnp.bfloat16

D_MODEL = 1024
H_A, DK_A, DV_A = 8, 128, 128
D_QK_A = H_A * DK_A
D_V_A = H_A * DV_A
D_QKV_A = 2 * D_QK_A + D_V_A
H_B, P_B, N_B, G_B = 16, 64, 128, 2
D_B = H_B * P_B
D_XBC = D_B + 2 * G_B * N_B
CONV_W = 4
N_GROUPS_E, EXPERTS_PER_GROUP = 4, 4
N_EXPERTS = N_GROUPS_E * EXPERTS_PER_GROUP
D_EXPERT = 512
EPS = 1e-6

D_CONV = D_QKV_A + D_XBC
OFF_GATE = D_CONV
OFF_Z = OFF_GATE + D_V_A
OFF_SMALL = OFF_Z + D_B
D_PROJ = OFF_SMALL + 128
LANE_G = H_A
LANE_DT = 2 * H_A

CHUNK = 64
SEQ_ROWS = 8
PAD_FRONT = CONV_W - 1
VMEM_LIMIT = 52 * 1024 * 1024


def _bf(x):
    return x.astype(BF16)


def _mm(a, b):
    return jnp.dot(_bf(a), _bf(b), preferred_element_type=F32)


def _mm_nt(a, b):
    return lax.dot_general(_bf(a), _bf(b), (((1,), (1,)), ((), ())), preferred_element_type=F32)


def _mm_tn(a, b):
    return lax.dot_general(_bf(a), _bf(b), (((0,), (0,)), ((), ())), preferred_element_type=F32)


def _split3(x):
    x1 = _bf(x)
    r1 = x - x1.astype(F32)
    x2 = _bf(r1)
    x3 = _bf(r1 - x2.astype(F32))
    return x1, x2, x3


def _mm01_r(x, m01):
    x1, x2, x3 = _split3(x)
    d = functools.partial(jnp.dot, preferred_element_type=F32)
    return d(x1, m01) + d(x2, m01) + d(x3, m01)


def _mm01_l(m01, x):
    x1, x2, x3 = _split3(x)
    d = functools.partial(jnp.dot, preferred_element_type=F32)
    return d(m01, x1) + d(m01, x2) + d(m01, x3)


def _rms(x, w):
    return x * lax.rsqrt(jnp.mean(x * x, axis=-1, keepdims=True) + EPS) * w


def _silu(x):
    return x * jax.nn.sigmoid(x)


def _softplus(x):
    return jnp.maximum(x, 0.0) + jnp.log1p(jnp.exp(-jnp.abs(x)))


def _inproj_kernel(x_ref, nw_ref, w_ref, o_ref):
    h = _rms(x_ref[...], nw_ref[...])
    o_ref[...] = jnp.dot(_bf(h), w_ref[...], preferred_element_type=F32)


def _inproj(x2d, norm_w, w_all, tm):
    t, d = x2d.shape
    n = w_all.shape[1]
    return pl.pallas_call(
        _inproj_kernel,
        grid=(t // tm,),
        in_specs=[
            pl.BlockSpec((tm, d), lambda i: (i, 0)),
            pl.BlockSpec((1, d), lambda i: (0, 0)),
            pl.BlockSpec((d, n), lambda i: (0, 0), pipeline_mode=pl.Buffered(1)),
        ],
        out_specs=pl.BlockSpec((tm, n), lambda i: (i, 0)),
        out_shape=jax.ShapeDtypeStruct((t, n), F32),
        compiler_params=pltpu.CompilerParams(dimension_semantics=("parallel",), vmem_limit_bytes=VMEM_LIMIT),
        name="inproj",
    )(x2d, norm_w, w_all)


def _inv_unit_lower(a, ri, ci, eye, merge_sizes):
    ad = jnp.where((ri >> 3) == (ci >> 3), a, 0.0)
    x = eye - ad
    a2 = _mm(ad, ad)
    x = x + _mm(x, a2)
    a4 = _mm(a2, a2)
    x = x + _mm(x, a4)
    for s in merge_sizes:
        sh = s.bit_length() - 1
        rb = ri >> sh
        off = jnp.where(((rb & 1) == 1) & ((ci >> sh) == rb - 1), a, 0.0)
        x = x - _mm(_mm(x, off), x)
    return x


def _mixer_kernel(n_seq, valid_lo, valid_hi, *refs):
    carry = n_seq == 1
    (proj_ref, cw_ref, cb_ref, bias_ref, alog_ref, na_ref, nb_ref, dsk_ref,
     ea_ref, eb_ref, ec_ref) = refs[:11]
    if carry:
        mix_ref, s_wr, h_wr, xc_ref, prev_ref = refs[11:]
        s_rd, h_rd = s_wr, h_wr

        @pl.when(pl.program_id(1) == 0)
        def _():
            s_wr[...] = jnp.zeros_like(s_wr)
            h_wr[...] = jnp.zeros_like(h_wr)
            prev_ref[...] = jnp.zeros_like(prev_ref)
    else:
        s_rd, h_rd, mix_ref, s_wr, h_wr, xc_ref = refs[11:]
        prev_ref = None

    c = CHUNK
    r_seq = c // n_seq
    row = lax.broadcasted_iota(jnp.int32, (c, 1), 0)
    lane = lax.broadcasted_iota(jnp.int32, (1, 128), 1)
    ri = lax.broadcasted_iota(jnp.int32, (c, c), 0)
    ci = lax.broadcasted_iota(jnp.int32, (c, c), 1)
    eye = (ri == ci).astype(F32)
    if carry:
        causal = ri >= ci
        strict = ri > ci
        valid = None
        merge_sizes = (8, 16, 32)
    else:
        sh = r_seq.bit_length() - 1
        same = (ri >> sh) == (ci >> sh)
        causal = (ri >= ci) & same
        strict = (ri > ci) & same
        rr = row & (r_seq - 1)
        valid = ((rr >= valid_lo) & (rr < valid_hi)).astype(F32)
        merge_sizes = ()

    row8 = lax.broadcasted_iota(jnp.int32, (8, 1), 0)
    for c0 in range(0, D_CONV, 512):
        sl = slice(c0, c0 + 512)
        u = proj_ref[:, sl]
        acc = u * cw_ref[CONV_W - 1:CONV_W, sl] + cb_ref[:, sl]
        for k in range(1, CONV_W):
            ru = pltpu.roll(u, k, 0)
            if carry:
                rp = pltpu.roll(prev_ref[:, sl], k, 0)
                first = jnp.where(row8 < k, rp, ru[0:8])
                ru = jnp.concatenate([first, ru[8:]], axis=0)
            acc = acc + ru * cw_ref[CONV_W - 1 - k:CONV_W - k, sl]
        if carry:
            prev_ref[:, sl] = u[c - 8:c]
        xc_ref[:, sl] = _silu(acc)

    raw = proj_ref[:, OFF_SMALL:OFF_SMALL + 128]
    sp = _softplus(raw + bias_ref[...])
    coef = -jnp.exp(alog_ref[...])
    is_beta = lane < LANE_G
    is_g = (lane >= LANE_G) & (lane < LANE_DT)
    is_dt = (lane >= LANE_DT) & (lane < LANE_DT + H_B)
    gd = jnp.where(is_g | is_dt, sp * coef, 0.0)
    q1 = jnp.where(is_beta, jax.nn.sigmoid(raw), jnp.where(is_dt, sp, 0.0))
    if valid is not None:
        gd = gd * valid
        q1 = q1 * valid
    gc = _mm01_l(_bf(causal.astype(F32)), gd)
    if carry:
        tot = jnp.broadcast_to(gc[c - 1:c, :], (c, 128))
    else:
        tot = _mm01_l(_bf(same.astype(F32)), gd)
    eg = jnp.exp(gc)
    et = jnp.exp(tot - gc)
    etot = jnp.exp(tot)
    xa = _mm01_r(jnp.concatenate([q1, jnp.where(is_g, eg, 0.0), jnp.where(is_g, et, 0.0)], axis=0), ea_ref[...])
    beta_x, egc_x, tail_x = xa[0:c], xa[c:2 * c], xa[2 * c:3 * c]
    xb = _mm01_r(jnp.concatenate([q1, eg, et], axis=0), eb_ref[...])
    dt_x, edac_x, tailb_x = xb[0:c], xb[c:2 * c], xb[2 * c:3 * c]
    dec_x = _mm01_r(etot, ec_ref[...])
    gct = gc.T

    for h in range(H_A):
        hs = slice(128 * h, 128 * (h + 1))
        q = xc_ref[:, 128 * h:128 * (h + 1)]
        k = xc_ref[:, D_QK_A + 128 * h:D_QK_A + 128 * (h + 1)]
        v = xc_ref[:, 2 * D_QK_A + 128 * h:2 * D_QK_A + 128 * (h + 1)]
        q = q * lax.rsqrt(jnp.sum(q * q, axis=-1, keepdims=True) + EPS) * (DK_A ** -0.5)
        k = k * lax.rsqrt(jnp.sum(k * k, axis=-1, keepdims=True) + EPS)
        bh, egh, tlh = beta_x[:, hs], egc_x[:, hs], tail_x[:, hs]
        diff = gc[:, LANE_G + h:LANE_G + h + 1] - gct[LANE_G + h:LANE_G + h + 1, :]
        dec = jnp.exp(jnp.where(causal, diff, -jnp.inf))
        kb = k * bh
        nt = _mm_nt(jnp.concatenate([kb, q], axis=0), k)
        a = jnp.where(strict, nt[:c] * dec, 0.0)
        qk = nt[c:] * dec
        x_inv = _inv_unit_lower(a, ri, ci, eye, merge_sizes)
        sol = _mm(x_inv, jnp.concatenate([v * bh, kb * egh], axis=1))
        u_, w_ = sol[:, :128], sol[:, 128:]
        qe = q * egh
        vnews, obs = [], []
        for s in range(n_seq):
            rows = slice(r_seq * s, r_seq * (s + 1))
            ws = _mm(jnp.concatenate([w_[rows], qe[rows]], axis=0), s_rd[s, h])
            vnews.append(u_[rows] - ws[:r_seq])
            obs.append(ws[r_seq:])
        vnew = vnews[0] if n_seq == 1 else jnp.concatenate(vnews, axis=0)
        ob = obs[0] if n_seq == 1 else jnp.concatenate(obs, axis=0)
        o = ob + _mm(qk, vnew)
        kt = k * tlh
        for s in range(n_seq):
            kts = kt if n_seq == 1 else jnp.where((row >> (r_seq.bit_length() - 1)) == s, kt, 0.0)
            s_wr[s, h] = s_rd[s, h] * dec_x[r_seq * s:r_seq * s + 1, hs] + _mm_tn(kts, vnew)
        gate = proj_ref[:, OFF_GATE + 128 * h:OFF_GATE + 128 * (h + 1)]
        mix_ref[:, hs] = _bf(_rms(o, na_ref[:, hs]) * _silu(gate))

    hg = H_B // G_B
    wg = hg * P_B
    off_x = D_QKV_A
    off_b = D_QKV_A + D_B
    off_c = off_b + G_B * N_B
    for g in range(G_B):
        gs = slice(wg * g, wg * (g + 1))
        bg = xc_ref[:, off_b + N_B * g:off_b + N_B * (g + 1)]
        cg = xc_ref[:, off_c + N_B * g:off_c + N_B * (g + 1)]
        xs_g = xc_ref[:, off_x + wg * g:off_x + wg * (g + 1)]
        xdt = xs_g * dt_x[:, gs]
        cb = _mm_nt(cg, bg)
        ypairs = []
        for p in range(hg // 2):
            xp = xdt[:, 128 * p:128 * (p + 1)]
            ys = []
            for a_ in (0, 1):
                ln = LANE_DT + hg * g + 2 * p + a_
                seg = jnp.exp(jnp.where(causal, gc[:, ln:ln + 1] - gct[ln:ln + 1, :], -jnp.inf))
                ys.append(_mm(cb * seg, xp))
            ypairs.append(jnp.where(lane < P_B, ys[0], ys[1]))
        y = jnp.concatenate(ypairs, axis=1)
        yoffs = []
        for s in range(n_seq):
            rows = slice(r_seq * s, r_seq * (s + 1))
            yoffs.append(_mm_nt(cg[rows], h_rd[s, wg * g:wg * (g + 1), :]))
        yoff = yoffs[0] if n_seq == 1 else jnp.concatenate(yoffs, axis=0)
        y = y + yoff * edac_x[:, gs] + dsk_ref[:, gs] * xs_g
        xt = xdt * tailb_x[:, gs]
        for s in range(n_seq):
            xts = xt if n_seq == 1 else jnp.where((row >> (r_seq.bit_length() - 1)) == s, xt, 0.0)
            st = _mm_tn(xts, bg)
            for j in range(hg):
                hh = hg * g + j
                rs = slice(P_B * hh, P_B * (hh + 1))
                drow = dec_x[r_seq * s:r_seq * s + 1, 128 * (H_A + hh):128 * (H_A + hh + 1)]
                h_wr[s, rs, :] = h_rd[s, rs, :] * drow + st[P_B * j:P_B * (j + 1), :]
        z = proj_ref[:, OFF_Z + wg * g:OFF_Z + wg * (g + 1)]
        mix_ref[:, D_V_A + wg * g:D_V_A + wg * (g + 1)] = _bf(_rms(y * _silu(z), nb_ref[:, gs]))


def _expand_mats():
    ea = np.zeros((128, 128 * H_A), np.float32)
    for h in range(H_A):
        ea[h, 128 * h:128 * (h + 1)] = 1.0
        ea[LANE_G + h, 128 * h:128 * (h + 1)] = 1.0
    eb = np.zeros((128, D_B), np.float32)
    for h in range(H_B):
        eb[LANE_DT + h, P_B * h:P_B * (h + 1)] = 1.0
    ec = np.zeros((128, 128 * (H_A + H_B)), np.float32)
    for h in range(H_A):
        ec[LANE_G + h, 128 * h:128 * (h + 1)] = 1.0
    for h in range(H_B):
        ec[LANE_DT + h, 128 * (H_A + h):128 * (H_A + h + 1)] = 1.0
    return jnp.asarray(ea, BF16), jnp.asarray(eb, BF16), jnp.asarray(ec, BF16)


def _const_spec(shape):
    return pl.BlockSpec(shape, lambda *_: (0,) * len(shape))


def _mixer_params(p):
    ea, eb, ec = _expand_mats()
    return (p["conv_w"], p["conv_b"], p["bias_slab"], p["alog_slab"], p["norm_a_x"], p["norm_b"], p["dskip_x"],
            ea, eb, ec)


def _mixer_prompt(proj, p):
    b, l, n = proj.shape
    consts = _mixer_params(p)
    const_specs = [_const_spec(a.shape) for a in consts]
    return pl.pallas_call(
        functools.partial(_mixer_kernel, 1, 0, CHUNK),
        grid=(b, l // CHUNK),
        in_specs=[pl.BlockSpec((None, CHUNK, n), lambda i, t: (i, t, 0))] + const_specs,
        out_specs=[
            pl.BlockSpec((None, CHUNK, D_V_A + D_B), lambda i, t: (i, t, 0)),
            pl.BlockSpec((1, H_A, DK_A, DV_A), lambda i, t: (i, 0, 0, 0)),
            pl.BlockSpec((1, D_B, N_B), lambda i, t: (i, 0, 0)),
        ],
        out_shape=[
            jax.ShapeDtypeStruct((b, l, D_V_A + D_B), BF16),
            jax.ShapeDtypeStruct((b, H_A, DK_A, DV_A), F32),
            jax.ShapeDtypeStruct((b, D_B, N_B), F32),
        ],
        scratch_shapes=[pltpu.VMEM((CHUNK, D_CONV), F32), pltpu.VMEM((8, D_CONV), F32)],
        compiler_params=pltpu.CompilerParams(dimension_semantics=("parallel", "arbitrary"),
                                             vmem_limit_bytes=VMEM_LIMIT),
        name="mixer_prompt",
    )(proj, *consts)


def _mixer_decode(slab, s_delta, s_ssm, p, l_dec):
    rows, n = slab.shape
    n_seq = CHUNK // SEQ_ROWS
    consts = _mixer_params(p)
    const_specs = [_const_spec(a.shape) for a in consts]
    bs = s_delta.shape[0]
    return pl.pallas_call(
        functools.partial(_mixer_kernel, n_seq, PAD_FRONT, PAD_FRONT + l_dec),
        grid=(rows // CHUNK,),
        in_specs=[pl.BlockSpec((CHUNK, n), lambda i: (i, 0))] + const_specs + [
            pl.BlockSpec((n_seq, H_A, DK_A, DV_A), lambda i: (i, 0, 0, 0)),
            pl.BlockSpec((n_seq, D_B, N_B), lambda i: (i, 0, 0)),
        ],
        out_specs=[
            pl.BlockSpec((CHUNK, D_V_A + D_B), lambda i: (i, 0)),
            pl.BlockSpec((n_seq, H_A, DK_A, DV_A), lambda i: (i, 0, 0, 0)),
            pl.BlockSpec((n_seq, D_B, N_B), lambda i: (i, 0, 0)),
        ],
        out_shape=[
            jax.ShapeDtypeStruct((rows, D_V_A + D_B), BF16),
            jax.ShapeDtypeStruct((bs, H_A, DK_A, DV_A), F32),
            jax.ShapeDtypeStruct((bs, D_B, N_B), F32),
        ],
        scratch_shapes=[pltpu.VMEM((CHUNK, D_CONV), F32)],
        compiler_params=pltpu.CompilerParams(dimension_semantics=("parallel",), vmem_limit_bytes=VMEM_LIMIT),
        name="mixer_decode",
    )(slab, *consts, s_delta, s_ssm)


def _outproj_router_kernel(mix_ref, x_ref, wo_ref, nf_ref, wr_ref, br_ref, xmid_ref, route_ref):
    xm = x_ref[...] + jnp.dot(mix_ref[...], wo_ref[...], preferred_element_type=F32)
    xmid_ref[...] = xm
    t = _rms(xm, nf_ref[...])
    t1 = _bf(t)
    t2 = _bf(t - t1.astype(F32))
    w = wr_ref[...]
    w1 = _bf(w)
    w2 = _bf(w - w1.astype(F32))
    d = functools.partial(jnp.dot, preferred_element_type=F32)
    logit = d(t1, w1) + d(t1, w2) + d(t2, w1) + br_ref[...]

    lane = lax.broadcasted_iota(jnp.int32, (1, 128), 1).astype(F32)
    neg = -jnp.inf
    big = 1e9
    is_grp = lane < N_GROUPS_E
    gl = jnp.where(is_grp, logit, neg)
    gmax = jnp.max(gl, axis=-1, keepdims=True)
    gsel = jnp.min(jnp.where(gl == gmax, lane, big), axis=-1, keepdims=True)
    gw = 1.0 / jnp.sum(jnp.exp(jnp.where(is_grp, logit - gmax, neg)), axis=-1, keepdims=True)
    lo = N_GROUPS_E + EXPERTS_PER_GROUP * gsel
    el = jnp.where((lane >= lo) & (lane < lo + EXPERTS_PER_GROUP), logit, neg)
    v1 = jnp.max(el, axis=-1, keepdims=True)
    i1 = jnp.min(jnp.where(el == v1, lane, big), axis=-1, keepdims=True)
    el2 = jnp.where(lane == i1, neg, el)
    v2 = jnp.max(el2, axis=-1, keepdims=True)
    i2 = jnp.min(jnp.where(el2 == v2, lane, big), axis=-1, keepdims=True)
    e = jnp.exp(v2 - v1)
    den = gw / (1.0 + e)
    comb = jnp.where(lane == i1, den, 0.0) + jnp.where(lane == i2, e * den, 0.0)
    route_ref[...] = jnp.where(lane == 0.0, gsel, comb)


def _outproj_router(mix, x2d, p, tm):
    t, d = x2d.shape
    return pl.pallas_call(
        _outproj_router_kernel,
        grid=(t // tm,),
        in_specs=[
            pl.BlockSpec((tm, mix.shape[1]), lambda i: (i, 0)),
            pl.BlockSpec((tm, d), lambda i: (i, 0)),
            _const_spec(p["w_out"].shape),
            _const_spec((1, d)),
            _const_spec(p["w_router"].shape),
            _const_spec((1, 128)),
        ],
        out_specs=[pl.BlockSpec((tm, d), lambda i: (i, 0)), pl.BlockSpec((tm, 128), lambda i: (i, 0))],
        out_shape=[jax.ShapeDtypeStruct((t, d), F32), jax.ShapeDtypeStruct((t, 128), F32)],
        compiler_params=pltpu.CompilerParams(dimension_semantics=("parallel",), vmem_limit_bytes=VMEM_LIMIT),
        name="outproj_router",
    )(mix, x2d, p["w_out"], p["norm_ffn"], p["w_router"], p["b_router"])


def _moe_kernel(xmid_ref, route_ref, nffn_ref, nfin_ref, wg_ref, wu_ref, wd_ref, y_ref, t_scr, acc_scr):
    e = pl.program_id(1)

    @pl.when(e == 0)
    def _():
        t_scr[...] = _bf(_rms(xmid_ref[...], nffn_ref[...]))
        acc_scr[...] = jnp.zeros_like(acc_scr)

    t = t_scr[...]
    hgate = jnp.dot(t, wg_ref[...], preferred_element_type=F32)
    hup = jnp.dot(t, wu_ref[...], preferred_element_type=F32)
    lane = lax.broadcasted_iota(jnp.int32, (1, 128), 1)
    cw = jnp.sum(jnp.where(lane == N_GROUPS_E + e, route_ref[...], 0.0), axis=-1, keepdims=True)
    acc_scr[...] += jnp.dot(_bf(_silu(hgate) * hup * cw), wd_ref[...], preferred_element_type=F32)

    @pl.when(e == N_EXPERTS - 1)
    def _():
        y_ref[...] = _rms(xmid_ref[...] + acc_scr[...], nfin_ref[...])


def _moe(xmid, route, p, tm):
    t, d = xmid.shape
    return pl.pallas_call(
        _moe_kernel,
        grid=(t // tm, N_EXPERTS),
        in_specs=[
            pl.BlockSpec((tm, d), lambda i, e: (i, 0)),
            pl.BlockSpec((tm, 128), lambda i, e: (i, 0)),
            _const_spec((1, d)),
            _const_spec((1, d)),
            pl.BlockSpec((None, d, D_EXPERT), lambda i, e: (e, 0, 0)),
            pl.BlockSpec((None, d, D_EXPERT), lambda i, e: (e, 0, 0)),
            pl.BlockSpec((None, D_EXPERT, d), lambda i, e: (e, 0, 0)),
        ],
        out_specs=pl.BlockSpec((tm, d), lambda i, e: (i, 0)),
        out_shape=jax.ShapeDtypeStruct((t, d), F32),
        scratch_shapes=[pltpu.VMEM((tm, d), BF16), pltpu.VMEM((tm, d), F32)],
        compiler_params=pltpu.CompilerParams(dimension_semantics=("parallel", "arbitrary"),
                                             vmem_limit_bytes=VMEM_LIMIT),
        name="moe",
    )(xmid, route, p["norm_ffn"], p["norm_final"], p["w_gate"], p["w_up"], p["w_down"])


def _pick_tile(t, pref):
    tm = min(pref, t)
    while t % tm:
        tm //= 2
    return tm


def _prep_layer(l, norm_mix, w_in, conv_a_w, a_log_a, dt_bias_a, norm_a, conv_b_w, conv_b_b, a_log_b, dt_bias_b,
                d_skip_b, norm_b, w_out, norm_ffn, w_router_group, b_router_group, w_router_expert,
                b_router_expert, w_gate_e, w_up_e, w_down_e, norm_final):
    w = w_in[l]
    cuts = np.cumsum([D_QKV_A, D_V_A, H_A, H_A, D_B, D_XBC]).tolist()
    w_qkv, w_gate, w_beta, w_alpha, w_z, w_xbc, w_dt = jnp.split(w, cuts, axis=1)
    w_small = jnp.concatenate([w_beta, w_alpha, w_dt, jnp.zeros((D_MODEL, 128 - 2 * H_A - H_B), F32)], axis=1)
    w_all = _bf(jnp.concatenate([w_qkv, w_xbc, w_gate, w_z, w_small], axis=1))

    def slab(a8, b16):
        return jnp.concatenate([jnp.zeros((H_A,), F32), a8.astype(F32), b16.astype(F32),
                                jnp.zeros((128 - 2 * H_A - H_B,), F32)]).reshape(1, 128)

    w_router = jnp.concatenate(
        [w_router_group[l].astype(F32), w_router_expert[l].reshape(D_MODEL, N_EXPERTS).astype(F32),
         jnp.zeros((D_MODEL, 128 - N_GROUPS_E - N_EXPERTS), F32)], axis=1)
    b_router = jnp.concatenate(
        [b_router_group[l].astype(F32), b_router_expert[l].reshape(N_EXPERTS).astype(F32),
         jnp.zeros((128 - N_GROUPS_E - N_EXPERTS,), F32)]).reshape(1, 128)
    return {
        "norm_mix": norm_mix[l].astype(F32).reshape(1, D_MODEL),
        "w_all": w_all,
        "conv_w": jnp.concatenate([conv_a_w[l], conv_b_w[l]], axis=1).astype(F32),
        "conv_b": jnp.concatenate([jnp.zeros((D_QKV_A,), F32), conv_b_b[l].astype(F32)]).reshape(1, D_CONV),
        "bias_slab": slab(dt_bias_a[l], dt_bias_b[l]),
        "alog_slab": slab(a_log_a[l], a_log_b[l]),
        "norm_a_x": jnp.tile(norm_a[l].astype(F32), H_A).reshape(1, D_V_A),
        "norm_b": norm_b[l].astype(F32).reshape(1, D_B),
        "dskip_x": jnp.repeat(d_skip_b[l].astype(F32), P_B).reshape(1, D_B),
        "w_out": _bf(w_out[l]),
        "norm_ffn": norm_ffn[l].astype(F32).reshape(1, D_MODEL),
        "w_router": w_router,
        "b_router": b_router,
        "w_gate": _bf(w_gate_e[l].reshape(N_EXPERTS, D_MODEL, D_EXPERT)),
        "w_up": _bf(w_up_e[l].reshape(N_EXPERTS, D_MODEL, D_EXPERT)),
        "w_down": _bf(w_down_e[l].reshape(N_EXPERTS, D_EXPERT, D_MODEL)),
        "norm_final": norm_final.astype(F32).reshape(1, D_MODEL),
    }


def _ffn_tail(mix2d, x2d, p):
    t = x2d.shape[0]
    xmid, route = _outproj_router(mix2d, x2d, p, _pick_tile(t, 512))
    return _moe(xmid, route, p, _pick_tile(t, 1024))


def kernel(x_prompt, x_sample, state_delta, state_delta_conv, state_ssm, state_ssm_conv, norm_mix, w_in, conv_a_w,
           a_log_a, dt_bias_a, norm_a, conv_b_w, conv_b_b, a_log_b, dt_bias_b, d_skip_b, norm_b, w_out, norm_ffn,
           w_router_group, b_router_group, w_router_expert, b_router_expert, w_gate_e, w_up_e, w_down_e,
           norm_final):
    depth = w_in.shape[0]
    assert depth == 1, "the fused final norm assumes a single layer"
    bp, lp, d = x_prompt.shape
    bs, ls, _ = x_sample.shape
    assert lp % CHUNK == 0 and lp >= CONV_W - 1
    assert PAD_FRONT + ls <= SEQ_ROWS and bs % (CHUNK // SEQ_ROWS) == 0
    l = 0
    p = _prep_layer(l, norm_mix, w_in, conv_a_w, a_log_a, dt_bias_a, norm_a, conv_b_w, conv_b_b, a_log_b,
                    dt_bias_b, d_skip_b, norm_b, w_out, norm_ffn, w_router_group, b_router_group,
                    w_router_expert, b_router_expert, w_gate_e, w_up_e, w_down_e, norm_final)

    xp2 = x_prompt.reshape(bp * lp, d)
    proj_p = _inproj(xp2, p["norm_mix"], p["w_all"], _pick_tile(bp * lp, 256)).reshape(bp, lp, D_PROJ)
    mix_p, delta_p, ssm_p = _mixer_prompt(proj_p, p)
    dconv_p = proj_p[:, lp - (CONV_W - 1):, :D_QKV_A]
    sconv_p = proj_p[:, lp - (CONV_W - 1):, D_QKV_A:D_CONV]
    y_p = _ffn_tail(mix_p.reshape(bp * lp, -1), xp2, p).reshape(bp, lp, d)

    xs2 = x_sample.reshape(bs * ls, d)
    proj_s = _inproj(xs2, p["norm_mix"], p["w_all"], _pick_tile(bs * ls, 256)).reshape(bs, ls, D_PROJ)
    conv_rows = jnp.concatenate(
        [state_delta_conv[l].astype(F32), state_ssm_conv[l].astype(F32),
         jnp.zeros((bs, CONV_W - 1, D_PROJ - D_CONV), F32)], axis=-1)
    slab = jnp.concatenate(
        [conv_rows, proj_s, jnp.zeros((bs, SEQ_ROWS - PAD_FRONT - ls, D_PROJ), F32)], axis=1)
    mix_s, delta_s, ssm_s = _mixer_decode(slab.reshape(bs * SEQ_ROWS, D_PROJ), state_delta[l].astype(F32),
                                          state_ssm[l].astype(F32).reshape(bs, D_B, N_B), p, ls)
    mix_s = mix_s.reshape(bs, SEQ_ROWS, -1)[:, PAD_FRONT:PAD_FRONT + ls].reshape(bs * ls, -1)
    dconv_s = slab[:, ls:ls + CONV_W - 1, :D_QKV_A]
    sconv_s = slab[:, ls:ls + CONV_W - 1, D_QKV_A:D_CONV]
    y_s = _ffn_tail(mix_s, xs2, p).reshape(bs, ls, d)

    return (y_p.astype(x_prompt.dtype), y_s.astype(x_sample.dtype),
            delta_p[None], dconv_p[None], ssm_p.reshape(bp, H_B, P_B, N_B)[None], sconv_p[None],
            delta_s[None], dconv_s[None], ssm_s.reshape(bs, H_B, P_B, N_B)[None], sconv_s[None])
```

```python
import functools

import numpy as np
import jax
import jax.numpy as jnp
from jax import lax
from jax.experimental import pallas as pl
from jax.experimental.pallas import tpu as pltpu

F32 = jnp.float32
BF16 = jnp.bfloat16

D_MODEL = 1024
H_A, DK_A, DV_A = 8, 128, 128
D_QK_A = H_A * DK_A
D_V_A = H_A * DV_A
D_QKV_A = 2 * D_QK_A + D_V_A
H_B, P_B, N_B, G_B = 16, 64, 128, 2
D_B = H_B * P_B
D_XBC = D_B + 2 * G_B * N_B
CONV_W = 4
N_GROUPS_E, EXPERTS_PER_GROUP = 4, 4
N_EXPERTS = N_GROUPS_E * EXPERTS_PER_GROUP
D_EXPERT = 512
EPS = 1e-6

D_CONV = D_QKV_A + D_XBC
OFF_GATE = D_CONV
OFF_Z = OFF_GATE + D_V_A
OFF_SMALL = OFF_Z + D_B
D_PROJ = OFF_SMALL + 128
LANE_G = H_A
LANE_DT = 2 * H_A

CHUNK = 64
SEQ_ROWS = 8
PAD_FRONT = CONV_W - 1
VMEM_LIMIT = 52 * 1024 * 1024


def _bf(x):
    return x.astype(BF16)


def _mm(a, b):
    return jnp.dot(_bf(a), _bf(b), preferred_element_type=F32)


def _mm_nt(a, b):
    return lax.dot_general(_bf(a), _bf(b), (((1,), (1,)), ((), ())), preferred_element_type=F32)


def _mm_tn(a, b):
    return lax.dot_general(_bf(a), _bf(b), (((0,), (0,)), ((), ())), preferred_element_type=F32)


def _split3(x):
    x1 = _bf(x)
    r1 = x - x1.astype(F32)
    x2 = _bf(r1)
    x3 = _bf(r1 - x2.astype(F32))
    return x1, x2, x3


def _mm01_r(x, m01):
    x1, x2, x3 = _split3(x)
    d = functools.partial(jnp.dot, preferred_element_type=F32)
    return d(x1, m01) + d(x2, m01) + d(x3, m01)


def _mm01_l(m01, x):
    x1, x2, x3 = _split3(x)
    d = functools.partial(jnp.dot, preferred_element_type=F32)
    return d(m01, x1) + d(m01, x2) + d(m01, x3)


def _rms(x, w):
    return x * lax.rsqrt(jnp.mean(x * x, axis=-1, keepdims=True) + EPS) * w


def _silu(x):
    return x * jax.nn.sigmoid(x)


def _softplus(x):
    return jnp.maximum(x, 0.0) + jnp.log1p(jnp.exp(-jnp.abs(x)))


def _inproj_kernel(x_ref, nw_ref, w_ref, o_ref):
    h = _rms(x_ref[...], nw_ref[...])
    o_ref[...] = jnp.dot(_bf(h), w_ref[...], preferred_element_type=F32)


def _inproj(x2d, norm_w, w_all, tm):
    t, d = x2d.shape
    n = w_all.shape[1]
    return pl.pallas_call(
        _inproj_kernel,
        grid=(t // tm,),
        in_specs=[
            pl.BlockSpec((tm, d), lambda i: (i, 0)),
            pl.BlockSpec((1, d), lambda i: (0, 0)),
            pl.BlockSpec((d, n), lambda i: (0, 0), pipeline_mode=pl.Buffered(1)),
        ],
        out_specs=pl.BlockSpec((tm, n), lambda i: (i, 0)),
        out_shape=jax.ShapeDtypeStruct((t, n), F32),
        compiler_params=pltpu.CompilerParams(dimension_semantics=("parallel",), vmem_limit_bytes=VMEM_LIMIT),
        name="inproj",
    )(x2d, norm_w, w_all)


def _inv_unit_lower(mats, ri, ci, eye, merge_sizes):
    blk = (ri >> 3) == (ci >> 3)
    ads = [jnp.where(blk, a, 0.0) for a in mats]
    xs = [eye - ad for ad in ads]
    a2s = [_mm(ad, ad) for ad in ads]
    xs = [x + _mm(x, a2) for x, a2 in zip(xs, a2s)]
    a4s = [_mm(a2, a2) for a2 in a2s]
    xs = [x + _mm(x, a4) for x, a4 in zip(xs, a4s)]
    for s in merge_sizes:
        sh = s.bit_length() - 1
        rb = ri >> sh
        sel = ((rb & 1) == 1) & ((ci >> sh) == rb - 1)
        ts = [_mm(x, jnp.where(sel, a, 0.0)) for x, a in zip(xs, mats)]
        xs = [x - _mm(t, x) for x, t in zip(xs, ts)]
    return xs


def _mixer_kernel(n_seq, valid_lo, valid_hi, *refs):
    carry = n_seq == 1
    (proj_ref, cw_ref, cb_ref, bias_ref, alog_ref, na_ref, nb_ref, dsk_ref,
     ea_ref, eb_ref, ec_ref) = refs[:11]
    if carry:
        mix_ref, s_wr, h_wr, xc_ref, prev_ref = refs[11:]
        s_rd, h_rd = s_wr, h_wr

        @pl.when(pl.program_id(1) == 0)
        def _():
            s_wr[...] = jnp.zeros_like(s_wr)
            h_wr[...] = jnp.zeros_like(h_wr)
            prev_ref[...] = jnp.zeros_like(prev_ref)
    else:
        s_rd, h_rd, mix_ref, s_wr, h_wr, xc_ref = refs[11:]
        prev_ref = None

    c = CHUNK
    r_seq = c // n_seq
    row = lax.broadcasted_iota(jnp.int32, (c, 1), 0)
    lane = lax.broadcasted_iota(jnp.int32, (1, 128), 1)
    ri = lax.broadcasted_iota(jnp.int32, (c, c), 0)
    ci = lax.broadcasted_iota(jnp.int32, (c, c), 1)
    eye = (ri == ci).astype(F32)
    if carry:
        causal = ri >= ci
        strict = ri > ci
        valid = None
        merge_sizes = (8, 16, 32)
    else:
        sh = r_seq.bit_length() - 1
        same = (ri >> sh) == (ci >> sh)
        causal = (ri >= ci) & same
        strict = (ri > ci) & same
        rr = row & (r_seq - 1)
        valid = ((rr >= valid_lo) & (rr < valid_hi)).astype(F32)
        merge_sizes = ()

    row8 = lax.broadcasted_iota(jnp.int32, (8, 1), 0)
    for c0 in range(0, D_CONV, 512):
        sl = slice(c0, c0 + 512)
        u = proj_ref[:, sl]
        acc = u * cw_ref[CONV_W - 1:CONV_W, sl] + cb_ref[:, sl]
        for k in range(1, CONV_W):
            ru = pltpu.roll(u, k, 0)
            if carry:
                rp = pltpu.roll(prev_ref[:, sl], k, 0)
                first = jnp.where(row8 < k, rp, ru[0:8])
                ru = jnp.concatenate([first, ru[8:]], axis=0)
            acc = acc + ru * cw_ref[CONV_W - 1 - k:CONV_W - k, sl]
        if carry:
            prev_ref[:, sl] = u[c - 8:c]
        xc_ref[:, sl] = _silu(acc)

    raw = proj_ref[:, OFF_SMALL:OFF_SMALL + 128]
    sp = _softplus(raw + bias_ref[...])
    coef = -jnp.exp(alog_ref[...])
    is_beta = lane < LANE_G
    is_g = (lane >= LANE_G) & (lane < LANE_DT)
    is_dt = (lane >= LANE_DT) & (lane < LANE_DT + H_B)
    gd = jnp.where(is_g | is_dt, sp * coef, 0.0)
    q1 = jnp.where(is_beta, jax.nn.sigmoid(raw), jnp.where(is_dt, sp, 0.0))
    if valid is not None:
        gd = gd * valid
        q1 = q1 * valid
    gc = _mm01_l(_bf(causal.astype(F32)), gd)
    if carry:
        tot = jnp.broadcast_to(gc[c - 1:c, :], (c, 128))
    else:
        tot = _mm01_l(_bf(same.astype(F32)), gd)
    eg = jnp.exp(gc)
    et = jnp.exp(tot - gc)
    etot = jnp.exp(tot)
    xa = _mm01_r(jnp.concatenate([q1, jnp.where(is_g, eg, 0.0), jnp.where(is_g, et, 0.0)], axis=0), ea_ref[...])
    beta_x, egc_x, tail_x = xa[0:c], xa[c:2 * c], xa[2 * c:3 * c]
    xb = _mm01_r(jnp.concatenate([q1, eg, et], axis=0), eb_ref[...])
    dt_x, edac_x, tailb_x = xb[0:c], xb[c:2 * c], xb[2 * c:3 * c]
    dec_x = _mm01_r(etot, ec_ref[...])
    gct = gc.T

    heads = range(H_A)
    hsl = [slice(128 * h, 128 * (h + 1)) for h in heads]
    seq_of_row = row >> (r_seq.bit_length() - 1)
    qs, ks, vs = [], [], []
    for h in heads:
        q = xc_ref[:, 128 * h:128 * (h + 1)]
        k = xc_ref[:, D_QK_A + 128 * h:D_QK_A + 128 * (h + 1)]
        qs.append(q * lax.rsqrt(jnp.sum(q * q, axis=-1, keepdims=True) + EPS) * (DK_A ** -0.5))
        ks.append(k * lax.rsqrt(jnp.sum(k * k, axis=-1, keepdims=True) + EPS))
        vs.append(xc_ref[:, 2 * D_QK_A + 128 * h:2 * D_QK_A + 128 * (h + 1)])
    kbs = [ks[h] * beta_x[:, hsl[h]] for h in heads]
    nts = [_mm_nt(jnp.concatenate([kbs[h], qs[h]], axis=0), ks[h]) for h in heads]
    decs = [jnp.exp(jnp.where(causal, gc[:, LANE_G + h:LANE_G + h + 1] - gct[LANE_G + h:LANE_G + h + 1, :],
                              -jnp.inf)) for h in heads]
    a_s = [jnp.where(strict, nts[h][:c] * decs[h], 0.0) for h in heads]
    qks = [nts[h][c:] * decs[h] for h in heads]
    x_invs = _inv_unit_lower(a_s, ri, ci, eye, merge_sizes)
    sols = [_mm(x_invs[h], jnp.concatenate([vs[h] * beta_x[:, hsl[h]], kbs[h] * egc_x[:, hsl[h]]], axis=1))
            for h in heads]
    qes = [qs[h] * egc_x[:, hsl[h]] for h in heads]
    wss = [[_mm(jnp.concatenate([sols[h][r_seq * s:r_seq * (s + 1), 128:], qes[h][r_seq * s:r_seq * (s + 1)]],
                                axis=0), s_rd[s, h]) for s in range(n_seq)] for h in heads]
    vnews, obs = [], []
    for h in heads:
        vn = [sols[h][r_seq * s:r_seq * (s + 1), :128] - wss[h][s][:r_seq] for s in range(n_seq)]
        ob = [wss[h][s][r_seq:] for s in range(n_seq)]
        vnews.append(vn[0] if n_seq == 1 else jnp.concatenate(vn, axis=0))
        obs.append(ob[0] if n_seq == 1 else jnp.concatenate(ob, axis=0))
    os_ = [obs[h] + _mm(qks[h], vnews[h]) for h in heads]
    for h in heads:
        kt = ks[h] * tail_x[:, hsl[h]]
        for s in range(n_seq):
            kts = kt if n_seq == 1 else jnp.where(seq_of_row == s, kt, 0.0)
            s_wr[s, h] = s_rd[s, h] * dec_x[r_seq * s:r_seq * s + 1, hsl[h]] + _mm_tn(kts, vnews[h])
    for h in heads:
        gate = proj_ref[:, OFF_GATE + 128 * h:OFF_GATE + 128 * (h + 1)]
        mix_ref[:, hsl[h]] = _bf(_rms(os_[h], na_ref[:, hsl[h]]) * _silu(gate))

    hg = H_B // G_B
    wg = hg * P_B
    off_x = D_QKV_A
    off_b = D_QKV_A + D_B
    off_c = off_b + G_B * N_B
    for g in range(G_B):
        gs = slice(wg * g, wg * (g + 1))
        bg = xc_ref[:, off_b + N_B * g:off_b + N_B * (g + 1)]
        cg = xc_ref[:, off_c + N_B * g:off_c + N_B * (g + 1)]
        xs_g = xc_ref[:, off_x + wg * g:off_x + wg * (g + 1)]
        xdt = xs_g * dt_x[:, gs]
        cb = _mm_nt(cg, bg)
        ypairs = []
        for p in range(hg // 2):
            xp = xdt[:, 128 * p:128 * (p + 1)]
            ys = []
            for a_ in (0, 1):
                ln = LANE_DT + hg * g + 2 * p + a_
                seg = jnp.exp(jnp.where(causal, gc[:, ln:ln + 1] - gct[ln:ln + 1, :], -jnp.inf))
                ys.append(_mm(cb * seg, xp))
            ypairs.append(jnp.where(lane < P_B, ys[0], ys[1]))
        y = jnp.concatenate(ypairs, axis=1)
        yoffs = []
        for s in range(n_seq):
            rows = slice(r_seq * s, r_seq * (s + 1))
            yoffs.append(_mm_nt(cg[rows], h_rd[s, wg * g:wg * (g + 1), :]))
        yoff = yoffs[0] if n_seq == 1 else jnp.concatenate(yoffs, axis=0)
        y = y + yoff * edac_x[:, gs] + dsk_ref[:, gs] * xs_g
        xt = xdt * tailb_x[:, gs]
        for s in range(n_seq):
            xts = xt if n_seq == 1 else jnp.where((row >> (r_seq.bit_length() - 1)) == s, xt, 0.0)
            st = _mm_tn(xts, bg)
            for j in range(hg):
                hh = hg * g + j
                rs = slice(P_B * hh, P_B * (hh + 1))
                drow = dec_x[r_seq * s:r_seq * s + 1, 128 * (H_A + hh):128 * (H_A + hh + 1)]
                h_wr[s, rs, :] = h_rd[s, rs, :] * drow + st[P_B * j:P_B * (j + 1), :]
        z = proj_ref[:, OFF_Z + wg * g:OFF_Z + wg * (g + 1)]
        mix_ref[:, D_V_A + wg * g:D_V_A + wg * (g + 1)] = _bf(_rms(y * _silu(z), nb_ref[:, gs]))


def _expand_mats():
    ea = np.zeros((128, 128 * H_A), np.float32)
    for h in range(H_A):
        ea[h, 128 * h:128 * (h + 1)] = 1.0
        ea[LANE_G + h, 128 * h:128 * (h + 1)] = 1.0
    eb = np.zeros((128, D_B), np.float32)
    for h in range(H_B):
        eb[LANE_DT + h, P_B * h:P_B * (h + 1)] = 1.0
    ec = np.zeros((128, 128 * (H_A + H_B)), np.float32)
    for h in range(H_A):
        ec[LANE_G + h, 128 * h:128 * (h + 1)] = 1.0
    for h in range(H_B):
        ec[LANE_DT + h, 128 * (H_A + h):128 * (H_A + h + 1)] = 1.0
    return jnp.asarray(ea, BF16), jnp.asarray(eb, BF16), jnp.asarray(ec, BF16)


def _const_spec(shape):
    return pl.BlockSpec(shape, lambda *_: (0,) * len(shape))


def _mixer_params(p):
    ea, eb, ec = _expand_mats()
    return (p["conv_w"], p["conv_b"], p["bias_slab"], p["alog_slab"], p["norm_a_x"], p["norm_b"], p["dskip_x"],
            ea, eb, ec)


def _mixer_prompt(proj, p):
    b, l, n = proj.shape
    consts = _mixer_params(p)
    const_specs = [_const_spec(a.shape) for a in consts]
    return pl.pallas_call(
        functools.partial(_mixer_kernel, 1, 0, CHUNK),
        grid=(b, l // CHUNK),
        in_specs=[pl.BlockSpec((None, CHUNK, n), lambda i, t: (i, t, 0))] + const_specs,
        out_specs=[
            pl.BlockSpec((None, CHUNK, D_V_A + D_B), lambda i, t: (i, t, 0)),
            pl.BlockSpec((1, H_A, DK_A, DV_A), lambda i, t: (i, 0, 0, 0)),
            pl.BlockSpec((1, D_B, N_B), lambda i, t: (i, 0, 0)),
        ],
        out_shape=[
            jax.ShapeDtypeStruct((b, l, D_V_A + D_B), BF16),
            jax.ShapeDtypeStruct((b, H_A, DK_A, DV_A), F32),
            jax.ShapeDtypeStruct((b, D_B, N_B), F32),
        ],
        scratch_shapes=[pltpu.VMEM((CHUNK, D_CONV), F32), pltpu.VMEM((8, D_CONV), F32)],
        compiler_params=pltpu.CompilerParams(dimension_semantics=("parallel", "arbitrary"),
                                             vmem_limit_bytes=VMEM_LIMIT),
        name="mixer_prompt",
    )(proj, *consts)


def _mixer_decode(slab, s_delta, s_ssm, p, l_dec):
    rows, n = slab.shape
    n_seq = CHUNK // SEQ_ROWS
    consts = _mixer_params(p)
    const_specs = [_const_spec(a.shape) for a in consts]
    bs = s_delta.shape[0]
    return pl.pallas_call(
        functools.partial(_mixer_kernel, n_seq, PAD_FRONT, PAD_FRONT + l_dec),
        grid=(rows // CHUNK,),
        in_specs=[pl.BlockSpec((CHUNK, n), lambda i: (i, 0))] + const_specs + [
            pl.BlockSpec((n_seq, H_A, DK_A, DV_A), lambda i: (i, 0, 0, 0)),
            pl.BlockSpec((n_seq, D_B, N_B), lambda i: (i, 0, 0)),
        ],
        out_specs=[
            pl.BlockSpec((CHUNK, D_V_A + D_B), lambda i: (i, 0)),
            pl.BlockSpec((n_seq, H_A, DK_A, DV_A), lambda i: (i, 0, 0, 0)),
            pl.BlockSpec((n_seq, D_B, N_B), lambda i: (i, 0, 0)),
        ],
        out_shape=[
            jax.ShapeDtypeStruct((rows, D_V_A + D_B), BF16),
            jax.ShapeDtypeStruct((bs, H_A, DK_A, DV_A), F32),
            jax.ShapeDtypeStruct((bs, D_B, N_B), F32),
        ],
        scratch_shapes=[pltpu.VMEM((CHUNK, D_CONV), F32)],
        compiler_params=pltpu.CompilerParams(dimension_semantics=("parallel",), vmem_limit_bytes=VMEM_LIMIT),
        name="mixer_decode",
    )(slab, *consts, s_delta, s_ssm)


def _outproj_router_kernel(mix_ref, x_ref, wo_ref, nf_ref, wr_ref, br_ref, xmid_ref, route_ref):
    xm = x_ref[...] + jnp.dot(mix_ref[...], wo_ref[...], preferred_element_type=F32)
    xmid_ref[...] = xm
    t = _rms(xm, nf_ref[...])
    t1 = _bf(t)
    t2 = _bf(t - t1.astype(F32))
    w = wr_ref[...]
    w1 = _bf(w)
    w2 = _bf(w - w1.astype(F32))
    d = functools.partial(jnp.dot, preferred_element_type=F32)
    logit = d(t1, w1) + d(t1, w2) + d(t2, w1) + br_ref[...]

    lane = lax.broadcasted_iota(jnp.int32, (1, 128), 1).astype(F32)
    neg = -jnp.inf
    big = 1e9
    is_grp = lane < N_GROUPS_E
    gl = jnp.where(is_grp, logit, neg)
    gmax = jnp.max(gl, axis=-1, keepdims=True)
    gsel = jnp.min(jnp.where(gl == gmax, lane, big), axis=-1, keepdims=True)
    gw = 1.0 / jnp.sum(jnp.exp(jnp.where(is_grp, logit - gmax, neg)), axis=-1, keepdims=True)
    lo = N_GROUPS_E + EXPERTS_PER_GROUP * gsel
    el = jnp.where((lane >= lo) & (lane < lo + EXPERTS_PER_GROUP), logit, neg)
    v1 = jnp.max(el, axis=-1, keepdims=True)
    i1 = jnp.min(jnp.where(el == v1, lane, big), axis=-1, keepdims=True)
    el2 = jnp.where(lane == i1, neg, el)
    v2 = jnp.max(el2, axis=-1, keepdims=True)
    i2 = jnp.min(jnp.where(el2 == v2, lane, big), axis=-1, keepdims=True)
    e = jnp.exp(v2 - v1)
    den = gw / (1.0 + e)
    comb = jnp.where(lane == i1, den, 0.0) + jnp.where(lane == i2, e * den, 0.0)
    route_ref[...] = jnp.where(lane == 0.0, gsel, comb)


def _outproj_router(mix, x2d, p, tm):
    t, d = x2d.shape
    return pl.pallas_call(
        _outproj_router_kernel,
        grid=(t // tm,),
        in_specs=[
            pl.BlockSpec((tm, mix.shape[1]), lambda i: (i, 0)),
            pl.BlockSpec((tm, d), lambda i: (i, 0)),
            _const_spec(p["w_out"].shape),
            _const_spec((1, d)),
            _const_spec(p["w_router"].shape),
            _const_spec((1, 128)),
        ],
        out_specs=[pl.BlockSpec((tm, d), lambda i: (i, 0)), pl.BlockSpec((tm, 128), lambda i: (i, 0))],
        out_shape=[jax.ShapeDtypeStruct((t, d), F32), jax.ShapeDtypeStruct((t, 128), F32)],
        compiler_params=pltpu.CompilerParams(dimension_semantics=("parallel",), vmem_limit_bytes=VMEM_LIMIT),
        name="outproj_router",
    )(mix, x2d, p["w_out"], p["norm_ffn"], p["w_router"], p["b_router"])


def _moe_kernel(xmid_ref, route_ref, nffn_ref, nfin_ref, wg_ref, wu_ref, wd_ref, y_ref, t_scr, acc_scr):
    e = pl.program_id(1)

    @pl.when(e == 0)
    def _():
        t_scr[...] = _bf(_rms(xmid_ref[...], nffn_ref[...]))
        acc_scr[...] = jnp.zeros_like(acc_scr)

    t = t_scr[...]
    hgate = jnp.dot(t, wg_ref[...], preferred_element_type=F32)
    hup = jnp.dot(t, wu_ref[...], preferred_element_type=F32)
    lane = lax.broadcasted_iota(jnp.int32, (1, 128), 1)
    cw = jnp.sum(jnp.where(lane == N_GROUPS_E + e, route_ref[...], 0.0), axis=-1, keepdims=True)
    acc_scr[...] += jnp.dot(_bf(_silu(hgate) * hup * cw), wd_ref[...], preferred_element_type=F32)

    @pl.when(e == N_EXPERTS - 1)
    def _():
        y_ref[...] = _rms(xmid_ref[...] + acc_scr[...], nfin_ref[...])


def _moe(xmid, route, p, tm):
    t, d = xmid.shape
    return pl.pallas_call(
        _moe_kernel,
        grid=(t // tm, N_EXPERTS),
        in_specs=[
            pl.BlockSpec((tm, d), lambda i, e: (i, 0)),
            pl.BlockSpec((tm, 128), lambda i, e: (i, 0)),
            _const_spec((1, d)),
            _const_spec((1, d)),
            pl.BlockSpec((None, d, D_EXPERT), lambda i, e: (e, 0, 0)),
            pl.BlockSpec((None, d, D_EXPERT), lambda i, e: (e, 0, 0)),
            pl.BlockSpec((None, D_EXPERT, d), lambda i, e: (e, 0, 0)),
        ],
        out_specs=pl.BlockSpec((tm, d), lambda i, e: (i, 0)),
        out_shape=jax.ShapeDtypeStruct((t, d), F32),
        scratch_shapes=[pltpu.VMEM((tm, d), BF16), pltpu.VMEM((tm, d), F32)],
        compiler_params=pltpu.CompilerParams(dimension_semantics=("parallel", "arbitrary"),
                                             vmem_limit_bytes=VMEM_LIMIT),
        name="moe",
    )(xmid, route, p["norm_ffn"], p["norm_final"], p["w_gate"], p["w_up"], p["w_down"])


def _pick_tile(t, pref):
    tm = min(pref, t)
    while t % tm:
        tm //= 2
    return tm


def _prep_layer(l, norm_mix, w_in, conv_a_w, a_log_a, dt_bias_a, norm_a, conv_b_w, conv_b_b, a_log_b, dt_bias_b,
                d_skip_b, norm_b, w_out, norm_ffn, w_router_group, b_router_group, w_router_expert,
                b_router_expert, w_gate_e, w_up_e, w_down_e, norm_final):
    w = w_in[l]
    cuts = np.cumsum([D_QKV_A, D_V_A, H_A, H_A, D_B, D_XBC]).tolist()
    w_qkv, w_gate, w_beta, w_alpha, w_z, w_xbc, w_dt = jnp.split(w, cuts, axis=1)
    w_small = jnp.concatenate([w_beta, w_alpha, w_dt, jnp.zeros((D_MODEL, 128 - 2 * H_A - H_B), F32)], axis=1)
    w_all = _bf(jnp.concatenate([w_qkv, w_xbc, w_gate, w_z, w_small], axis=1))

    def slab(a8, b16):
        return jnp.concatenate([jnp.zeros((H_A,), F32), a8.astype(F32), b16.astype(F32),
                                jnp.zeros((128 - 2 * H_A - H_B,), F32)]).reshape(1, 128)

    w_router = jnp.concatenate(
        [w_router_group[l].astype(F32), w_router_expert[l].reshape(D_MODEL, N_EXPERTS).astype(F32),
         jnp.zeros((D_MODEL, 128 - N_GROUPS_E - N_EXPERTS), F32)], axis=1)
    b_router = jnp.concatenate(
        [b_router_group[l].astype(F32), b_router_expert[l].reshape(N_EXPERTS).astype(F32),
         jnp.zeros((128 - N_GROUPS_E - N_EXPERTS,), F32)]).reshape(1, 128)
    return {
        "norm_mix": norm_mix[l].astype(F32).reshape(1, D_MODEL),
        "w_all": w_all,
        "conv_w": jnp.concatenate([conv_a_w[l], conv_b_w[l]], axis=1).astype(F32),
        "conv_b": jnp.concatenate([jnp.zeros((D_QKV_A,), F32), conv_b_b[l].astype(F32)]).reshape(1, D_CONV),
        "bias_slab": slab(dt_bias_a[l], dt_bias_b[l]),
        "alog_slab": slab(a_log_a[l], a_log_b[l]),
        "norm_a_x": jnp.tile(norm_a[l].astype(F32), H_A).reshape(1, D_V_A),
        "norm_b": norm_b[l].astype(F32).reshape(1, D_B),
        "dskip_x": jnp.repeat(d_skip_b[l].astype(F32), P_B).reshape(1, D_B),
        "w_out": _bf(w_out[l]),
        "norm_ffn": norm_ffn[l].astype(F32).reshape(1, D_MODEL),
        "w_router": w_router,
        "b_router": b_router,
        "w_gate": _bf(w_gate_e[l].reshape(N_EXPERTS, D_MODEL, D_EXPERT)),
        "w_up": _bf(w_up_e[l].reshape(N_EXPERTS, D_MODEL, D_EXPERT)),
        "w_down": _bf(w_down_e[l].reshape(N_EXPERTS, D_EXPERT, D_MODEL)),
        "norm_final": norm_final.astype(F32).reshape(1, D_MODEL),
    }


def _ffn_tail(mix2d, x2d, p):
    t = x2d.shape[0]
    xmid, route = _outproj_router(mix2d, x2d, p, _pick_tile(t, 512))
    return _moe(xmid, route, p, _pick_tile(t, 1024))


def kernel(x_prompt, x_sample, state_delta, state_delta_conv, state_ssm, state_ssm_conv, norm_mix, w_in, conv_a_w,
           a_log_a, dt_bias_a, norm_a, conv_b_w, conv_b_b, a_log_b, dt_bias_b, d_skip_b, norm_b, w_out, norm_ffn,
           w_router_group, b_router_group, w_router_expert, b_router_expert, w_gate_e, w_up_e, w_down_e,
           norm_final):
    depth = w_in.shape[0]
    assert depth == 1, "the fused final norm assumes a single layer"
    bp, lp, d = x_prompt.shape
    bs, ls, _ = x_sample.shape
    assert lp % CHUNK == 0 and lp >= CONV_W - 1
    assert PAD_FRONT + ls <= SEQ_ROWS and bs % (CHUNK // SEQ_ROWS) == 0
    l = 0
    p = _prep_layer(l, norm_mix, w_in, conv_a_w, a_log_a, dt_bias_a, norm_a, conv_b_w, conv_b_b, a_log_b,
                    dt_bias_b, d_skip_b, norm_b, w_out, norm_ffn, w_router_group, b_router_group,
                    w_router_expert, b_router_expert, w_gate_e, w_up_e, w_down_e, norm_final)

    xp2 = x_prompt.reshape(bp * lp, d)
    proj_p = _inproj(xp2, p["norm_mix"], p["w_all"], _pick_tile(bp * lp, 256)).reshape(bp, lp, D_PROJ)
    mix_p, delta_p, ssm_p = _mixer_prompt(proj_p, p)
    dconv_p = proj_p[:, lp - (CONV_W - 1):, :D_QKV_A]
    sconv_p = proj_p[:, lp - (CONV_W - 1):, D_QKV_A:D_CONV]
    y_p = _ffn_tail(mix_p.reshape(bp * lp, -1), xp2, p).reshape(bp, lp, d)

    xs2 = x_sample.reshape(bs * ls, d)
    proj_s = _inproj(xs2, p["norm_mix"], p["w_all"], _pick_tile(bs * ls, 256)).reshape(bs, ls, D_PROJ)
    conv_rows = jnp.concatenate(
        [state_delta_conv[l].astype(F32), state_ssm_conv[l].astype(F32),
         jnp.zeros((bs, CONV_W - 1, D_PROJ - D_CONV), F32)], axis=-1)
    slab = jnp.concatenate(
        [conv_rows, proj_s, jnp.zeros((bs, SEQ_ROWS - PAD_FRONT - ls, D_PROJ), F32)], axis=1)
    mix_s, delta_s, ssm_s = _mixer_decode(slab.reshape(bs * SEQ_ROWS, D_PROJ), state_delta[l].astype(F32),
                                          state_ssm[l].astype(F32).reshape(bs, D_B, N_B), p, ls)
    mix_s = mix_s.reshape(bs, SEQ_ROWS, -1)[:, PAD_FRONT:PAD_FRONT + ls].reshape(bs * ls, -1)
    dconv_s = slab[:, ls:ls + CONV_W - 1, :D_QKV_A]
    sconv_s = slab[:, ls:ls + CONV_W - 1, D_QKV_A:D_CONV]
    y_s = _ffn_tail(mix_s, xs2, p).reshape(bs, ls, d)

    return (y_p.astype(x_prompt.dtype), y_s.astype(x_sample.dtype),
            delta_p[None], dconv_p[None], ssm_p.reshape(bp, H_B, P_B, N_B)[None], sconv_p[None],
            delta_s[None], dconv_s[None], ssm_s.reshape(bs, H_B, P_B, N_B)[None], sconv_s[None])
```

```python
import functools

import numpy as np
import jax
import jax.numpy as jnp
from jax import lax
from jax.experimental import pallas as pl
from jax.experimental.pallas import tpu as pltpu

F32 = jnp.float32
BF16 = jnp.bfloat16

D_MODEL = 1024
H_A, DK_A, DV_A = 8, 128, 128
D_QK_A = H_A * DK_A
D_V_A = H_A * DV_A
D_QKV_A = 2 * D_QK_A + D_V_A
H_B, P_B, N_B, G_B = 16, 64, 128, 2
D_B = H_B * P_B
D_XBC = D_B + 2 * G_B * N_B
CONV_W = 4
N_GROUPS_E, EXPERTS_PER_GROUP = 4, 4
N_EXPERTS = N_GROUPS_E * EXPERTS_PER_GROUP
D_EXPERT = 512
EPS = 1e-6

D_CONV = D_QKV_A + D_XBC
OFF_GATE = D_CONV
OFF_Z = OFF_GATE + D_V_A
OFF_SMALL = OFF_Z + D_B
D_PROJ = OFF_SMALL + 128
LANE_G = H_A
LANE_DT = 2 * H_A

CHUNK = 64
SEQ_ROWS = 8
PAD_FRONT = CONV_W - 1
MOE_BLOCK = 1024
MOE_SUB = 128
VMEM_LIMIT = 52 * 1024 * 1024


def _bf(x):
    return x.astype(BF16)


def _mm(a, b):
    return jnp.dot(_bf(a), _bf(b), preferred_element_type=F32)


def _mm_nt(a, b):
    return lax.dot_general(_bf(a), _bf(b), (((1,), (1,)), ((), ())), preferred_element_type=F32)


def _mm_tn(a, b):
    return lax.dot_general(_bf(a), _bf(b), (((0,), (0,)), ((), ())), preferred_element_type=F32)


def _split3(x):
    x1 = _bf(x)
    r1 = x - x1.astype(F32)
    x2 = _bf(r1)
    x3 = _bf(r1 - x2.astype(F32))
    return x1, x2, x3


def _mm01_r(x, m01):
    x1, x2, x3 = _split3(x)
    d = functools.partial(jnp.dot, preferred_element_type=F32)
    return d(x1, m01) + d(x2, m01) + d(x3, m01)


def _mm01_l(m01, x):
    x1, x2, x3 = _split3(x)
    d = functools.partial(jnp.dot, preferred_element_type=F32)
    return d(m01, x1) + d(m01, x2) + d(m01, x3)


def _rms(x, w):
    return x * lax.rsqrt(jnp.mean(x * x, axis=-1, keepdims=True) + EPS) * w


def _silu(x):
    return x * jax.nn.sigmoid(x)


def _softplus(x):
    return jnp.maximum(x, 0.0) + jnp.log1p(jnp.exp(-jnp.abs(x)))


def _inproj_kernel(x_ref, nw_ref, w_ref, o_ref):
    h = _rms(x_ref[...], nw_ref[...])
    o_ref[...] = jnp.dot(_bf(h), w_ref[...], preferred_element_type=F32)


def _inproj(x2d, norm_w, w_all, tm):
    t, d = x2d.shape
    n = w_all.shape[1]
    return pl.pallas_call(
        _inproj_kernel,
        grid=(t // tm,),
        in_specs=[
            pl.BlockSpec((tm, d), lambda i: (i, 0)),
            pl.BlockSpec((1, d), lambda i: (0, 0)),
            pl.BlockSpec((d, n), lambda i: (0, 0), pipeline_mode=pl.Buffered(1)),
        ],
        out_specs=pl.BlockSpec((tm, n), lambda i: (i, 0)),
        out_shape=jax.ShapeDtypeStruct((t, n), F32),
        compiler_params=pltpu.CompilerParams(dimension_semantics=("parallel",), vmem_limit_bytes=VMEM_LIMIT),
        name="inproj",
    )(x2d, norm_w, w_all)


def _inv_unit_lower(mats, ri, ci, eye, merge_sizes):
    blk = (ri >> 3) == (ci >> 3)
    ads = [jnp.where(blk, a, 0.0) for a in mats]
    xs = [eye - ad for ad in ads]
    a2s = [_mm(ad, ad) for ad in ads]
    xs = [x + _mm(x, a2) for x, a2 in zip(xs, a2s)]
    a4s = [_mm(a2, a2) for a2 in a2s]
    xs = [x + _mm(x, a4) for x, a4 in zip(xs, a4s)]
    for s in merge_sizes:
        sh = s.bit_length() - 1
        rb = ri >> sh
        sel = ((rb & 1) == 1) & ((ci >> sh) == rb - 1)
        ts = [_mm(x, jnp.where(sel, a, 0.0)) for x, a in zip(xs, mats)]
        xs = [x - _mm(t, x) for x, t in zip(xs, ts)]
    return xs


def _mixer_kernel(n_seq, valid_lo, valid_hi, *refs):
    carry = n_seq == 1
    (proj_ref, cw_ref, cb_ref, bias_ref, alog_ref, na_ref, nb_ref, dsk_ref,
     ea_ref, eb_ref, ec_ref) = refs[:11]
    if carry:
        mix_ref, s_wr, h_wr, xc_ref, prev_ref = refs[11:]
        s_rd, h_rd = s_wr, h_wr

        @pl.when(pl.program_id(1) == 0)
        def _():
            s_wr[...] = jnp.zeros_like(s_wr)
            h_wr[...] = jnp.zeros_like(h_wr)
            prev_ref[...] = jnp.zeros_like(prev_ref)
    else:
        s_rd, h_rd, mix_ref, s_wr, h_wr, xc_ref = refs[11:]
        prev_ref = None

    c = CHUNK
    r_seq = c // n_seq
    row = lax.broadcasted_iota(jnp.int32, (c, 1), 0)
    lane = lax.broadcasted_iota(jnp.int32, (1, 128), 1)
    ri = lax.broadcasted_iota(jnp.int32, (c, c), 0)
    ci = lax.broadcasted_iota(jnp.int32, (c, c), 1)
    eye = (ri == ci).astype(F32)
    if carry:
        causal = ri >= ci
        strict = ri > ci
        valid = None
        merge_sizes = (8, 16, 32)
    else:
        sh = r_seq.bit_length() - 1
        same = (ri >> sh) == (ci >> sh)
        causal = (ri >= ci) & same
        strict = (ri > ci) & same
        rr = row & (r_seq - 1)
        valid = ((rr >= valid_lo) & (rr < valid_hi)).astype(F32)
        merge_sizes = ()

    row8 = lax.broadcasted_iota(jnp.int32, (8, 1), 0)
    for c0 in range(0, D_CONV, 512):
        sl = slice(c0, c0 + 512)
        u = proj_ref[:, sl]
        acc = u * cw_ref[CONV_W - 1:CONV_W, sl] + cb_ref[:, sl]
        for k in range(1, CONV_W):
            ru = pltpu.roll(u, k, 0)
            if carry:
                rp = pltpu.roll(prev_ref[:, sl], k, 0)
                first = jnp.where(row8 < k, rp, ru[0:8])
                ru = jnp.concatenate([first, ru[8:]], axis=0)
            acc = acc + ru * cw_ref[CONV_W - 1 - k:CONV_W - k, sl]
        if carry:
            prev_ref[:, sl] = u[c - 8:c]
        xc_ref[:, sl] = _silu(acc)

    raw = proj_ref[:, OFF_SMALL:OFF_SMALL + 128]
    sp = _softplus(raw + bias_ref[...])
    coef = -jnp.exp(alog_ref[...])
    is_beta = lane < LANE_G
    is_g = (lane >= LANE_G) & (lane < LANE_DT)
    is_dt = (lane >= LANE_DT) & (lane < LANE_DT + H_B)
    gd = jnp.where(is_g | is_dt, sp * coef, 0.0)
    q1 = jnp.where(is_beta, jax.nn.sigmoid(raw), jnp.where(is_dt, sp, 0.0))
    if valid is not None:
        gd = gd * valid
        q1 = q1 * valid
    gc = _mm01_l(_bf(causal.astype(F32)), gd)
    if carry:
        tot = jnp.broadcast_to(gc[c - 1:c, :], (c, 128))
    else:
        tot = _mm01_l(_bf(same.astype(F32)), gd)
    eg = jnp.exp(gc)
    et = jnp.exp(tot - gc)
    etot = jnp.exp(tot)
    xa = _mm01_r(jnp.concatenate([q1, jnp.where(is_g, eg, 0.0), jnp.where(is_g, et, 0.0)], axis=0), ea_ref[...])
    beta_x, egc_x, tail_x = xa[0:c], xa[c:2 * c], xa[2 * c:3 * c]
    xb = _mm01_r(jnp.concatenate([q1, eg, et], axis=0), eb_ref[...])
    dt_x, edac_x, tailb_x = xb[0:c], xb[c:2 * c], xb[2 * c:3 * c]
    dec_x = _mm01_r(etot, ec_ref[...])
    gct = gc.T

    heads = range(H_A)
    hsl = [slice(128 * h, 128 * (h + 1)) for h in heads]
    seq_of_row = row >> (r_seq.bit_length() - 1)
    qs, ks, vs = [], [], []
    for h in heads:
        q = xc_ref[:, 128 * h:128 * (h + 1)]
        k = xc_ref[:, D_QK_A + 128 * h:D_QK_A + 128 * (h + 1)]
        qs.append(q * lax.rsqrt(jnp.sum(q * q, axis=-1, keepdims=True) + EPS) * (DK_A ** -0.5))
        ks.append(k * lax.rsqrt(jnp.sum(k * k, axis=-1, keepdims=True) + EPS))
        vs.append(xc_ref[:, 2 * D_QK_A + 128 * h:2 * D_QK_A + 128 * (h + 1)])
    kbs = [ks[h] * beta_x[:, hsl[h]] for h in heads]
    nts = [_mm_nt(jnp.concatenate([kbs[h], qs[h]], axis=0), ks[h]) for h in heads]
    decs = [jnp.exp(jnp.where(causal, gc[:, LANE_G + h:LANE_G + h + 1] - gct[LANE_G + h:LANE_G + h + 1, :],
                              -jnp.inf)) for h in heads]
    a_s = [jnp.where(strict, nts[h][:c] * decs[h], 0.0) for h in heads]
    qks = [nts[h][c:] * decs[h] for h in heads]
    x_invs = _inv_unit_lower(a_s, ri, ci, eye, merge_sizes)
    sols = [_mm(x_invs[h], jnp.concatenate([vs[h] * beta_x[:, hsl[h]], kbs[h] * egc_x[:, hsl[h]]], axis=1))
            for h in heads]
    qes = [qs[h] * egc_x[:, hsl[h]] for h in heads]
    wss = [[_mm(jnp.concatenate([sols[h][r_seq * s:r_seq * (s + 1), 128:], qes[h][r_seq * s:r_seq * (s + 1)]],
                                axis=0), s_rd[s, h]) for s in range(n_seq)] for h in heads]
    vnews, obs = [], []
    for h in heads:
        vn = [sols[h][r_seq * s:r_seq * (s + 1), :128] - wss[h][s][:r_seq] for s in range(n_seq)]
        ob = [wss[h][s][r_seq:] for s in range(n_seq)]
        vnews.append(vn[0] if n_seq == 1 else jnp.concatenate(vn, axis=0))
        obs.append(ob[0] if n_seq == 1 else jnp.concatenate(ob, axis=0))
    os_ = [obs[h] + _mm(qks[h], vnews[h]) for h in heads]
    for h in heads:
        kt = ks[h] * tail_x[:, hsl[h]]
        for s in range(n_seq):
            kts = kt if n_seq == 1 else jnp.where(seq_of_row == s, kt, 0.0)
            s_wr[s, h] = s_rd[s, h] * dec_x[r_seq * s:r_seq * s + 1, hsl[h]] + _mm_tn(kts, vnews[h])
    for h in heads:
        gate = proj_ref[:, OFF_GATE + 128 * h:OFF_GATE + 128 * (h + 1)]
        mix_ref[:, hsl[h]] = _bf(_rms(os_[h], na_ref[:, hsl[h]]) * _silu(gate))

    hg = H_B // G_B
    wg = hg * P_B
    off_x = D_QKV_A
    off_b = D_QKV_A + D_B
    off_c = off_b + G_B * N_B
    for g in range(G_B):
        gs = slice(wg * g, wg * (g + 1))
        bg = xc_ref[:, off_b + N_B * g:off_b + N_B * (g + 1)]
        cg = xc_ref[:, off_c + N_B * g:off_c + N_B * (g + 1)]
        xs_g = xc_ref[:, off_x + wg * g:off_x + wg * (g + 1)]
        xdt = xs_g * dt_x[:, gs]
        cb = _mm_nt(cg, bg)
        ypairs = []
        for p in range(hg // 2):
            xp = xdt[:, 128 * p:128 * (p + 1)]
            ys = []
            for a_ in (0, 1):
                ln = LANE_DT + hg * g + 2 * p + a_
                seg = jnp.exp(jnp.where(causal, gc[:, ln:ln + 1] - gct[ln:ln + 1, :], -jnp.inf))
                ys.append(_mm(cb * seg, xp))
            ypairs.append(jnp.where(lane < P_B, ys[0], ys[1]))
        y = jnp.concatenate(ypairs, axis=1)
        yoffs = []
        for s in range(n_seq):
            rows = slice(r_seq * s, r_seq * (s + 1))
            yoffs.append(_mm_nt(cg[rows], h_rd[s, wg * g:wg * (g + 1), :]))
        yoff = yoffs[0] if n_seq == 1 else jnp.concatenate(yoffs, axis=0)
        y = y + yoff * edac_x[:, gs] + dsk_ref[:, gs] * xs_g
        xt = xdt * tailb_x[:, gs]
        for s in range(n_seq):
            xts = xt if n_seq == 1 else jnp.where((row >> (r_seq.bit_length() - 1)) == s, xt, 0.0)
            st = _mm_tn(xts, bg)
            for j in range(hg):
                hh = hg * g + j
                rs = slice(P_B * hh, P_B * (hh + 1))
                drow = dec_x[r_seq * s:r_seq * s + 1, 128 * (H_A + hh):128 * (H_A + hh + 1)]
                h_wr[s, rs, :] = h_rd[s, rs, :] * drow + st[P_B * j:P_B * (j + 1), :]
        z = proj_ref[:, OFF_Z + wg * g:OFF_Z + wg * (g + 1)]
        mix_ref[:, D_V_A + wg * g:D_V_A + wg * (g + 1)] = _bf(_rms(y * _silu(z), nb_ref[:, gs]))


def _expand_mats():
    ea = np.zeros((128, 128 * H_A), np.float32)
    for h in range(H_A):
        ea[h, 128 * h:128 * (h + 1)] = 1.0
        ea[LANE_G + h, 128 * h:128 * (h + 1)] = 1.0
    eb = np.zeros((128, D_B), np.float32)
    for h in range(H_B):
        eb[LANE_DT + h, P_B * h:P_B * (h + 1)] = 1.0
    ec = np.zeros((128, 128 * (H_A + H_B)), np.float32)
    for h in range(H_A):
        ec[LANE_G + h, 128 * h:128 * (h + 1)] = 1.0
    for h in range(H_B):
        ec[LANE_DT + h, 128 * (H_A + h):128 * (H_A + h + 1)] = 1.0
    return jnp.asarray(ea, BF16), jnp.asarray(eb, BF16), jnp.asarray(ec, BF16)


def _const_spec(shape):
    return pl.BlockSpec(shape, lambda *_: (0,) * len(shape))


def _mixer_params(p):
    ea, eb, ec = _expand_mats()
    return (p["conv_w"], p["conv_b"], p["bias_slab"], p["alog_slab"], p["norm_a_x"], p["norm_b"], p["dskip_x"],
            ea, eb, ec)


def _mixer_prompt(proj, p):
    b, l, n = proj.shape
    consts = _mixer_params(p)
    const_specs = [_const_spec(a.shape) for a in consts]
    return pl.pallas_call(
        functools.partial(_mixer_kernel, 1, 0, CHUNK),
        grid=(b, l // CHUNK),
        in_specs=[pl.BlockSpec((None, CHUNK, n), lambda i, t: (i, t, 0))] + const_specs,
        out_specs=[
            pl.BlockSpec((None, CHUNK, D_V_A + D_B), lambda i, t: (i, t, 0)),
            pl.BlockSpec((1, H_A, DK_A, DV_A), lambda i, t: (i, 0, 0, 0)),
            pl.BlockSpec((1, D_B, N_B), lambda i, t: (i, 0, 0)),
        ],
        out_shape=[
            jax.ShapeDtypeStruct((b, l, D_V_A + D_B), BF16),
            jax.ShapeDtypeStruct((b, H_A, DK_A, DV_A), F32),
            jax.ShapeDtypeStruct((b, D_B, N_B), F32),
        ],
        scratch_shapes=[pltpu.VMEM((CHUNK, D_CONV), F32), pltpu.VMEM((8, D_CONV), F32)],
        compiler_params=pltpu.CompilerParams(dimension_semantics=("parallel", "arbitrary"),
                                             vmem_limit_bytes=VMEM_LIMIT),
        name="mixer_prompt",
    )(proj, *consts)


def _mixer_decode(slab, s_delta, s_ssm, p, l_dec):
    rows, n = slab.shape
    n_seq = CHUNK // SEQ_ROWS
    consts = _mixer_params(p)
    const_specs = [_const_spec(a.shape) for a in consts]
    bs = s_delta.shape[0]
    return pl.pallas_call(
        functools.partial(_mixer_kernel, n_seq, PAD_FRONT, PAD_FRONT + l_dec),
        grid=(rows // CHUNK,),
        in_specs=[pl.BlockSpec((CHUNK, n), lambda i: (i, 0))] + const_specs + [
            pl.BlockSpec((n_seq, H_A, DK_A, DV_A), lambda i: (i, 0, 0, 0)),
            pl.BlockSpec((n_seq, D_B, N_B), lambda i: (i, 0, 0)),
        ],
        out_specs=[
            pl.BlockSpec((CHUNK, D_V_A + D_B), lambda i: (i, 0)),
            pl.BlockSpec((n_seq, H_A, DK_A, DV_A), lambda i: (i, 0, 0, 0)),
            pl.BlockSpec((n_seq, D_B, N_B), lambda i: (i, 0, 0)),
        ],
        out_shape=[
            jax.ShapeDtypeStruct((rows, D_V_A + D_B), BF16),
            jax.ShapeDtypeStruct((bs, H_A, DK_A, DV_A), F32),
            jax.ShapeDtypeStruct((bs, D_B, N_B), F32),
        ],
        scratch_shapes=[pltpu.VMEM((CHUNK, D_CONV), F32)],
        compiler_params=pltpu.CompilerParams(dimension_semantics=("parallel",), vmem_limit_bytes=VMEM_LIMIT),
        name="mixer_decode",
    )(slab, *consts, s_delta, s_ssm)


def _outproj_router_kernel(sub, mix_ref, x_ref, wo_ref, nf_ref, wr_ref, br_ref, xmid_ref, route_ref, meta_ref):
    xm = x_ref[...] + jnp.dot(mix_ref[...], wo_ref[...], preferred_element_type=F32)
    xmid_ref[...] = xm
    t = _rms(xm, nf_ref[...])
    t1 = _bf(t)
    t2 = _bf(t - t1.astype(F32))
    w = wr_ref[...]
    w1 = _bf(w)
    w2 = _bf(w - w1.astype(F32))
    d = functools.partial(jnp.dot, preferred_element_type=F32)
    logit = d(t1, w1) + d(t1, w2) + d(t2, w1) + br_ref[...]

    lane = lax.broadcasted_iota(jnp.int32, (1, 128), 1).astype(F32)
    neg = -jnp.inf
    big = 1e9
    is_grp = lane < N_GROUPS_E
    gl = jnp.where(is_grp, logit, neg)
    gmax = jnp.max(gl, axis=-1, keepdims=True)
    gsel = jnp.min(jnp.where(gl == gmax, lane, big), axis=-1, keepdims=True)
    gw = 1.0 / jnp.sum(jnp.exp(jnp.where(is_grp, logit - gmax, neg)), axis=-1, keepdims=True)
    lo = N_GROUPS_E + EXPERTS_PER_GROUP * gsel
    el = jnp.where((lane >= lo) & (lane < lo + EXPERTS_PER_GROUP), logit, neg)
    v1 = jnp.max(el, axis=-1, keepdims=True)
    i1 = jnp.min(jnp.where(el == v1, lane, big), axis=-1, keepdims=True)
    el2 = jnp.where(lane == i1, neg, el)
    v2 = jnp.max(el2, axis=-1, keepdims=True)
    i2 = jnp.min(jnp.where(el2 == v2, lane, big), axis=-1, keepdims=True)
    e = jnp.exp(v2 - v1)
    den = gw / (1.0 + e)
    comb = jnp.where(lane == i1, den, 0.0) + jnp.where(lane == i2, e * den, 0.0)

    tm = logit.shape[0]
    onehot = jnp.where((lane == gsel) & is_grp, 1.0, 0.0)
    ri = lax.broadcasted_iota(jnp.int32, (tm, tm), 0)
    ci = lax.broadcasted_iota(jnp.int32, (tm, tm), 1)
    ranks = jnp.dot(_bf((ri > ci).astype(F32)), _bf(onehot), preferred_element_type=F32)
    cnt = ranks[tm - 1:tm, :] + onehot[tm - 1:tm, :]
    ntile = jnp.floor((cnt + (sub - 1)) * (1.0 / sub))
    li = lax.broadcasted_iota(jnp.int32, (128, 128), 0)
    lj = lax.broadcasted_iota(jnp.int32, (128, 128), 1)
    off = _mm01_r(jnp.broadcast_to(ntile * sub, (8, 128)), _bf((li < lj).astype(F32)))[0:1]
    pos = jnp.sum(onehot * (off + ranks), axis=-1, keepdims=True)
    route_ref[...] = jnp.where(lane == 0.0, gsel, jnp.where(lane == 1.0, pos, comb))
    row8 = lax.broadcasted_iota(jnp.int32, (8, 128), 0)
    meta_ref[...] = jnp.where(row8 == 0, off, ntile)


def _outproj_router(mix, x2d, p, tm, sub):
    t, d = x2d.shape
    nb = t // tm
    return pl.pallas_call(
        functools.partial(_outproj_router_kernel, sub),
        grid=(nb,),
        in_specs=[
            pl.BlockSpec((tm, mix.shape[1]), lambda i: (i, 0)),
            pl.BlockSpec((tm, d), lambda i: (i, 0)),
            _const_spec(p["w_out"].shape),
            _const_spec((1, d)),
            _const_spec(p["w_router"].shape),
            _const_spec((1, 128)),
        ],
        out_specs=[pl.BlockSpec((tm, d), lambda i: (i, 0)), pl.BlockSpec((tm, 128), lambda i: (i, 0)),
                   pl.BlockSpec((None, 8, 128), lambda i: (i, 0, 0))],
        out_shape=[jax.ShapeDtypeStruct((t, d), F32), jax.ShapeDtypeStruct((t, 128), F32),
                   jax.ShapeDtypeStruct((nb, 8, 128), F32)],
        compiler_params=pltpu.CompilerParams(dimension_semantics=("parallel",), vmem_limit_bytes=VMEM_LIMIT),
        name="outproj_router",
    )(mix, x2d, p["w_out"], p["norm_ffn"], p["w_router"], p["b_router"])


def _unpack_halves(xp):
    lo = lax.bitcast_convert_type(lax.shift_left(xp, jnp.uint32(16)), F32)
    hi = lax.bitcast_convert_type(xp & jnp.uint32(0xFFFF0000), F32)
    return _bf(lo), _bf(hi)


def _moe_kernel(tb, sub, meta_ref, pos_ref, xmid_ref, route_ref, nffn_ref, nfin_ref, wg_ref, wu_ref, wd_ref,
                y_ref, tp_scr, xg_scr, rg_scr, yg_scr):
    b = pl.program_id(0)
    e16 = pl.program_id(1)
    g = e16 // EXPERTS_PER_GROUP
    half = D_MODEL // 2
    unroll = 8

    @pl.when(e16 == 0)
    def _():
        t = _rms(xmid_ref[...], nffn_ref[...])
        lo = lax.shift_right_logical(lax.bitcast_convert_type(_bf(t[:, :half]).astype(F32), jnp.uint32),
                                     jnp.uint32(16))
        hi = lax.bitcast_convert_type(_bf(t[:, half:]).astype(F32), jnp.uint32)
        tp_scr[...] = hi | lo
        xg_scr[...] = jnp.zeros_like(xg_scr)
        rg_scr[...] = jnp.zeros_like(rg_scr)

        def dispatch(i, carry):
            base = pl.multiple_of(i * unroll, unroll)
            for u in range(unroll):
                p = pos_ref[0, base + u]
                xg_scr[pl.ds(p, 1), :] = tp_scr[pl.ds(base + u, 1), :]
                rg_scr[pl.ds(p, 1), :] = route_ref[pl.ds(base + u, 1), :]
            return carry

        lax.fori_loop(0, tb // unroll, dispatch, 0)

    off = meta_ref[b, g]
    ntile = meta_ref[b, N_GROUPS_E + g]
    lane = lax.broadcasted_iota(jnp.int32, (1, 128), 1)
    first = (e16 % EXPERTS_PER_GROUP) == 0
    d = functools.partial(jnp.dot, preferred_element_type=F32)

    def tile(j, carry):
        r0 = pl.multiple_of(off + j * sub, sub)
        xlo, xhi = _unpack_halves(xg_scr[pl.ds(r0, sub), :])
        hgate = d(xlo, wg_ref[:half, :]) + d(xhi, wg_ref[half:, :])
        hup = d(xlo, wu_ref[:half, :]) + d(xhi, wu_ref[half:, :])
        cw = jnp.sum(jnp.where(lane == N_GROUPS_E + e16, rg_scr[pl.ds(r0, sub), :], 0.0), axis=-1, keepdims=True)
        contrib = d(_bf(_silu(hgate) * hup * cw), wd_ref[...])

        @pl.when(first)
        def _():
            yg_scr[pl.ds(r0, sub), :] = contrib

        @pl.when(jnp.logical_not(first))
        def _():
            yg_scr[pl.ds(r0, sub), :] += contrib

        return carry

    lax.fori_loop(0, ntile, tile, 0)

    @pl.when(e16 == N_EXPERTS - 1)
    def _():
        def combine(i, carry):
            base = pl.multiple_of(i * unroll, unroll)
            for u in range(unroll):
                p = pos_ref[0, base + u]
                y_ref[pl.ds(base + u, 1), :] = yg_scr[pl.ds(p, 1), :]
            return carry

        lax.fori_loop(0, tb // unroll, combine, 0)
        y_ref[...] = _rms(xmid_ref[...] + y_ref[...], nfin_ref[...])


def _moe(xmid, route, meta, p, tb, sub):
    t, d = xmid.shape
    nb = t // tb
    rows = tb + N_GROUPS_E * sub
    pos = route[:, 1].astype(jnp.int32).reshape(nb, 1, tb)
    meta_i = jnp.concatenate([meta[:, 0, :N_GROUPS_E], meta[:, 1, :N_GROUPS_E]], axis=1).astype(jnp.int32)
    grid_spec = pltpu.PrefetchScalarGridSpec(
        num_scalar_prefetch=1,
        grid=(nb, N_EXPERTS),
        in_specs=[
            pl.BlockSpec((None, 1, tb), lambda i, e, m: (i, 0, 0), memory_space=pltpu.SMEM),
            pl.BlockSpec((tb, d), lambda i, e, m: (i, 0)),
            pl.BlockSpec((tb, 128), lambda i, e, m: (i, 0)),
            pl.BlockSpec((1, d), lambda i, e, m: (0, 0)),
            pl.BlockSpec((1, d), lambda i, e, m: (0, 0)),
            pl.BlockSpec((None, d, D_EXPERT), lambda i, e, m: (e, 0, 0)),
            pl.BlockSpec((None, d, D_EXPERT), lambda i, e, m: (e, 0, 0)),
            pl.BlockSpec((None, D_EXPERT, d), lambda i, e, m: (e, 0, 0)),
        ],
        out_specs=pl.BlockSpec((tb, d), lambda i, e, m: (i, 0)),
        scratch_shapes=[pltpu.VMEM((tb, d // 2), jnp.uint32), pltpu.VMEM((rows, d // 2), jnp.uint32),
                        pltpu.VMEM((rows, 128), F32), pltpu.VMEM((rows, d), F32)],
    )
    return pl.pallas_call(
        functools.partial(_moe_kernel, tb, sub),
        grid_spec=grid_spec,
        out_shape=jax.ShapeDtypeStruct((t, d), F32),
        compiler_params=pltpu.CompilerParams(dimension_semantics=("parallel", "arbitrary"),
                                             vmem_limit_bytes=VMEM_LIMIT),
        name="moe",
    )(meta_i, pos, xmid, route, p["norm_ffn"], p["norm_final"], p["w_gate"], p["w_up"], p["w_down"])


def _pick_tile(t, pref):
    tm = min(pref, t)
    while t % tm:
        tm //= 2
    return tm


def _prep_layer(l, norm_mix, w_in, conv_a_w, a_log_a, dt_bias_a, norm_a, conv_b_w, conv_b_b, a_log_b, dt_bias_b,
                d_skip_b, norm_b, w_out, norm_ffn, w_router_group, b_router_group, w_router_expert,
                b_router_expert, w_gate_e, w_up_e, w_down_e, norm_final):
    w = w_in[l]
    cuts = np.cumsum([D_QKV_A, D_V_A, H_A, H_A, D_B, D_XBC]).tolist()
    w_qkv, w_gate, w_beta, w_alpha, w_z, w_xbc, w_dt = jnp.split(w, cuts, axis=1)
    w_small = jnp.concatenate([w_beta, w_alpha, w_dt, jnp.zeros((D_MODEL, 128 - 2 * H_A - H_B), F32)], axis=1)
    w_all = _bf(jnp.concatenate([w_qkv, w_xbc, w_gate, w_z, w_small], axis=1))

    def slab(a8, b16):
        return jnp.concatenate([jnp.zeros((H_A,), F32), a8.astype(F32), b16.astype(F32),
                                jnp.zeros((128 - 2 * H_A - H_B,), F32)]).reshape(1, 128)

    w_router = jnp.concatenate(
        [w_router_group[l].astype(F32), w_router_expert[l].reshape(D_MODEL, N_EXPERTS).astype(F32),
         jnp.zeros((D_MODEL, 128 - N_GROUPS_E - N_EXPERTS), F32)], axis=1)
    b_router = jnp.concatenate(
        [b_router_group[l].astype(F32), b_router_expert[l].reshape(N_EXPERTS).astype(F32),
         jnp.zeros((128 - N_GROUPS_E - N_EXPERTS,), F32)]).reshape(1, 128)
    return {
        "norm_mix": norm_mix[l].astype(F32).reshape(1, D_MODEL),
        "w_all": w_all,
        "conv_w": jnp.concatenate([conv_a_w[l], conv_b_w[l]], axis=1).astype(F32),
        "conv_b": jnp.concatenate([jnp.zeros((D_QKV_A,), F32), conv_b_b[l].astype(F32)]).reshape(1, D_CONV),
        "bias_slab": slab(dt_bias_a[l], dt_bias_b[l]),
        "alog_slab": slab(a_log_a[l], a_log_b[l]),
        "norm_a_x": jnp.tile(norm_a[l].astype(F32), H_A).reshape(1, D_V_A),
        "norm_b": norm_b[l].astype(F32).reshape(1, D_B),
        "dskip_x": jnp.repeat(d_skip_b[l].astype(F32), P_B).reshape(1, D_B),
        "w_out": _bf(w_out[l]),
        "norm_ffn": norm_ffn[l].astype(F32).reshape(1, D_MODEL),
        "w_router": w_router,
        "b_router": b_router,
        "w_gate": _bf(w_gate_e[l].reshape(N_EXPERTS, D_MODEL, D_EXPERT)),
        "w_up": _bf(w_up_e[l].reshape(N_EXPERTS, D_MODEL, D_EXPERT)),
        "w_down": _bf(w_down_e[l].reshape(N_EXPERTS, D_EXPERT, D_MODEL)),
        "norm_final": norm_final.astype(F32).reshape(1, D_MODEL),
    }


def _ffn_tail(mix2d, x2d, p):
    t = x2d.shape[0]
    tb = _pick_tile(t, MOE_BLOCK)
    xmid, route, meta = _outproj_router(mix2d, x2d, p, tb, MOE_SUB)
    return _moe(xmid, route, meta, p, tb, MOE_SUB)


def kernel(x_prompt, x_sample, state_delta, state_delta_conv, state_ssm, state_ssm_conv, norm_mix, w_in, conv_a_w,
           a_log_a, dt_bias_a, norm_a, conv_b_w, conv_b_b, a_log_b, dt_bias_b, d_skip_b, norm_b, w_out, norm_ffn,
           w_router_group, b_router_group, w_router_expert, b_router_expert, w_gate_e, w_up_e, w_down_e,
           norm_final):
    depth = w_in.shape[0]
    assert depth == 1, "the fused final norm assumes a single layer"
    bp, lp, d = x_prompt.shape
    bs, ls, _ = x_sample.shape
    assert lp % CHUNK == 0 and lp >= CONV_W - 1
    assert PAD_FRONT + ls <= SEQ_ROWS and bs % (CHUNK // SEQ_ROWS) == 0
    l = 0
    p = _prep_layer(l, norm_mix, w_in, conv_a_w, a_log_a, dt_bias_a, norm_a, conv_b_w, conv_b_b, a_log_b,
                    dt_bias_b, d_skip_b, norm_b, w_out, norm_ffn, w_router_group, b_router_group,
                    w_router_expert, b_router_expert, w_gate_e, w_up_e, w_down_e, norm_final)

    xp2 = x_prompt.reshape(bp * lp, d)
    proj_p = _inproj(xp2, p["norm_mix"], p["w_all"], _pick_tile(bp * lp, 256)).reshape(bp, lp, D_PROJ)
    mix_p, delta_p, ssm_p = _mixer_prompt(proj_p, p)
    dconv_p = proj_p[:, lp - (CONV_W - 1):, :D_QKV_A]
    sconv_p = proj_p[:, lp - (CONV_W - 1):, D_QKV_A:D_CONV]
    y_p = _ffn_tail(mix_p.reshape(bp * lp, -1), xp2, p).reshape(bp, lp, d)

    xs2 = x_sample.reshape(bs * ls, d)
    proj_s = _inproj(xs2, p["norm_mix"], p["w_all"], _pick_tile(bs * ls, 256)).reshape(bs, ls, D_PROJ)
    conv_rows = jnp.concatenate(
        [state_delta_conv[l].astype(F32), state_ssm_conv[l].astype(F32),
         jnp.zeros((bs, CONV_W - 1, D_PROJ - D_CONV), F32)], axis=-1)
    slab = jnp.concatenate(
        [conv_rows, proj_s, jnp.zeros((bs, SEQ_ROWS - PAD_FRONT - ls, D_PROJ), F32)], axis=1)
    mix_s, delta_s, ssm_s = _mixer_decode(slab.reshape(bs * SEQ_ROWS, D_PROJ), state_delta[l].astype(F32),
                                          state_ssm[l].astype(F32).reshape(bs, D_B, N_B), p, ls)
    mix_s = mix_s.reshape(bs, SEQ_ROWS, -1)[:, PAD_FRONT:PAD_FRONT + ls].reshape(bs * ls, -1)
    dconv_s = slab[:, ls:ls + CONV_W - 1, :D_QKV_A]
    sconv_s = slab[:, ls:ls + CONV_W - 1, D_QKV_A:D_CONV]
    y_s = _ffn_tail(mix_s, xs2, p).reshape(bs, ls, d)

    return (y_p.astype(x_prompt.dtype), y_s.astype(x_sample.dtype),
            delta_p[None], dconv_p[None], ssm_p.reshape(bp, H_B, P_B, N_B)[None], sconv_p[None],
            delta_s[None], dconv_s[None], ssm_s.reshape(bs, H_B, P_B, N_B)[None], sconv_s[None])
```

```python
import functools

import numpy as np
import jax
import jax.numpy as jnp
from jax import lax
from jax.experimental import pallas as pl
from jax.experimental.pallas import tpu as pltpu

F32 = jnp.float32
BF16 = jnp.bfloat16

D_MODEL = 1024
H_A, DK_A, DV_A = 8, 128, 128
D_QK_A = H_A * DK_A
D_V_A = H_A * DV_A
D_QKV_A = 2 * D_QK_A + D_V_A
H_B, P_B, N_B, G_B = 16, 64, 128, 2
D_B = H_B * P_B
D_XBC = D_B + 2 * G_B * N_B
CONV_W = 4
N_GROUPS_E, EXPERTS_PER_GROUP = 4, 4
N_EXPERTS = N_GROUPS_E * EXPERTS_PER_GROUP
D_EXPERT = 512
EPS = 1e-6

D_CONV = D_QKV_A + D_XBC
OFF_GATE = D_CONV
OFF_Z = OFF_GATE + D_V_A
OFF_SMALL = OFF_Z + D_B
D_PROJ = OFF_SMALL + 128
LANE_G = H_A
LANE_DT = 2 * H_A

CHUNK = 64
SEQ_ROWS = 8
PROMPT_STREAMS = 2
PAD_FRONT = CONV_W - 1
MOE_BLOCK = 1024
MOE_SUB = 256
MOE_ALIGN = 64
VMEM_LIMIT = 52 * 1024 * 1024


def _bf(x):
    return x.astype(BF16)


def _mm(a, b):
    return jnp.dot(_bf(a), _bf(b), preferred_element_type=F32)


def _mm_nt(a, b):
    return lax.dot_general(_bf(a), _bf(b), (((1,), (1,)), ((), ())), preferred_element_type=F32)


def _mm_tn(a, b):
    return lax.dot_general(_bf(a), _bf(b), (((0,), (0,)), ((), ())), preferred_element_type=F32)


def _split3(x):
    x1 = _bf(x)
    r1 = x - x1.astype(F32)
    x2 = _bf(r1)
    x3 = _bf(r1 - x2.astype(F32))
    return x1, x2, x3


def _mm01_r(x, m01):
    x1, x2, x3 = _split3(x)
    d = functools.partial(jnp.dot, preferred_element_type=F32)
    return d(x1, m01) + d(x2, m01) + d(x3, m01)


def _mm01_l(m01, x):
    x1, x2, x3 = _split3(x)
    d = functools.partial(jnp.dot, preferred_element_type=F32)
    return d(m01, x1) + d(m01, x2) + d(m01, x3)


def _rms(x, w):
    return x * lax.rsqrt(jnp.mean(x * x, axis=-1, keepdims=True) + EPS) * w


def _silu(x):
    return x * jax.nn.sigmoid(x)


def _softplus(x):
    return jnp.maximum(x, 0.0) + jnp.log1p(jnp.exp(-jnp.abs(x)))


def _inproj_kernel(x_ref, nw_ref, w_ref, o_ref):
    h = _rms(x_ref[...], nw_ref[...])
    o_ref[...] = jnp.dot(_bf(h), w_ref[...], preferred_element_type=F32)


def _inproj(x2d, norm_w, w_all, tm):
    t, d = x2d.shape
    n = w_all.shape[1]
    return pl.pallas_call(
        _inproj_kernel,
        grid=(t // tm,),
        in_specs=[
            pl.BlockSpec((tm, d), lambda i: (i, 0)),
            pl.BlockSpec((1, d), lambda i: (0, 0)),
            pl.BlockSpec((d, n), lambda i: (0, 0), pipeline_mode=pl.Buffered(1)),
        ],
        out_specs=pl.BlockSpec((tm, n), lambda i: (i, 0)),
        out_shape=jax.ShapeDtypeStruct((t, n), F32),
        compiler_params=pltpu.CompilerParams(dimension_semantics=("parallel",), vmem_limit_bytes=VMEM_LIMIT),
        name="inproj",
    )(x2d, norm_w, w_all)


def _inv_unit_lower(mats, ri, ci, eye, merge_sizes):
    blk = (ri >> 3) == (ci >> 3)
    ads = [jnp.where(blk, a, 0.0) for a in mats]
    xs = [eye - ad for ad in ads]
    a2s = [_mm(ad, ad) for ad in ads]
    xs = [x + _mm(x, a2) for x, a2 in zip(xs, a2s)]
    a4s = [_mm(a2, a2) for a2 in a2s]
    xs = [x + _mm(x, a4) for x, a4 in zip(xs, a4s)]
    for s in merge_sizes:
        sh = s.bit_length() - 1
        rb = ri >> sh
        sel = ((rb & 1) == 1) & ((ci >> sh) == rb - 1)
        ts = [_mm(x, jnp.where(sel, a, 0.0)) for x, a in zip(xs, mats)]
        xs = [x - _mm(t, x) for x, t in zip(xs, ts)]
    return xs


def _mixer_kernel(n_streams, n_seq, valid_lo, valid_hi, *refs):
    carry = n_seq == 1
    (proj_ref, cw_ref, cb_ref, bias_ref, alog_ref, na_ref, nb_ref, dsk_ref,
     ea_ref, eb_ref, ec_ref) = refs[:11]
    if carry:
        mix_ref, s_wr, h_wr, xc_ref, prev_ref = refs[11:]
        s_rd, h_rd = s_wr, h_wr

        @pl.when(pl.program_id(1) == 0)
        def _():
            s_wr[...] = jnp.zeros_like(s_wr)
            h_wr[...] = jnp.zeros_like(h_wr)
            prev_ref[...] = jnp.zeros_like(prev_ref)
    else:
        s_rd, h_rd, mix_ref, s_wr, h_wr, xc_ref = refs[11:]
        prev_ref = None

    c = CHUNK
    r_seq = c // n_seq
    sh_seq = r_seq.bit_length() - 1
    row = lax.broadcasted_iota(jnp.int32, (c, 1), 0)
    lane = lax.broadcasted_iota(jnp.int32, (1, 128), 1)
    ri = lax.broadcasted_iota(jnp.int32, (c, c), 0)
    ci = lax.broadcasted_iota(jnp.int32, (c, c), 1)
    eye = (ri == ci).astype(F32)
    seq_of_row = row >> sh_seq
    if carry:
        same = None
        causal = ri >= ci
        strict = ri > ci
        valid = None
        merge_sizes = (8, 16, 32)
    else:
        same = (ri >> sh_seq) == (ci >> sh_seq)
        causal = (ri >= ci) & same
        strict = (ri > ci) & same
        rr = row & (r_seq - 1)
        valid = ((rr >= valid_lo) & (rr < valid_hi)).astype(F32)
        merge_sizes = ()
    causal_bf = _bf(causal.astype(F32))
    is_beta = lane < LANE_G
    is_g = (lane >= LANE_G) & (lane < LANE_DT)
    is_dt = (lane >= LANE_DT) & (lane < LANE_DT + H_B)
    row8 = lax.broadcasted_iota(jnp.int32, (8, 1), 0)
    streams = range(n_streams)

    def conv(st):
        for c0 in range(0, D_CONV, 512):
            sl = slice(c0, c0 + 512)
            u = proj_ref[st, :, sl]
            acc = u * cw_ref[CONV_W - 1:CONV_W, sl] + cb_ref[:, sl]
            for k in range(1, CONV_W):
                ru = pltpu.roll(u, k, 0)
                if carry:
                    rp = pltpu.roll(prev_ref[st, :, sl], k, 0)
                    head = jnp.where(row8 < k, rp, ru[0:8])
                    ru = jnp.concatenate([head, ru[8:]], axis=0)
                acc = acc + ru * cw_ref[CONV_W - 1 - k:CONV_W - k, sl]
            if carry:
                prev_ref[st, :, sl] = u[c - 8:c]
            xc_ref[st, :, sl] = _silu(acc)

    def gates(st):
        raw = proj_ref[st, :, OFF_SMALL:OFF_SMALL + 128]
        sp = _softplus(raw + bias_ref[...])
        coef = -jnp.exp(alog_ref[...])
        gd = jnp.where(is_g | is_dt, sp * coef, 0.0)
        q1 = jnp.where(is_beta, jax.nn.sigmoid(raw), jnp.where(is_dt, sp, 0.0))
        if valid is not None:
            gd = gd * valid
            q1 = q1 * valid
        gc = _mm01_l(causal_bf, gd)
        if carry:
            tot = jnp.broadcast_to(gc[c - 1:c, :], (c, 128))
        else:
            tot = _mm01_l(_bf(same.astype(F32)), gd)
        eg = jnp.exp(gc)
        et = jnp.exp(tot - gc)
        xa = _mm01_r(jnp.concatenate([q1, jnp.where(is_g, eg, 0.0), jnp.where(is_g, et, 0.0)], axis=0),
                     ea_ref[...])
        xb = _mm01_r(jnp.concatenate([q1, eg, et], axis=0), eb_ref[...])
        return {
            "gc": gc, "gct": gc.T,
            "beta_x": xa[0:c], "egc_x": xa[c:2 * c], "tail_x": xa[2 * c:3 * c],
            "dt_x": xb[0:c], "edac_x": xb[c:2 * c], "tailb_x": xb[2 * c:3 * c],
            "dec_x": _mm01_r(jnp.exp(tot), ec_ref[...]),
        }

    for st in streams:
        conv(st)
    gts = [gates(st) for st in streams]

    pairs = [(st, h) for st in streams for h in range(H_A)]
    hsl = [slice(128 * h, 128 * (h + 1)) for h in range(H_A)]
    qs, ks, vs = [], [], []
    for st, h in pairs:
        q = xc_ref[st, :, 128 * h:128 * (h + 1)]
        k = xc_ref[st, :, D_QK_A + 128 * h:D_QK_A + 128 * (h + 1)]
        qs.append(q * lax.rsqrt(jnp.sum(q * q, axis=-1, keepdims=True) + EPS) * (DK_A ** -0.5))
        ks.append(k * lax.rsqrt(jnp.sum(k * k, axis=-1, keepdims=True) + EPS))
        vs.append(xc_ref[st, :, 2 * D_QK_A + 128 * h:2 * D_QK_A + 128 * (h + 1)])
    n_p = len(pairs)
    betas = [gts[st]["beta_x"][:, hsl[h]] for st, h in pairs]
    egcs = [gts[st]["egc_x"][:, hsl[h]] for st, h in pairs]
    kbs = [ks[i] * betas[i] for i in range(n_p)]
    nts = [_mm_nt(jnp.concatenate([kbs[i], qs[i]], axis=0), ks[i]) for i in range(n_p)]
    decs = [jnp.exp(jnp.where(causal, gts[st]["gc"][:, LANE_G + h:LANE_G + h + 1]
                              - gts[st]["gct"][LANE_G + h:LANE_G + h + 1, :], -jnp.inf)) for st, h in pairs]
    a_s = [jnp.where(strict, nts[i][:c] * decs[i], 0.0) for i in range(n_p)]
    qks = [nts[i][c:] * decs[i] for i in range(n_p)]
    x_invs = _inv_unit_lower(a_s, ri, ci, eye, merge_sizes)
    sols = [_mm(x_invs[i], jnp.concatenate([vs[i] * betas[i], kbs[i] * egcs[i]], axis=1)) for i in range(n_p)]
    qes = [qs[i] * egcs[i] for i in range(n_p)]
    wss = [[_mm(jnp.concatenate([sols[i][r_seq * s:r_seq * (s + 1), 128:], qes[i][r_seq * s:r_seq * (s + 1)]],
                                axis=0), s_rd[st * n_seq + s, h]) for s in range(n_seq)]
           for i, (st, h) in enumerate(pairs)]
    vnews, obs = [], []
    for i in range(n_p):
        vn = [sols[i][r_seq * s:r_seq * (s + 1), :128] - wss[i][s][:r_seq] for s in range(n_seq)]
        ob = [wss[i][s][r_seq:] for s in range(n_seq)]
        vnews.append(vn[0] if n_seq == 1 else jnp.concatenate(vn, axis=0))
        obs.append(ob[0] if n_seq == 1 else jnp.concatenate(ob, axis=0))
    os_ = [obs[i] + _mm(qks[i], vnews[i]) for i in range(n_p)]
    for i, (st, h) in enumerate(pairs):
        kt = ks[i] * gts[st]["tail_x"][:, hsl[h]]
        for s in range(n_seq):
            kts = kt if n_seq == 1 else jnp.where(seq_of_row == s, kt, 0.0)
            sq = st * n_seq + s
            s_wr[sq, h] = s_rd[sq, h] * gts[st]["dec_x"][r_seq * s:r_seq * s + 1, hsl[h]] + _mm_tn(kts, vnews[i])
    for i, (st, h) in enumerate(pairs):
        gate = proj_ref[st, :, OFF_GATE + 128 * h:OFF_GATE + 128 * (h + 1)]
        mix_ref[st, :, hsl[h]] = _bf(_rms(os_[i], na_ref[:, hsl[h]]) * _silu(gate))

    hg = H_B // G_B
    wg = hg * P_B
    off_x = D_QKV_A
    off_b = D_QKV_A + D_B
    off_c = off_b + G_B * N_B
    for st in streams:
        gt = gts[st]
        gc, gct = gt["gc"], gt["gct"]
        for g in range(G_B):
            gs = slice(wg * g, wg * (g + 1))
            bg = xc_ref[st, :, off_b + N_B * g:off_b + N_B * (g + 1)]
            cg = xc_ref[st, :, off_c + N_B * g:off_c + N_B * (g + 1)]
            xs_g = xc_ref[st, :, off_x + wg * g:off_x + wg * (g + 1)]
            xdt = xs_g * gt["dt_x"][:, gs]
            cb = _mm_nt(cg, bg)
            ypairs = []
            for p in range(hg // 2):
                xp = xdt[:, 128 * p:128 * (p + 1)]
                ys = []
                for a_ in (0, 1):
                    ln = LANE_DT + hg * g + 2 * p + a_
                    seg = jnp.exp(jnp.where(causal, gc[:, ln:ln + 1] - gct[ln:ln + 1, :], -jnp.inf))
                    ys.append(_mm(cb * seg, xp))
                ypairs.append(jnp.where(lane < P_B, ys[0], ys[1]))
            y = jnp.concatenate(ypairs, axis=1)
            yoffs = []
            for s in range(n_seq):
                rows = slice(r_seq * s, r_seq * (s + 1))
                yoffs.append(_mm_nt(cg[rows], h_rd[st * n_seq + s, wg * g:wg * (g + 1), :]))
            yoff = yoffs[0] if n_seq == 1 else jnp.concatenate(yoffs, axis=0)
            y = y + yoff * gt["edac_x"][:, gs] + dsk_ref[:, gs] * xs_g
            xt = xdt * gt["tailb_x"][:, gs]
            for s in range(n_seq):
                xts = xt if n_seq == 1 else jnp.where(seq_of_row == s, xt, 0.0)
                stt = _mm_tn(xts, bg)
                sq = st * n_seq + s
                for j in range(hg):
                    hh = hg * g + j
                    rs = slice(P_B * hh, P_B * (hh + 1))
                    drow = gt["dec_x"][r_seq * s:r_seq * s + 1, 128 * (H_A + hh):128 * (H_A + hh + 1)]
                    h_wr[sq, rs, :] = h_rd[sq, rs, :] * drow + stt[P_B * j:P_B * (j + 1), :]
            z = proj_ref[st, :, OFF_Z + wg * g:OFF_Z + wg * (g + 1)]
            mix_ref[st, :, D_V_A + wg * g:D_V_A + wg * (g + 1)] = _bf(_rms(y * _silu(z), nb_ref[:, gs]))


def _expand_mats():
    ea = np.zeros((128, 128 * H_A), np.float32)
    for h in range(H_A):
        ea[h, 128 * h:128 * (h + 1)] = 1.0
        ea[LANE_G + h, 128 * h:128 * (h + 1)] = 1.0
    eb = np.zeros((128, D_B), np.float32)
    for h in range(H_B):
        eb[LANE_DT + h, P_B * h:P_B * (h + 1)] = 1.0
    ec = np.zeros((128, 128 * (H_A + H_B)), np.float32)
    for h in range(H_A):
        ec[LANE_G + h, 128 * h:128 * (h + 1)] = 1.0
    for h in range(H_B):
        ec[LANE_DT + h, 128 * (H_A + h):128 * (H_A + h + 1)] = 1.0
    return jnp.asarray(ea, BF16), jnp.asarray(eb, BF16), jnp.asarray(ec, BF16)


def _const_spec(shape):
    return pl.BlockSpec(shape, lambda *_: (0,) * len(shape))


def _mixer_params(p):
    ea, eb, ec = _expand_mats()
    return (p["conv_w"], p["conv_b"], p["bias_slab"], p["alog_slab"], p["norm_a_x"], p["norm_b"], p["dskip_x"],
            ea, eb, ec)


def _mixer_prompt(proj, p):
    b, l, n = proj.shape
    nb = PROMPT_STREAMS if b % PROMPT_STREAMS == 0 else 1
    consts = _mixer_params(p)
    const_specs = [_const_spec(a.shape) for a in consts]
    return pl.pallas_call(
        functools.partial(_mixer_kernel, nb, 1, 0, CHUNK),
        grid=(b // nb, l // CHUNK),
        in_specs=[pl.BlockSpec((nb, CHUNK, n), lambda i, t: (i, t, 0))] + const_specs,
        out_specs=[
            pl.BlockSpec((nb, CHUNK, D_V_A + D_B), lambda i, t: (i, t, 0)),
            pl.BlockSpec((nb, H_A, DK_A, DV_A), lambda i, t: (i, 0, 0, 0)),
            pl.BlockSpec((nb, D_B, N_B), lambda i, t: (i, 0, 0)),
        ],
        out_shape=[
            jax.ShapeDtypeStruct((b, l, D_V_A + D_B), BF16),
            jax.ShapeDtypeStruct((b, H_A, DK_A, DV_A), F32),
            jax.ShapeDtypeStruct((b, D_B, N_B), F32),
        ],
        scratch_shapes=[pltpu.VMEM((nb, CHUNK, D_CONV), F32), pltpu.VMEM((nb, 8, D_CONV), F32)],
        compiler_params=pltpu.CompilerParams(dimension_semantics=("parallel", "arbitrary"),
                                             vmem_limit_bytes=VMEM_LIMIT),
        name="mixer_prompt",
    )(proj, *consts)


def _mixer_decode(slab, s_delta, s_ssm, p, l_dec):
    tiles, _, n = slab.shape
    n_seq = CHUNK // SEQ_ROWS
    consts = _mixer_params(p)
    const_specs = [_const_spec(a.shape) for a in consts]
    bs = s_delta.shape[0]
    return pl.pallas_call(
        functools.partial(_mixer_kernel, 1, n_seq, PAD_FRONT, PAD_FRONT + l_dec),
        grid=(tiles,),
        in_specs=[pl.BlockSpec((1, CHUNK, n), lambda i: (i, 0, 0))] + const_specs + [
            pl.BlockSpec((n_seq, H_A, DK_A, DV_A), lambda i: (i, 0, 0, 0)),
            pl.BlockSpec((n_seq, D_B, N_B), lambda i: (i, 0, 0)),
        ],
        out_specs=[
            pl.BlockSpec((1, CHUNK, D_V_A + D_B), lambda i: (i, 0, 0)),
            pl.BlockSpec((n_seq, H_A, DK_A, DV_A), lambda i: (i, 0, 0, 0)),
            pl.BlockSpec((n_seq, D_B, N_B), lambda i: (i, 0, 0)),
        ],
        out_shape=[
            jax.ShapeDtypeStruct((tiles, CHUNK, D_V_A + D_B), BF16),
            jax.ShapeDtypeStruct((bs, H_A, DK_A, DV_A), F32),
            jax.ShapeDtypeStruct((bs, D_B, N_B), F32),
        ],
        scratch_shapes=[pltpu.VMEM((1, CHUNK, D_CONV), F32)],
        compiler_params=pltpu.CompilerParams(dimension_semantics=("parallel",), vmem_limit_bytes=VMEM_LIMIT),
        name="mixer_decode",
    )(slab, *consts, s_delta, s_ssm)


def _outproj_router_kernel(sub, mix_ref, x_ref, wo_ref, nf_ref, wr_ref, br_ref, xmid_ref, route_ref, meta_ref):
    xm = x_ref[...] + jnp.dot(mix_ref[...], wo_ref[...], preferred_element_type=F32)
    xmid_ref[...] = xm
    t = _rms(xm, nf_ref[...])
    t1 = _bf(t)
    t2 = _bf(t - t1.astype(F32))
    w = wr_ref[...]
    w1 = _bf(w)
    w2 = _bf(w - w1.astype(F32))
    d = functools.partial(jnp.dot, preferred_element_type=F32)
    logit = d(t1, w1) + d(t1, w2) + d(t2, w1) + br_ref[...]

    lane = lax.broadcasted_iota(jnp.int32, (1, 128), 1).astype(F32)
    neg = -jnp.inf
    big = 1e9
    is_grp = lane < N_GROUPS_E
    gl = jnp.where(is_grp, logit, neg)
    gmax = jnp.max(gl, axis=-1, keepdims=True)
    gsel = jnp.min(jnp.where(gl == gmax, lane, big), axis=-1, keepdims=True)
    gw = 1.0 / jnp.sum(jnp.exp(jnp.where(is_grp, logit - gmax, neg)), axis=-1, keepdims=True)
    lo = N_GROUPS_E + EXPERTS_PER_GROUP * gsel
    el = jnp.where((lane >= lo) & (lane < lo + EXPERTS_PER_GROUP), logit, neg)
    v1 = jnp.max(el, axis=-1, keepdims=True)
    i1 = jnp.min(jnp.where(el == v1, lane, big), axis=-1, keepdims=True)
    el2 = jnp.where(lane == i1, neg, el)
    v2 = jnp.max(el2, axis=-1, keepdims=True)
    i2 = jnp.min(jnp.where(el2 == v2, lane, big), axis=-1, keepdims=True)
    e = jnp.exp(v2 - v1)
    den = gw / (1.0 + e)
    comb = jnp.where(lane == i1, den, 0.0) + jnp.where(lane == i2, e * den, 0.0)

    tm = logit.shape[0]
    onehot = jnp.where((lane == gsel) & is_grp, 1.0, 0.0)
    ri = lax.broadcasted_iota(jnp.int32, (tm, tm), 0)
    ci = lax.broadcasted_iota(jnp.int32, (tm, tm), 1)
    ranks = jnp.dot(_bf((ri > ci).astype(F32)), _bf(onehot), preferred_element_type=F32)
    cnt = ranks[tm - 1:tm, :] + onehot[tm - 1:tm, :]
    ntile = jnp.floor((cnt + (sub - 1)) * (1.0 / sub))
    li = lax.broadcasted_iota(jnp.int32, (128, 128), 0)
    lj = lax.broadcasted_iota(jnp.int32, (128, 128), 1)
    off = _mm01_r(jnp.broadcast_to(ntile * sub, (8, 128)), _bf((li < lj).astype(F32)))[0:1]
    pos = jnp.sum(onehot * (off + ranks), axis=-1, keepdims=True)
    route_ref[...] = jnp.where(lane == 0.0, gsel, jnp.where(lane == 1.0, pos, comb))
    row8 = lax.broadcasted_iota(jnp.int32, (8, 128), 0)
    meta_ref[...] = jnp.where(row8 == 0, off, ntile)


def _outproj_router(mix, x2d, p, tm, sub):
    t, d = x2d.shape
    nb = t // tm
    return pl.pallas_call(
        functools.partial(_outproj_router_kernel, sub),
        grid=(nb,),
        in_specs=[
            pl.BlockSpec((tm, mix.shape[1]), lambda i: (i, 0)),
            pl.BlockSpec((tm, d), lambda i: (i, 0)),
            _const_spec(p["w_out"].shape),
            _const_spec((1, d)),
            _const_spec(p["w_router"].shape),
            _const_spec((1, 128)),
        ],
        out_specs=[pl.BlockSpec((tm, d), lambda i: (i, 0)), pl.BlockSpec((tm, 128), lambda i: (i, 0)),
                   pl.BlockSpec((None, 8, 128), lambda i: (i, 0, 0))],
        out_shape=[jax.ShapeDtypeStruct((t, d), F32), jax.ShapeDtypeStruct((t, 128), F32),
                   jax.ShapeDtypeStruct((nb, 8, 128), F32)],
        compiler_params=pltpu.CompilerParams(dimension_semantics=("parallel",), vmem_limit_bytes=VMEM_LIMIT),
        name="outproj_router",
    )(mix, x2d, p["w_out"], p["norm_ffn"], p["w_router"], p["b_router"])


def _moe_kernel(tb, meta_ref, pos_ref, xmid_ref, route_ref, nffn_ref, nfin_ref, wg_ref, wu_ref, wd_ref,
                y_ref, xg_scr, rg_scr, yg_scr):
    b = pl.program_id(0)
    e16 = pl.program_id(1)
    g = e16 // EXPERTS_PER_GROUP
    unroll = 8

    @pl.when(e16 == 0)
    def _():
        y_ref[...] = _rms(xmid_ref[...], nffn_ref[...])
        xg_scr[...] = jnp.zeros_like(xg_scr)
        rg_scr[...] = jnp.zeros_like(rg_scr)

        def dispatch(i, carry):
            base = pl.multiple_of(i * unroll, unroll)
            for u in range(unroll):
                p = pos_ref[0, base + u]
                xg_scr[pl.ds(p, 1), :] = y_ref[pl.ds(base + u, 1), :]
                rg_scr[pl.ds(p, 1), :] = route_ref[pl.ds(base + u, 1), :]
            return carry

        lax.fori_loop(0, tb // unroll, dispatch, 0)

    off = meta_ref[b, g]
    nq = meta_ref[b, N_GROUPS_E + g]
    lane = lax.broadcasted_iota(jnp.int32, (1, 128), 1)
    first = (e16 % EXPERTS_PER_GROUP) == 0
    d = functools.partial(jnp.dot, preferred_element_type=F32)

    def tile(r0, m):
        r0 = pl.multiple_of(r0, MOE_ALIGN)
        x = _bf(xg_scr[pl.ds(r0, m), :])
        hgate = d(x, wg_ref[...])
        hup = d(x, wu_ref[...])
        cw = jnp.sum(jnp.where(lane == N_GROUPS_E + e16, rg_scr[pl.ds(r0, m), :], 0.0), axis=-1, keepdims=True)
        contrib = d(_bf(_silu(hgate) * hup * cw), wd_ref[...])

        @pl.when(first)
        def _():
            yg_scr[pl.ds(r0, m), :] = contrib

        @pl.when(jnp.logical_not(first))
        def _():
            yg_scr[pl.ds(r0, m), :] += contrib

    per_tile = MOE_SUB // MOE_ALIGN
    nloop = jnp.maximum(nq // per_tile - 1, 0)

    def full_tile(j, carry):
        tile(off + j * MOE_SUB, MOE_SUB)
        return carry

    lax.fori_loop(0, nloop, full_tile, 0)
    last = nq - nloop * per_tile
    r_last = off + nloop * MOE_SUB
    for q in range(1, 2 * per_tile):
        @pl.when(last == q)
        def _(q=q):
            tile(r_last, q * MOE_ALIGN)

    @pl.when(e16 == N_EXPERTS - 1)
    def _():
        def combine(i, carry):
            base = pl.multiple_of(i * unroll, unroll)
            for u in range(unroll):
                p = pos_ref[0, base + u]
                y_ref[pl.ds(base + u, 1), :] = yg_scr[pl.ds(p, 1), :]
            return carry

        lax.fori_loop(0, tb // unroll, combine, 0)
        y_ref[...] = _rms(xmid_ref[...] + y_ref[...], nfin_ref[...])


def _moe(xmid, route, meta, p, tb):
    t, d = xmid.shape
    nb = t // tb
    rows = tb + N_GROUPS_E * MOE_ALIGN
    pos = route[:, 1].astype(jnp.int32).reshape(nb, 1, tb)
    meta_i = jnp.concatenate([meta[:, 0, :N_GROUPS_E], meta[:, 1, :N_GROUPS_E]], axis=1).astype(jnp.int32)
    grid_spec = pltpu.PrefetchScalarGridSpec(
        num_scalar_prefetch=1,
        grid=(nb, N_EXPERTS),
        in_specs=[
            pl.BlockSpec((None, 1, tb), lambda i, e, m: (i, 0, 0), memory_space=pltpu.SMEM),
            pl.BlockSpec((tb, d), lambda i, e, m: (i, 0)),
            pl.BlockSpec((tb, 128), lambda i, e, m: (i, 0)),
            pl.BlockSpec((1, d), lambda i, e, m: (0, 0)),
            pl.BlockSpec((1, d), lambda i, e, m: (0, 0)),
            pl.BlockSpec((None, d, D_EXPERT), lambda i, e, m: (e, 0, 0)),
            pl.BlockSpec((None, d, D_EXPERT), lambda i, e, m: (e, 0, 0)),
            pl.BlockSpec((None, D_EXPERT, d), lambda i, e, m: (e, 0, 0)),
        ],
        out_specs=pl.BlockSpec((tb, d), lambda i, e, m: (i, 0)),
        scratch_shapes=[pltpu.VMEM((rows, d), F32), pltpu.VMEM((rows, 128), F32), pltpu.VMEM((rows, d), F32)],
    )
    return pl.pallas_call(
        functools.partial(_moe_kernel, tb),
        grid_spec=grid_spec,
        out_shape=jax.ShapeDtypeStruct((t, d), F32),
        compiler_params=pltpu.CompilerParams(dimension_semantics=("parallel", "arbitrary"),
                                             vmem_limit_bytes=VMEM_LIMIT),
        name="moe",
    )(meta_i, pos, xmid, route, p["norm_ffn"], p["norm_final"], p["w_gate"], p["w_up"], p["w_down"])


def _pick_tile(t, pref):
    tm = min(pref, t)
    while t % tm:
        tm //= 2
    return tm


def _prep_layer(l, norm_mix, w_in, conv_a_w, a_log_a, dt_bias_a, norm_a, conv_b_w, conv_b_b, a_log_b, dt_bias_b,
                d_skip_b, norm_b, w_out, norm_ffn, w_router_group, b_router_group, w_router_expert,
                b_router_expert, w_gate_e, w_up_e, w_down_e, norm_final):
    w = w_in[l]
    cuts = np.cumsum([D_QKV_A, D_V_A, H_A, H_A, D_B, D_XBC]).tolist()
    w_qkv, w_gate, w_beta, w_alpha, w_z, w_xbc, w_dt = jnp.split(w, cuts, axis=1)
    w_small = jnp.concatenate([w_beta, w_alpha, w_dt, jnp.zeros((D_MODEL, 128 - 2 * H_A - H_B), F32)], axis=1)
    w_all = _bf(jnp.concatenate([w_qkv, w_xbc, w_gate, w_z, w_small], axis=1))

    def slab(a8, b16):
        return jnp.concatenate([jnp.zeros((H_A,), F32), a8.astype(F32), b16.astype(F32),
                                jnp.zeros((128 - 2 * H_A - H_B,), F32)]).reshape(1, 128)

    w_router = jnp.concatenate(
        [w_router_group[l].astype(F32), w_router_expert[l].reshape(D_MODEL, N_EXPERTS).astype(F32),
         jnp.zeros((D_MODEL, 128 - N_GROUPS_E - N_EXPERTS), F32)], axis=1)
    b_router = jnp.concatenate(
        [b_router_group[l].astype(F32), b_router_expert[l].reshape(N_EXPERTS).astype(F32),
         jnp.zeros((128 - N_GROUPS_E - N_EXPERTS,), F32)]).reshape(1, 128)
    return {
        "norm_mix": norm_mix[l].astype(F32).reshape(1, D_MODEL),
        "w_all": w_all,
        "conv_w": jnp.concatenate([conv_a_w[l], conv_b_w[l]], axis=1).astype(F32),
        "conv_b": jnp.concatenate([jnp.zeros((D_QKV_A,), F32), conv_b_b[l].astype(F32)]).reshape(1, D_CONV),
        "bias_slab": slab(dt_bias_a[l], dt_bias_b[l]),
        "alog_slab": slab(a_log_a[l], a_log_b[l]),
        "norm_a_x": jnp.tile(norm_a[l].astype(F32), H_A).reshape(1, D_V_A),
        "norm_b": norm_b[l].astype(F32).reshape(1, D_B),
        "dskip_x": jnp.repeat(d_skip_b[l].astype(F32), P_B).reshape(1, D_B),
        "w_out": _bf(w_out[l]),
        "norm_ffn": norm_ffn[l].astype(F32).reshape(1, D_MODEL),
        "w_router": w_router,
        "b_router": b_router,
        "w_gate": _bf(w_gate_e[l].reshape(N_EXPERTS, D_MODEL, D_EXPERT)),
        "w_up": _bf(w_up_e[l].reshape(N_EXPERTS, D_MODEL, D_EXPERT)),
        "w_down": _bf(w_down_e[l].reshape(N_EXPERTS, D_EXPERT, D_MODEL)),
        "norm_final": norm_final.astype(F32).reshape(1, D_MODEL),
    }


def _ffn_tail(mix2d, x2d, p):
    t = x2d.shape[0]
    tb = _pick_tile(t, MOE_BLOCK)
    xmid, route, meta = _outproj_router(mix2d, x2d, p, tb, MOE_ALIGN)
    return _moe(xmid, route, meta, p, tb)


def kernel(x_prompt, x_sample, state_delta, state_delta_conv, state_ssm, state_ssm_conv, norm_mix, w_in, conv_a_w,
           a_log_a, dt_bias_a, norm_a, conv_b_w, conv_b_b, a_log_b, dt_bias_b, d_skip_b, norm_b, w_out, norm_ffn,
           w_router_group, b_router_group, w_router_expert, b_router_expert, w_gate_e, w_up_e, w_down_e,
           norm_final):
    depth = w_in.shape[0]
    assert depth == 1, "the fused final norm assumes a single layer"
    bp, lp, d = x_prompt.shape
    bs, ls, _ = x_sample.shape
    assert lp % CHUNK == 0 and lp >= CONV_W - 1
    assert PAD_FRONT + ls <= SEQ_ROWS and bs % (CHUNK // SEQ_ROWS) == 0
    l = 0
    p = _prep_layer(l, norm_mix, w_in, conv_a_w, a_log_a, dt_bias_a, norm_a, conv_b_w, conv_b_b, a_log_b,
                    dt_bias_b, d_skip_b, norm_b, w_out, norm_ffn, w_router_group, b_router_group,
                    w_router_expert, b_router_expert, w_gate_e, w_up_e, w_down_e, norm_final)

    xp2 = x_prompt.reshape(bp * lp, d)
    proj_p = _inproj(xp2, p["norm_mix"], p["w_all"], _pick_tile(bp * lp, 256)).reshape(bp, lp, D_PROJ)
    mix_p, delta_p, ssm_p = _mixer_prompt(proj_p, p)
    dconv_p = proj_p[:, lp - (CONV_W - 1):, :D_QKV_A]
    sconv_p = proj_p[:, lp - (CONV_W - 1):, D_QKV_A:D_CONV]
    y_p = _ffn_tail(mix_p.reshape(bp * lp, -1), xp2, p).reshape(bp, lp, d)

    xs2 = x_sample.reshape(bs * ls, d)
    proj_s = _inproj(xs2, p["norm_mix"], p["w_all"], _pick_tile(bs * ls, 256)).reshape(bs, ls, D_PROJ)
    conv_rows = jnp.concatenate(
        [state_delta_conv[l].astype(F32), state_ssm_conv[l].astype(F32),
         jnp.zeros((bs, CONV_W - 1, D_PROJ - D_CONV), F32)], axis=-1)
    slab = jnp.concatenate(
        [conv_rows, proj_s, jnp.zeros((bs, SEQ_ROWS - PAD_FRONT - ls, D_PROJ), F32)], axis=1)
    mix_s, delta_s, ssm_s = _mixer_decode(slab.reshape(bs * SEQ_ROWS // CHUNK, CHUNK, D_PROJ), state_delta[l].astype(F32),
                                          state_ssm[l].astype(F32).reshape(bs, D_B, N_B), p, ls)
    mix_s = mix_s.reshape(bs, SEQ_ROWS, -1)[:, PAD_FRONT:PAD_FRONT + ls].reshape(bs * ls, -1)
    dconv_s = slab[:, ls:ls + CONV_W - 1, :D_QKV_A]
    sconv_s = slab[:, ls:ls + CONV_W - 1, D_QKV_A:D_CONV]
    y_s = _ffn_tail(mix_s, xs2, p).reshape(bs, ls, d)

    return (y_p.astype(x_prompt.dtype), y_s.astype(x_sample.dtype),
            delta_p[None], dconv_p[None], ssm_p.reshape(bp, H_B, P_B, N_B)[None], sconv_p[None],
            delta_s[None], dconv_s[None], ssm_s.reshape(bs, H_B, P_B, N_B)[None], sconv_s[None])
```

```python
import functools

import numpy as np
import jax
import jax.numpy as jnp
from jax import lax
from jax.experimental import pallas as pl
from jax.experimental.pallas import tpu as pltpu

F32 = jnp.float32
BF16 = jnp.bfloat16

D_MODEL = 1024
H_A, DK_A, DV_A = 8, 128, 128
D_QK_A = H_A * DK_A
D_V_A = H_A * DV_A
D_QKV_A = 2 * D_QK_A + D_V_A
H_B, P_B, N_B, G_B = 16, 64, 128, 2
D_B = H_B * P_B
D_XBC = D_B + 2 * G_B * N_B
CONV_W = 4
N_GROUPS_E, EXPERTS_PER_GROUP = 4, 4
N_EXPERTS = N_GROUPS_E * EXPERTS_PER_GROUP
D_EXPERT = 512
EPS = 1e-6

D_CONV = D_QKV_A + D_XBC
OFF_GATE = D_CONV
OFF_Z = OFF_GATE + D_V_A
OFF_SMALL = OFF_Z + D_B
D_PROJ = OFF_SMALL + 128
LANE_G = H_A
LANE_DT = 2 * H_A

CHUNK = 64
PROMPT_CHUNK = 128
SEQ_ROWS = 8
PROMPT_STREAMS = 2
PAD_FRONT = CONV_W - 1
INPROJ_COLS = 256
MOE_BLOCK = 1024
MOE_SUB = 256
MOE_EXPERTS_PER_STEP = 2
MOE_ALIGN = 64
VMEM_LIMIT = 52 * 1024 * 1024


def _bf(x):
    return x.astype(BF16)


def _mm(a, b):
    return jnp.dot(_bf(a), _bf(b), preferred_element_type=F32)


def _mm_nt(a, b):
    return lax.dot_general(_bf(a), _bf(b), (((1,), (1,)), ((), ())), preferred_element_type=F32)


def _mm_tn(a, b):
    return lax.dot_general(_bf(a), _bf(b), (((0,), (0,)), ((), ())), preferred_element_type=F32)


def _split3(x):
    x1 = _bf(x)
    r1 = x - x1.astype(F32)
    x2 = _bf(r1)
    x3 = _bf(r1 - x2.astype(F32))
    return x1, x2, x3


def _mm01_r(x, m01):
    x1, x2, x3 = _split3(x)
    d = functools.partial(jnp.dot, preferred_element_type=F32)
    return d(x1, m01) + d(x2, m01) + d(x3, m01)


def _mm01_r2(x, m01):
    x1 = _bf(x)
    x2 = _bf(x - x1.astype(F32))
    d = functools.partial(jnp.dot, preferred_element_type=F32)
    return d(x1, m01) + d(x2, m01)


def _mm01_l(m01, x):
    x1, x2, x3 = _split3(x)
    d = functools.partial(jnp.dot, preferred_element_type=F32)
    return d(m01, x1) + d(m01, x2) + d(m01, x3)


def _rms(x, w):
    return x * lax.rsqrt(jnp.mean(x * x, axis=-1, keepdims=True) + EPS) * w


def _silu(x):
    return x * jax.nn.sigmoid(x)


def _softplus(x):
    return jnp.maximum(x, 0.0) + jnp.log1p(jnp.exp(-jnp.abs(x)))


def _inproj_kernel(x_ref, nw_ref, w_ref, o_ref):
    h = _rms(x_ref[...], nw_ref[...])
    o_ref[...] = jnp.dot(_bf(h), w_ref[...], preferred_element_type=F32)


def _inproj(x2d, norm_w, w_all, tm):
    t, d = x2d.shape
    n = w_all.shape[1]
    return pl.pallas_call(
        _inproj_kernel,
        grid=(t // tm,),
        in_specs=[
            pl.BlockSpec((tm, d), lambda i: (i, 0)),
            pl.BlockSpec((1, d), lambda i: (0, 0)),
            pl.BlockSpec((d, n), lambda i: (0, 0), pipeline_mode=pl.Buffered(1)),
        ],
        out_specs=pl.BlockSpec((tm, n), lambda i: (i, 0)),
        out_shape=jax.ShapeDtypeStruct((t, n), F32),
        compiler_params=pltpu.CompilerParams(dimension_semantics=("parallel",), vmem_limit_bytes=VMEM_LIMIT),
        name="inproj",
    )(x2d, norm_w, w_all)


def _conv_silu(u, prev8, cw_ref, cb_ref, sl):
    row8 = lax.broadcasted_iota(jnp.int32, (8, 1), 0)
    acc = u * cw_ref[CONV_W - 1:CONV_W, sl] + cb_ref[:, sl]
    for k in range(1, CONV_W):
        ru = pltpu.roll(u, k, 0)
        if prev8 is not None:
            head = jnp.where(row8 < k, pltpu.roll(prev8, k, 0), ru[0:8])
            ru = jnp.concatenate([head, ru[8:]], axis=0)
        acc = acc + ru * cw_ref[CONV_W - 1 - k:CONV_W - k, sl]
    return _silu(acc)


def _inproj_conv_kernel(tiles_per_seq, x_ref, nw_ref, w_ref, cw_ref, cb_ref, o_ref, tail_ref, prev_ref):
    @pl.when(pl.program_id(0) % tiles_per_seq == 0)
    def _():
        prev_ref[...] = jnp.zeros_like(prev_ref)

    h = _bf(_rms(x_ref[...], nw_ref[...]))
    tm = h.shape[0]
    blocks = [slice(c0, min(c0 + INPROJ_COLS, D_PROJ)) for c0 in range(0, D_PROJ, INPROJ_COLS)]
    conv_blocks = [b for b in blocks if b.stop <= D_CONV]
    plain_blocks = [b for b in blocks if b.stop > D_CONV]
    blocks = []
    for j, b in enumerate(conv_blocks):
        blocks.append(b)
        if j % 2 == 1 and plain_blocks:
            blocks.append(plain_blocks.pop(0))
    blocks += plain_blocks
    u_next = jnp.dot(h, w_ref[:, blocks[0]], preferred_element_type=F32)
    for j, sl in enumerate(blocks):
        u = u_next
        if j + 1 < len(blocks):
            u_next = jnp.dot(h, w_ref[:, blocks[j + 1]], preferred_element_type=F32)
        if sl.stop <= D_CONV:
            o_ref[:, sl] = _conv_silu(u, prev_ref[:, sl], cw_ref, cb_ref, sl)
            prev_ref[:, sl] = u[tm - 8:tm]
            tail_ref[:, sl] = u[tm - 8:tm]
        else:
            o_ref[:, sl] = u


def _inproj_conv(x2d, p, tm, seq_len):
    t, d = x2d.shape
    n = p["w_all"].shape[1]
    assert seq_len % tm == 0
    return pl.pallas_call(
        functools.partial(_inproj_conv_kernel, seq_len // tm),
        grid=(t // tm,),
        in_specs=[
            pl.BlockSpec((tm, d), lambda i: (i, 0)),
            pl.BlockSpec((1, d), lambda i: (0, 0)),
            pl.BlockSpec((d, n), lambda i: (0, 0), pipeline_mode=pl.Buffered(1)),
            _const_spec(p["conv_w"].shape),
            _const_spec(p["conv_b"].shape),
        ],
        out_specs=[pl.BlockSpec((tm, n), lambda i: (i, 0)), pl.BlockSpec((None, 8, D_CONV), lambda i: (i, 0, 0))],
        out_shape=[jax.ShapeDtypeStruct((t, n), F32), jax.ShapeDtypeStruct((t // tm, 8, D_CONV), F32)],
        scratch_shapes=[pltpu.VMEM((8, D_CONV), F32)],
        compiler_params=pltpu.CompilerParams(dimension_semantics=("arbitrary",), vmem_limit_bytes=VMEM_LIMIT),
        name="inproj_conv",
    )(x2d, p["norm_mix"], p["w_all"], p["conv_w"], p["conv_b"])


def _inv_unit_lower(mats, ri, ci, eye, merge_sizes):
    blk = (ri >> 3) == (ci >> 3)
    ads = [jnp.where(blk, a, 0.0) for a in mats]
    xs = [eye - ad for ad in ads]
    n = eye.shape[0]
    a2s = [_mm(ad, ad) for ad in ads]
    st = [_mm(jnp.concatenate([x, a2], axis=0), a2) for x, a2 in zip(xs, a2s)]
    xs = [x + t[:n] for x, t in zip(xs, st)]
    xs = [x + _mm(x, t[n:]) for x, t in zip(xs, st)]
    for s in merge_sizes:
        sh = s.bit_length() - 1
        rb = ri >> sh
        sel = ((rb & 1) == 1) & ((ci >> sh) == rb - 1)
        ts = [_mm(x, jnp.where(sel, a, 0.0)) for x, a in zip(xs, mats)]
        xs = [x - _mm(t, x) for x, t in zip(xs, ts)]
    return xs


def _mixer_kernel(c, n_streams, n_seq, valid_lo, valid_hi, *refs):
    carry = n_seq == 1
    (proj_ref, cw_ref, cb_ref, bias_ref, alog_ref, na_ref, nb_ref, dsk_ref,
     ea_ref, eb_ref, ec_ref) = refs[:11]
    if carry:
        mix_ref, s_wr, h_wr = refs[11:]
        s_rd, h_rd = s_wr, h_wr
        xc_ref = proj_ref

        @pl.when(pl.program_id(1) == 0)
        def _():
            s_wr[...] = jnp.zeros_like(s_wr)
            h_wr[...] = jnp.zeros_like(h_wr)
    else:
        s_rd, h_rd, mix_ref, s_wr, h_wr, xc_ref = refs[11:]

    r_seq = c // n_seq
    sh_seq = r_seq.bit_length() - 1
    row = lax.broadcasted_iota(jnp.int32, (c, 1), 0)
    lane = lax.broadcasted_iota(jnp.int32, (1, 128), 1)
    ri = lax.broadcasted_iota(jnp.int32, (c, c), 0)
    ci = lax.broadcasted_iota(jnp.int32, (c, c), 1)
    eye = (ri == ci).astype(F32)
    seq_of_row = row >> sh_seq
    if carry:
        same = None
        causal = ri >= ci
        strict = ri > ci
        valid = None
        merge_sizes = tuple(8 << i for i in range((c // 8).bit_length() - 1))
    else:
        same = (ri >> sh_seq) == (ci >> sh_seq)
        causal = (ri >= ci) & same
        strict = (ri > ci) & same
        rr = row & (r_seq - 1)
        valid = ((rr >= valid_lo) & (rr < valid_hi)).astype(F32)
        merge_sizes = ()
    causal_bf = _bf(causal.astype(F32))
    is_beta = lane < LANE_G
    is_g = (lane >= LANE_G) & (lane < LANE_DT)
    is_dt = (lane >= LANE_DT) & (lane < LANE_DT + H_B)
    streams = range(n_streams)

    def conv(st):
        for c0 in range(0, D_CONV, 512):
            sl = slice(c0, c0 + 512)
            xc_ref[st, :, sl] = _conv_silu(proj_ref[st, :, sl], None, cw_ref, cb_ref, sl)

    def gates(st):
        raw = proj_ref[st, :, OFF_SMALL:OFF_SMALL + 128]
        sp = _softplus(raw + bias_ref[...])
        coef = -jnp.exp(alog_ref[...])
        gd = jnp.where(is_g | is_dt, sp * coef, 0.0)
        q1 = jnp.where(is_beta, jax.nn.sigmoid(raw), jnp.where(is_dt, sp, 0.0))
        if valid is not None:
            gd = gd * valid
            q1 = q1 * valid
        gc = _mm01_l(causal_bf, gd)
        if carry:
            tot = jnp.broadcast_to(gc[c - 1:c, :], (c, 128))
        else:
            tot = _mm01_l(_bf(same.astype(F32)), gd)
        eg = jnp.exp(gc)
        et = jnp.exp(tot - gc)
        xa = _mm01_r2(jnp.concatenate([q1, jnp.where(is_g, eg, 0.0), jnp.where(is_g, et, 0.0)], axis=0),
                      ea_ref[...])
        xb = _mm01_r2(jnp.concatenate([q1, eg, et], axis=0), eb_ref[...])
        etot = jnp.exp(tot[0:8] if carry else tot)
        return {
            "gc": gc, "gct": gc.T,
            "beta_x": xa[0:c], "egc_x": xa[c:2 * c], "tail_x": xa[2 * c:3 * c],
            "dt_x": xb[0:c], "edac_x": xb[c:2 * c], "tailb_x": xb[2 * c:3 * c],
            "dec_x": _mm01_r(etot, ec_ref[...]),
        }

    if not carry:
        for st in streams:
            conv(st)
    gts = [gates(st) for st in streams]

    pairs = [(st, h) for st in streams for h in range(H_A)]
    hsl = [slice(128 * h, 128 * (h + 1)) for h in range(H_A)]
    qs, ks, vs = [], [], []
    for st, h in pairs:
        q = xc_ref[st, :, 128 * h:128 * (h + 1)]
        k = xc_ref[st, :, D_QK_A + 128 * h:D_QK_A + 128 * (h + 1)]
        qs.append(q * lax.rsqrt(jnp.sum(q * q, axis=-1, keepdims=True) + EPS) * (DK_A ** -0.5))
        ks.append(k * lax.rsqrt(jnp.sum(k * k, axis=-1, keepdims=True) + EPS))
        vs.append(xc_ref[st, :, 2 * D_QK_A + 128 * h:2 * D_QK_A + 128 * (h + 1)])
    n_p = len(pairs)
    betas = [gts[st]["beta_x"][:, hsl[h]] for st, h in pairs]
    egcs = [gts[st]["egc_x"][:, hsl[h]] for st, h in pairs]
    kbs = [ks[i] * betas[i] for i in range(n_p)]
    nts = [_mm_nt(jnp.concatenate([kbs[i], qs[i]], axis=0), ks[i]) for i in range(n_p)]
    decs = [jnp.exp(jnp.where(causal, gts[st]["gc"][:, LANE_G + h:LANE_G + h + 1]
                              - gts[st]["gct"][LANE_G + h:LANE_G + h + 1, :], -jnp.inf)) for st, h in pairs]
    a_s = [jnp.where(strict, nts[i][:c] * decs[i], 0.0) for i in range(n_p)]
    qks = [nts[i][c:] * decs[i] for i in range(n_p)]
    x_invs = _inv_unit_lower(a_s, ri, ci, eye, merge_sizes)
    sols = [_mm(x_invs[i], jnp.concatenate([vs[i] * betas[i], kbs[i] * egcs[i]], axis=1)) for i in range(n_p)]
    qes = [qs[i] * egcs[i] for i in range(n_p)]
    wss = [[_mm(jnp.concatenate([sols[i][r_seq * s:r_seq * (s + 1), 128:], qes[i][r_seq * s:r_seq * (s + 1)]],
                                axis=0), s_rd[st * n_seq + s, h]) for s in range(n_seq)]
           for i, (st, h) in enumerate(pairs)]
    vnews, obs = [], []
    for i in range(n_p):
        vn = [sols[i][r_seq * s:r_seq * (s + 1), :128] - wss[i][s][:r_seq] for s in range(n_seq)]
        ob = [wss[i][s][r_seq:] for s in range(n_seq)]
        vnews.append(vn[0] if n_seq == 1 else jnp.concatenate(vn, axis=0))
        obs.append(ob[0] if n_seq == 1 else jnp.concatenate(ob, axis=0))
    os_ = [obs[i] + _mm(qks[i], vnews[i]) for i in range(n_p)]
    for i, (st, h) in enumerate(pairs):
        kt = ks[i] * gts[st]["tail_x"][:, hsl[h]]
        for s in range(n_seq):
            kts = kt if n_seq == 1 else jnp.where(seq_of_row == s, kt, 0.0)
            sq = st * n_seq + s
            s_wr[sq, h] = s_rd[sq, h] * gts[st]["dec_x"][r_seq * s:r_seq * s + 1, hsl[h]] + _mm_tn(kts, vnews[i])
    for i, (st, h) in enumerate(pairs):
        gate = proj_ref[st, :, OFF_GATE + 128 * h:OFF_GATE + 128 * (h + 1)]
        mix_ref[st, :, hsl[h]] = _bf(_rms(os_[i], na_ref[:, hsl[h]]) * _silu(gate))

    hg = H_B // G_B
    wg = hg * P_B
    off_x = D_QKV_A
    off_b = D_QKV_A + D_B
    off_c = off_b + G_B * N_B
    for st, g in [(st, g) for st in streams for g in range(G_B)]:
        gt = gts[st]
        gc, gct = gt["gc"], gt["gct"]
        gs = slice(wg * g, wg * (g + 1))
        bg = xc_ref[st, :, off_b + N_B * g:off_b + N_B * (g + 1)]
        cg = xc_ref[st, :, off_c + N_B * g:off_c + N_B * (g + 1)]
        xs_g = xc_ref[st, :, off_x + wg * g:off_x + wg * (g + 1)]
        xdt = xs_g * gt["dt_x"][:, gs]
        cb = _mm_nt(cg, bg)
        ypairs = []
        for p in range(hg // 2):
            xp = xdt[:, 128 * p:128 * (p + 1)]
            ys = []
            for a_ in (0, 1):
                ln = LANE_DT + hg * g + 2 * p + a_
                seg = jnp.exp(jnp.where(causal, gc[:, ln:ln + 1] - gct[ln:ln + 1, :], -jnp.inf))
                ys.append(_mm(cb * seg, xp))
            ypairs.append(jnp.where(lane < P_B, ys[0], ys[1]))
        y = jnp.concatenate(ypairs, axis=1)
        yoffs = []
        for s in range(n_seq):
            rows = slice(r_seq * s, r_seq * (s + 1))
            yoffs.append(_mm_nt(cg[rows], h_rd[st * n_seq + s, wg * g:wg * (g + 1), :]))
        yoff = yoffs[0] if n_seq == 1 else jnp.concatenate(yoffs, axis=0)
        y = y + yoff * gt["edac_x"][:, gs] + dsk_ref[:, gs] * xs_g
        xt = xdt * gt["tailb_x"][:, gs]
        for s in range(n_seq):
            xts = xt if n_seq == 1 else jnp.where(seq_of_row == s, xt, 0.0)
            stt = _mm_tn(xts, bg)
            sq = st * n_seq + s
            for j in range(hg):
                hh = hg * g + j
                rs = slice(P_B * hh, P_B * (hh + 1))
                drow = gt["dec_x"][r_seq * s:r_seq * s + 1, 128 * (H_A + hh):128 * (H_A + hh + 1)]
                h_wr[sq, rs, :] = h_rd[sq, rs, :] * drow + stt[P_B * j:P_B * (j + 1), :]
        z = proj_ref[st, :, OFF_Z + wg * g:OFF_Z + wg * (g + 1)]
        mix_ref[st, :, D_V_A + wg * g:D_V_A + wg * (g + 1)] = _bf(_rms(y * _silu(z), nb_ref[:, gs]))


def _expand_mats():
    ea = np.zeros((128, 128 * H_A), np.float32)
    for h in range(H_A):
        ea[h, 128 * h:128 * (h + 1)] = 1.0
        ea[LANE_G + h, 128 * h:128 * (h + 1)] = 1.0
    eb = np.zeros((128, D_B), np.float32)
    for h in range(H_B):
        eb[LANE_DT + h, P_B * h:P_B * (h + 1)] = 1.0
    ec = np.zeros((128, 128 * (H_A + H_B)), np.float32)
    for h in range(H_A):
        ec[LANE_G + h, 128 * h:128 * (h + 1)] = 1.0
    for h in range(H_B):
        ec[LANE_DT + h, 128 * (H_A + h):128 * (H_A + h + 1)] = 1.0
    return jnp.asarray(ea, BF16), jnp.asarray(eb, BF16), jnp.asarray(ec, BF16)


def _const_spec(shape):
    return pl.BlockSpec(shape, lambda *_: (0,) * len(shape))


def _mixer_params(p):
    ea, eb, ec = _expand_mats()
    return (p["conv_w"], p["conv_b"], p["bias_slab"], p["alog_slab"], p["norm_a_x"], p["norm_b"], p["dskip_x"],
            ea, eb, ec)


def _mixer_prompt(proj, p):
    b, l, n = proj.shape
    nb = PROMPT_STREAMS if b % PROMPT_STREAMS == 0 else 1
    consts = _mixer_params(p)
    const_specs = [_const_spec(a.shape) for a in consts]
    return pl.pallas_call(
        functools.partial(_mixer_kernel, PROMPT_CHUNK, nb, 1, 0, PROMPT_CHUNK),
        grid=(b // nb, l // PROMPT_CHUNK),
        in_specs=[pl.BlockSpec((nb, PROMPT_CHUNK, n), lambda i, t: (i, t, 0))] + const_specs,
        out_specs=[
            pl.BlockSpec((nb, PROMPT_CHUNK, D_V_A + D_B), lambda i, t: (i, t, 0)),
            pl.BlockSpec((nb, H_A, DK_A, DV_A), lambda i, t: (i, 0, 0, 0)),
            pl.BlockSpec((nb, D_B, N_B), lambda i, t: (i, 0, 0)),
        ],
        out_shape=[
            jax.ShapeDtypeStruct((b, l, D_V_A + D_B), BF16),
            jax.ShapeDtypeStruct((b, H_A, DK_A, DV_A), F32),
            jax.ShapeDtypeStruct((b, D_B, N_B), F32),
        ],
        compiler_params=pltpu.CompilerParams(dimension_semantics=("parallel", "arbitrary"),
                                             vmem_limit_bytes=VMEM_LIMIT),
        name="mixer_prompt",
    )(proj, *consts)


def _mixer_decode(slab, s_delta, s_ssm, p, l_dec):
    tiles, _, n = slab.shape
    n_seq = CHUNK // SEQ_ROWS
    consts = _mixer_params(p)
    const_specs = [_const_spec(a.shape) for a in consts]
    bs = s_delta.shape[0]
    return pl.pallas_call(
        functools.partial(_mixer_kernel, CHUNK, 1, n_seq, PAD_FRONT, PAD_FRONT + l_dec),
        grid=(tiles,),
        in_specs=[pl.BlockSpec((1, CHUNK, n), lambda i: (i, 0, 0))] + const_specs + [
            pl.BlockSpec((n_seq, H_A, DK_A, DV_A), lambda i: (i, 0, 0, 0)),
            pl.BlockSpec((n_seq, D_B, N_B), lambda i: (i, 0, 0)),
        ],
        out_specs=[
            pl.BlockSpec((1, CHUNK, D_V_A + D_B), lambda i: (i, 0, 0)),
            pl.BlockSpec((n_seq, H_A, DK_A, DV_A), lambda i: (i, 0, 0, 0)),
            pl.BlockSpec((n_seq, D_B, N_B), lambda i: (i, 0, 0)),
        ],
        out_shape=[
            jax.ShapeDtypeStruct((tiles, CHUNK, D_V_A + D_B), BF16),
            jax.ShapeDtypeStruct((bs, H_A, DK_A, DV_A), F32),
            jax.ShapeDtypeStruct((bs, D_B, N_B), F32),
        ],
        scratch_shapes=[pltpu.VMEM((1, CHUNK, D_CONV), F32)],
        compiler_params=pltpu.CompilerParams(dimension_semantics=("parallel",), vmem_limit_bytes=VMEM_LIMIT),
        name="mixer_decode",
    )(slab, *consts, s_delta, s_ssm)


def _outproj_router_kernel(sub, mix_ref, x_ref, wo_ref, nf_ref, wr_ref, br_ref, xmid_ref, route_ref, meta_ref):
    xm = x_ref[...] + jnp.dot(mix_ref[...], wo_ref[...], preferred_element_type=F32)
    xmid_ref[...] = xm
    t = _rms(xm, nf_ref[...])
    t1 = _bf(t)
    t2 = _bf(t - t1.astype(F32))
    w = wr_ref[...]
    w1 = _bf(w)
    w2 = _bf(w - w1.astype(F32))
    d = functools.partial(jnp.dot, preferred_element_type=F32)
    logit = d(t1, w1) + d(t1, w2) + d(t2, w1) + br_ref[...]

    lane = lax.broadcasted_iota(jnp.int32, (1, 128), 1).astype(F32)
    neg = -jnp.inf
    big = 1e9
    is_grp = lane < N_GROUPS_E
    gl = jnp.where(is_grp, logit, neg)
    gmax = jnp.max(gl, axis=-1, keepdims=True)
    gsel = jnp.min(jnp.where(gl == gmax, lane, big), axis=-1, keepdims=True)
    gw = 1.0 / jnp.sum(jnp.exp(jnp.where(is_grp, logit - gmax, neg)), axis=-1, keepdims=True)
    lo = N_GROUPS_E + EXPERTS_PER_GROUP * gsel
    el = jnp.where((lane >= lo) & (lane < lo + EXPERTS_PER_GROUP), logit, neg)
    v1 = jnp.max(el, axis=-1, keepdims=True)
    i1 = jnp.min(jnp.where(el == v1, lane, big), axis=-1, keepdims=True)
    el2 = jnp.where(lane == i1, neg, el)
    v2 = jnp.max(el2, axis=-1, keepdims=True)
    i2 = jnp.min(jnp.where(el2 == v2, lane, big), axis=-1, keepdims=True)
    e = jnp.exp(v2 - v1)
    den = gw / (1.0 + e)
    comb = jnp.where(lane == i1, den, 0.0) + jnp.where(lane == i2, e * den, 0.0)

    tm = logit.shape[0]
    onehot = jnp.where((lane == gsel) & is_grp, 1.0, 0.0)
    ri = lax.broadcasted_iota(jnp.int32, (tm, tm), 0)
    ci = lax.broadcasted_iota(jnp.int32, (tm, tm), 1)
    ranks = jnp.dot(_bf((ri > ci).astype(F32)), _bf(onehot), preferred_element_type=F32)
    cnt = ranks[tm - 1:tm, :] + onehot[tm - 1:tm, :]
    ntile = jnp.floor((cnt + (sub - 1)) * (1.0 / sub))
    li = lax.broadcasted_iota(jnp.int32, (128, 128), 0)
    lj = lax.broadcasted_iota(jnp.int32, (128, 128), 1)
    off = _mm01_r(jnp.broadcast_to(ntile * sub, (8, 128)), _bf((li < lj).astype(F32)))[0:1]
    pos = jnp.sum(onehot * (off + ranks), axis=-1, keepdims=True)
    route_ref[...] = jnp.where(lane == 0.0, gsel, jnp.where(lane == 1.0, pos, comb))
    row8 = lax.broadcasted_iota(jnp.int32, (8, 128), 0)
    meta_ref[...] = jnp.where(row8 == 0, off, ntile)


def _outproj_router(mix, x2d, p, tm, sub):
    t, d = x2d.shape
    nb = t // tm
    return pl.pallas_call(
        functools.partial(_outproj_router_kernel, sub),
        grid=(nb,),
        in_specs=[
            pl.BlockSpec((tm, mix.shape[1]), lambda i: (i, 0)),
            pl.BlockSpec((tm, d), lambda i: (i, 0)),
            _const_spec(p["w_out"].shape),
            _const_spec((1, d)),
            _const_spec(p["w_router"].shape),
            _const_spec((1, 128)),
        ],
        out_specs=[pl.BlockSpec((tm, d), lambda i: (i, 0)), pl.BlockSpec((tm, 128), lambda i: (i, 0)),
                   pl.BlockSpec((None, 8, 128), lambda i: (i, 0, 0))],
        out_shape=[jax.ShapeDtypeStruct((t, d), F32), jax.ShapeDtypeStruct((t, 128), F32),
                   jax.ShapeDtypeStruct((nb, 8, 128), F32)],
        compiler_params=pltpu.CompilerParams(dimension_semantics=("parallel",), vmem_limit_bytes=VMEM_LIMIT),
        name="outproj_router",
    )(mix, x2d, p["w_out"], p["norm_ffn"], p["w_router"], p["b_router"])


def _moe_kernel(tb, meta_ref, pos_ref, xmid_ref, route_ref, nffn_ref, nfin_ref, wg_ref, wu_ref, wd_ref,
                y_ref, xg_scr, rg_scr, yg_scr):
    b = pl.program_id(0)
    step = pl.program_id(1)
    n_steps = N_EXPERTS // MOE_EXPERTS_PER_STEP
    steps_per_group = EXPERTS_PER_GROUP // MOE_EXPERTS_PER_STEP
    g = step // steps_per_group
    unroll = 8

    @pl.when(step == 0)
    def _():
        y_ref[...] = _rms(xmid_ref[...], nffn_ref[...])
        xg_scr[...] = jnp.zeros_like(xg_scr)
        rg_scr[...] = jnp.zeros_like(rg_scr)

        def dispatch(i, carry):
            base = pl.multiple_of(i * unroll, unroll)
            src = y_ref.at[pl.ds(base, unroll), :]
            rsrc = route_ref.at[pl.ds(base, unroll), :]
            for u in range(unroll):
                p = pos_ref[0, base + u]
                xg_scr[pl.ds(p, 1), :] = src[u:u + 1, :]
                rg_scr[pl.ds(p, 1), :] = rsrc[u:u + 1, :]
            return carry

        lax.fori_loop(0, tb // unroll, dispatch, 0)

    off = meta_ref[b, g]
    nq = meta_ref[b, N_GROUPS_E + g]
    lane = lax.broadcasted_iota(jnp.int32, (1, 128), 1)
    first = (step % steps_per_group) == 0
    d = functools.partial(jnp.dot, preferred_element_type=F32)

    def tile(r0, m):
        r0 = pl.multiple_of(r0, MOE_ALIGN)
        x = _bf(xg_scr[pl.ds(r0, m), :])
        rg = rg_scr[pl.ds(r0, m), :]
        contrib = None
        for j in range(MOE_EXPERTS_PER_STEP):
            e16 = step * MOE_EXPERTS_PER_STEP + j
            cw = jnp.sum(jnp.where(lane == N_GROUPS_E + e16, rg, 0.0), axis=-1, keepdims=True)
            hid = _silu(d(x, wg_ref[j])) * d(x, wu_ref[j]) * cw
            cj = d(_bf(hid), wd_ref[j])
            contrib = cj if contrib is None else contrib + cj

        @pl.when(first)
        def _():
            yg_scr[pl.ds(r0, m), :] = contrib

        @pl.when(jnp.logical_not(first))
        def _():
            yg_scr[pl.ds(r0, m), :] += contrib

    per_tile = MOE_SUB // MOE_ALIGN
    nloop = jnp.maximum(nq // per_tile - 1, 0)

    def full_tile(j, carry):
        tile(off + j * MOE_SUB, MOE_SUB)
        return carry

    lax.fori_loop(0, nloop, full_tile, 0)
    last = nq - nloop * per_tile
    r_last = off + nloop * MOE_SUB
    for q in range(1, 2 * per_tile):
        @pl.when(last == q)
        def _(q=q):
            tile(r_last, q * MOE_ALIGN)

    @pl.when(step == n_steps - 1)
    def _():
        def combine(i, carry):
            base = pl.multiple_of(i * unroll, unroll)
            dst = y_ref.at[pl.ds(base, unroll), :]
            for u in range(unroll):
                p = pos_ref[0, base + u]
                dst[u:u + 1, :] = yg_scr[pl.ds(p, 1), :]
            return carry

        lax.fori_loop(0, tb // unroll, combine, 0)
        y_ref[...] = _rms(xmid_ref[...] + y_ref[...], nfin_ref[...])


def _moe(xmid, route, meta, p, tb):
    t, d = xmid.shape
    nb = t // tb
    rows = tb + N_GROUPS_E * MOE_ALIGN
    pos = route[:, 1].astype(jnp.int32).reshape(nb, 1, tb)
    meta_i = jnp.concatenate([meta[:, 0, :N_GROUPS_E], meta[:, 1, :N_GROUPS_E]], axis=1).astype(jnp.int32)
    grid_spec = pltpu.PrefetchScalarGridSpec(
        num_scalar_prefetch=1,
        grid=(nb, N_EXPERTS // MOE_EXPERTS_PER_STEP),
        in_specs=[
            pl.BlockSpec((None, 1, tb), lambda i, e, m: (i, 0, 0), memory_space=pltpu.SMEM),
            pl.BlockSpec((tb, d), lambda i, e, m: (i, 0)),
            pl.BlockSpec((tb, 128), lambda i, e, m: (i, 0)),
            pl.BlockSpec((1, d), lambda i, e, m: (0, 0)),
            pl.BlockSpec((1, d), lambda i, e, m: (0, 0)),
            pl.BlockSpec((MOE_EXPERTS_PER_STEP, d, D_EXPERT), lambda i, e, m: (e, 0, 0)),
            pl.BlockSpec((MOE_EXPERTS_PER_STEP, d, D_EXPERT), lambda i, e, m: (e, 0, 0)),
            pl.BlockSpec((MOE_EXPERTS_PER_STEP, D_EXPERT, d), lambda i, e, m: (e, 0, 0)),
        ],
        out_specs=pl.BlockSpec((tb, d), lambda i, e, m: (i, 0)),
        scratch_shapes=[pltpu.VMEM((rows, d), F32), pltpu.VMEM((rows, 128), F32), pltpu.VMEM((rows, d), F32)],
    )
    return pl.pallas_call(
        functools.partial(_moe_kernel, tb),
        grid_spec=grid_spec,
        out_shape=jax.ShapeDtypeStruct((t, d), F32),
        compiler_params=pltpu.CompilerParams(dimension_semantics=("parallel", "arbitrary"),
                                             vmem_limit_bytes=VMEM_LIMIT),
        name="moe",
    )(meta_i, pos, xmid, route, p["norm_ffn"], p["norm_final"], p["w_gate"], p["w_up"], p["w_down"])


def _pick_tile(t, pref):
    tm = min(pref, t)
    while t % tm:
        tm //= 2
    return tm


def _prep_layer(l, norm_mix, w_in, conv_a_w, a_log_a, dt_bias_a, norm_a, conv_b_w, conv_b_b, a_log_b, dt_bias_b,
                d_skip_b, norm_b, w_out, norm_ffn, w_router_group, b_router_group, w_router_expert,
                b_router_expert, w_gate_e, w_up_e, w_down_e, norm_final):
    w = w_in[l]
    cuts = np.cumsum([D_QKV_A, D_V_A, H_A, H_A, D_B, D_XBC]).tolist()
    w_qkv, w_gate, w_beta, w_alpha, w_z, w_xbc, w_dt = jnp.split(w, cuts, axis=1)
    w_small = jnp.concatenate([w_beta, w_alpha, w_dt, jnp.zeros((D_MODEL, 128 - 2 * H_A - H_B), F32)], axis=1)
    w_all = _bf(jnp.concatenate([w_qkv, w_xbc, w_gate, w_z, w_small], axis=1))

    def slab(a8, b16):
        return jnp.concatenate([jnp.zeros((H_A,), F32), a8.astype(F32), b16.astype(F32),
                                jnp.zeros((128 - 2 * H_A - H_B,), F32)]).reshape(1, 128)

    w_router = jnp.concatenate(
        [w_router_group[l].astype(F32), w_router_expert[l].reshape(D_MODEL, N_EXPERTS).astype(F32),
         jnp.zeros((D_MODEL, 128 - N_GROUPS_E - N_EXPERTS), F32)], axis=1)
    b_router = jnp.concatenate(
        [b_router_group[l].astype(F32), b_router_expert[l].reshape(N_EXPERTS).astype(F32),
         jnp.zeros((128 - N_GROUPS_E - N_EXPERTS,), F32)]).reshape(1, 128)
    return {
        "norm_mix": norm_mix[l].astype(F32).reshape(1, D_MODEL),
        "w_all": w_all,
        "conv_w": jnp.concatenate([conv_a_w[l], conv_b_w[l]], axis=1).astype(F32),
        "conv_b": jnp.concatenate([jnp.zeros((D_QKV_A,), F32), conv_b_b[l].astype(F32)]).reshape(1, D_CONV),
        "bias_slab": slab(dt_bias_a[l], dt_bias_b[l]),
        "alog_slab": slab(a_log_a[l], a_log_b[l]),
        "norm_a_x": jnp.tile(norm_a[l].astype(F32), H_A).reshape(1, D_V_A),
        "norm_b": norm_b[l].astype(F32).reshape(1, D_B),
        "dskip_x": jnp.repeat(d_skip_b[l].astype(F32), P_B).reshape(1, D_B),
        "w_out": _bf(w_out[l]),
        "norm_ffn": norm_ffn[l].astype(F32).reshape(1, D_MODEL),
        "w_router": w_router,
        "b_router": b_router,
        "w_gate": _bf(w_gate_e[l].reshape(N_EXPERTS, D_MODEL, D_EXPERT)),
        "w_up": _bf(w_up_e[l].reshape(N_EXPERTS, D_MODEL, D_EXPERT)),
        "w_down": _bf(w_down_e[l].reshape(N_EXPERTS, D_EXPERT, D_MODEL)),
        "norm_final": norm_final.astype(F32).reshape(1, D_MODEL),
    }


def _ffn_tail(mix2d, x2d, p):
    t = x2d.shape[0]
    tb = _pick_tile(t, MOE_BLOCK)
    xmid, route, meta = _outproj_router(mix2d, x2d, p, tb, MOE_ALIGN)
    return _moe(xmid, route, meta, p, tb)


def kernel(x_prompt, x_sample, state_delta, state_delta_conv, state_ssm, state_ssm_conv, norm_mix, w_in, conv_a_w,
           a_log_a, dt_bias_a, norm_a, conv_b_w, conv_b_b, a_log_b, dt_bias_b, d_skip_b, norm_b, w_out, norm_ffn,
           w_router_group, b_router_group, w_router_expert, b_router_expert, w_gate_e, w_up_e, w_down_e,
           norm_final):
    depth = w_in.shape[0]
    assert depth == 1, "the fused final norm assumes a single layer"
    bp, lp, d = x_prompt.shape
    bs, ls, _ = x_sample.shape
    assert lp % PROMPT_CHUNK == 0 and lp >= CONV_W - 1
    assert PAD_FRONT + ls <= SEQ_ROWS and bs % (CHUNK // SEQ_ROWS) == 0
    l = 0
    p = _prep_layer(l, norm_mix, w_in, conv_a_w, a_log_a, dt_bias_a, norm_a, conv_b_w, conv_b_b, a_log_b,
                    dt_bias_b, d_skip_b, norm_b, w_out, norm_ffn, w_router_group, b_router_group,
                    w_router_expert, b_router_expert, w_gate_e, w_up_e, w_down_e, norm_final)

    xp2 = x_prompt.reshape(bp * lp, d)
    tm_p = _pick_tile(lp, 256)
    proj_p, tails = _inproj_conv(xp2, p, tm_p, lp)
    mix_p, delta_p, ssm_p = _mixer_prompt(proj_p.reshape(bp, lp, D_PROJ), p)
    tails = tails.reshape(bp, lp // tm_p, 8, D_CONV)[:, -1, 8 - (CONV_W - 1):]
    dconv_p = tails[:, :, :D_QKV_A]
    sconv_p = tails[:, :, D_QKV_A:]
    y_p = _ffn_tail(mix_p.reshape(bp * lp, -1), xp2, p).reshape(bp, lp, d)

    xs2 = x_sample.reshape(bs * ls, d)
    proj_s = _inproj(xs2, p["norm_mix"], p["w_all"], _pick_tile(bs * ls, 256)).reshape(bs, ls, D_PROJ)
    conv_rows = jnp.concatenate(
        [state_delta_conv[l].astype(F32), state_ssm_conv[l].astype(F32),
         jnp.zeros((bs, CONV_W - 1, D_PROJ - D_CONV), F32)], axis=-1)
    slab = jnp.concatenate(
        [conv_rows, proj_s, jnp.zeros((bs, SEQ_ROWS - PAD_FRONT - ls, D_PROJ), F32)], axis=1)
    mix_s, delta_s, ssm_s = _mixer_decode(slab.reshape(bs * SEQ_ROWS // CHUNK, CHUNK, D_PROJ), state_delta[l].astype(F32),
                                          state_ssm[l].astype(F32).reshape(bs, D_B, N_B), p, ls)
    mix_s = mix_s.reshape(bs, SEQ_ROWS, -1)[:, PAD_FRONT:PAD_FRONT + ls].reshape(bs * ls, -1)
    dconv_s = slab[:, ls:ls + CONV_W - 1, :D_QKV_A]
    sconv_s = slab[:, ls:ls + CONV_W - 1, D_QKV_A:D_CONV]
    y_s = _ffn_tail(mix_s, xs2, p).reshape(bs, ls, d)

    return (y_p.astype(x_prompt.dtype), y_s.astype(x_sample.dtype),
            delta_p[None], dconv_p[None], ssm_p.reshape(bp, H_B, P_B, N_B)[None], sconv_p[None],
            delta_s[None], dconv_s[None], ssm_s.reshape(bs, H_B, P_B, N_B)[None], sconv_s[None])
```

```python
import functools

import numpy as np
import jax
import jax.numpy as jnp
from jax import lax
from jax.experimental import pallas as pl
from jax.experimental.pallas import tpu as pltpu

F32 = jnp.float32
BF16 = jnp.bfloat16

D_MODEL = 1024
H_A, DK_A, DV_A = 8, 128, 128
D_QK_A = H_A * DK_A
D_V_A = H_A * DV_A
D_QKV_A = 2 * D_QK_A + D_V_A
H_B, P_B, N_B, G_B = 16, 64, 128, 2
D_B = H_B * P_B
D_XBC = D_B + 2 * G_B * N_B
CONV_W = 4
N_GROUPS_E, EXPERTS_PER_GROUP = 4, 4
N_EXPERTS = N_GROUPS_E * EXPERTS_PER_GROUP
D_EXPERT = 512
EPS = 1e-6

OFF_QKV = 0
OFF_GATE = OFF_QKV + D_QKV_A
OFF_Z = OFF_GATE + D_V_A
OFF_XBC = OFF_Z + D_B
OFF_SMALL = OFF_XBC + D_XBC
D_PROJ = OFF_SMALL + 128
D_CONV = D_QKV_A + D_XBC
CONV_RANGES = ((OFF_QKV, D_QKV_A), (OFF_XBC, D_XBC))
LANE_G = H_A
LANE_DT = 2 * H_A

CHUNK = 64
PROMPT_CHUNK = 128
SEQ_ROWS = 8
PROMPT_STREAMS = 2
PAD_FRONT = CONV_W - 1
INPROJ_SPREAD = 1
INPROJ_COLS = 256
MOE_BLOCK = 1024
MOE_SUB = 256
MOE_EXPERTS_PER_STEP = 2
MOE_ALIGN = 64
VMEM_LIMIT = 52 * 1024 * 1024


def _bf(x):
    return x.astype(BF16)


def _mm(a, b):
    return jnp.dot(_bf(a), _bf(b), preferred_element_type=F32)


def _mm_nt(a, b):
    return lax.dot_general(_bf(a), _bf(b), (((1,), (1,)), ((), ())), preferred_element_type=F32)


def _mm_tn(a, b):
    return lax.dot_general(_bf(a), _bf(b), (((0,), (0,)), ((), ())), preferred_element_type=F32)


def _split3(x):
    x1 = _bf(x)
    r1 = x - x1.astype(F32)
    x2 = _bf(r1)
    x3 = _bf(r1 - x2.astype(F32))
    return x1, x2, x3


def _mm01_r(x, m01):
    x1, x2, x3 = _split3(x)
    d = functools.partial(jnp.dot, preferred_element_type=F32)
    return d(x1, m01) + d(x2, m01) + d(x3, m01)


def _mm01_r2(x, m01):
    x1 = _bf(x)
    x2 = _bf(x - x1.astype(F32))
    d = functools.partial(jnp.dot, preferred_element_type=F32)
    return d(x1, m01) + d(x2, m01)


def _mm01_l(m01, x):
    x1, x2, x3 = _split3(x)
    d = functools.partial(jnp.dot, preferred_element_type=F32)
    return d(m01, x1) + d(m01, x2) + d(m01, x3)


def _rms(x, w):
    return x * lax.rsqrt(jnp.mean(x * x, axis=-1, keepdims=True) + EPS) * w


def _silu(x):
    return x * jax.nn.sigmoid(x)


def _softplus(x):
    return jnp.maximum(x, 0.0) + jnp.log1p(jnp.exp(-jnp.abs(x)))


def _weight_blocks(width):
    return [slice(c0, min(c0 + width, D_PROJ)) for c0 in range(0, D_PROJ, width)]


def _is_conv_block(dst):
    return any(lo <= dst.start and dst.stop <= lo + n for lo, n in CONV_RANGES)


def _inproj_kernel(x_ref, nw_ref, w_ref, o_ref):
    h = _bf(_rms(x_ref[...], nw_ref[...]))
    o_ref[...] = jnp.dot(h, w_ref[...], preferred_element_type=F32)


def _weight_specs(p):
    return [pl.BlockSpec(p["w_all"].shape, lambda i: (0, 0), pipeline_mode=pl.Buffered(1))]


def _inproj(x2d, p, tm):
    t, d = x2d.shape
    return pl.pallas_call(
        _inproj_kernel,
        grid=(t // tm,),
        in_specs=[pl.BlockSpec((tm, d), lambda i: (i, 0)), pl.BlockSpec((1, d), lambda i: (0, 0))] + _weight_specs(p),
        out_specs=pl.BlockSpec((tm, D_PROJ), lambda i: (i, 0)),
        out_shape=jax.ShapeDtypeStruct((t, D_PROJ), F32),
        compiler_params=pltpu.CompilerParams(dimension_semantics=("parallel",), vmem_limit_bytes=VMEM_LIMIT),
        name="inproj",
    )(x2d, p["norm_mix"], p["w_all"])


def _conv_silu(u, prev8, cw_ref, cb_ref, sl):
    row8 = lax.broadcasted_iota(jnp.int32, (8, 1), 0)
    acc = u * cw_ref[CONV_W - 1:CONV_W, sl] + cb_ref[:, sl]
    for k in range(1, CONV_W):
        ru = pltpu.roll(u, k, 0)
        if prev8 is not None:
            head = jnp.where(row8 < k, pltpu.roll(prev8, k, 0), ru[0:8])
            ru = jnp.concatenate([head, ru[8:]], axis=0)
        acc = acc + ru * cw_ref[CONV_W - 1 - k:CONV_W - k, sl]
    return _silu(acc)


def _inproj_conv_kernel(tiles_per_seq, x_ref, nw_ref, w_ref, cw_ref, cb_ref, o_ref, tail_ref, prev_ref):
    @pl.when(pl.program_id(0) % tiles_per_seq == 0)
    def _():
        prev_ref[...] = jnp.zeros_like(prev_ref)

    h = _bf(_rms(x_ref[...], nw_ref[...]))
    tm = h.shape[0]
    blocks = _weight_blocks(INPROJ_COLS)
    conv_blocks = [b for b in blocks if _is_conv_block(b)]
    plain_blocks = [b for b in blocks if not _is_conv_block(b)]
    blocks = []
    for j, b in enumerate(conv_blocks):
        blocks.append(b)
        if j % INPROJ_SPREAD == INPROJ_SPREAD - 1 and plain_blocks:
            blocks.append(plain_blocks.pop(0))
    blocks += plain_blocks
    for dst in blocks:
        u = jnp.dot(h, w_ref[:, dst], preferred_element_type=F32)
        if _is_conv_block(dst):
            o_ref[:, dst] = _conv_silu(u, prev_ref[:, dst], cw_ref, cb_ref, dst)
            prev_ref[:, dst] = u[tm - 8:tm]
            t0 = dst.start - OFF_QKV if dst.start < OFF_XBC else dst.start - OFF_XBC + D_QKV_A
            tail_ref[:, t0:t0 + (dst.stop - dst.start)] = u[tm - 8:tm]
        else:
            o_ref[:, dst] = u


def _inproj_conv(x2d, p, tm, seq_len):
    t, d = x2d.shape
    assert seq_len % tm == 0
    return pl.pallas_call(
        functools.partial(_inproj_conv_kernel, seq_len // tm),
        grid=(t // tm,),
        in_specs=[pl.BlockSpec((tm, d), lambda i: (i, 0)), pl.BlockSpec((1, d), lambda i: (0, 0))] + _weight_specs(p)
        + [_const_spec(p["conv_w"].shape), _const_spec(p["conv_b"].shape)],
        out_specs=[pl.BlockSpec((tm, D_PROJ), lambda i: (i, 0)), pl.BlockSpec((None, 8, D_CONV), lambda i: (i, 0, 0))],
        out_shape=[jax.ShapeDtypeStruct((t, D_PROJ), F32), jax.ShapeDtypeStruct((t // tm, 8, D_CONV), F32)],
        scratch_shapes=[pltpu.VMEM((8, D_PROJ), F32)],
        compiler_params=pltpu.CompilerParams(dimension_semantics=("arbitrary",), vmem_limit_bytes=VMEM_LIMIT),
        name="inproj_conv",
    )(x2d, p["norm_mix"], p["w_all"], p["conv_w"], p["conv_b"])


def _inv_unit_lower(mats, ri, ci, eye, merge_sizes):
    blk = (ri >> 3) == (ci >> 3)
    ads = [jnp.where(blk, a, 0.0) for a in mats]
    xs = [eye - ad for ad in ads]
    n = eye.shape[0]
    a2s = [_mm(ad, ad) for ad in ads]
    st = [_mm(jnp.concatenate([x, a2], axis=0), a2) for x, a2 in zip(xs, a2s)]
    xs = [x + t[:n] for x, t in zip(xs, st)]
    xs = [x + _mm(x, t[n:]) for x, t in zip(xs, st)]
    for s in merge_sizes:
        sh = s.bit_length() - 1
        rb = ri >> sh
        sel = ((rb & 1) == 1) & ((ci >> sh) == rb - 1)
        odd = [slice(r, r + s) for r in range(s, n, 2 * s)]
        even = [slice(r, r + s) for r in range(0, n, 2 * s)]
        xo = [jnp.concatenate([x[sl] for sl in odd], axis=0) for x in xs]
        ts = [_mm(o, jnp.where(sel, a, 0.0)) for o, a in zip(xo, mats)]
        xo = [o - _mm(t, x) for o, t, x in zip(xo, ts, xs)]
        xs = [jnp.concatenate([blk for i, ev in enumerate(even) for blk in (x[ev], o[i * s:(i + 1) * s])], axis=0)
              for x, o in zip(xs, xo)]
    return xs


def _mixer_kernel(c, n_streams, n_seq, valid_lo, valid_hi, *refs):
    carry = n_seq == 1
    (proj_ref, cw_ref, cb_ref, bias_ref, alog_ref, na_ref, nb_ref, dsk_ref,
     ea_ref, eb_ref, ec_ref) = refs[:11]
    if carry:
        mix_ref, s_wr, h_wr = refs[11:]
        s_rd, h_rd = s_wr, h_wr
        xc_ref = proj_ref

        @pl.when(pl.program_id(1) == 0)
        def _():
            s_wr[...] = jnp.zeros_like(s_wr)
            h_wr[...] = jnp.zeros_like(h_wr)
    else:
        dconv_ref, sconv_ref, s_rd, h_rd, mix_ref, s_wr, h_wr, xc_ref, tile_ref = refs[11:]
        tok_ref = proj_ref
        l_tok = tok_ref.shape[0] // n_seq
        tile_ref[...] = jnp.zeros_like(tile_ref)
        for s in range(n_seq):
            r0 = (c // n_seq) * s
            tile_ref[0, r0:r0 + CONV_W - 1, OFF_QKV:OFF_QKV + D_QKV_A] = dconv_ref[s]
            tile_ref[0, r0:r0 + CONV_W - 1, OFF_XBC:OFF_XBC + D_XBC] = sconv_ref[s]
            tile_ref[0, r0 + valid_lo:r0 + valid_lo + l_tok, :] = tok_ref[l_tok * s:l_tok * (s + 1), :]
        proj_ref = tile_ref

    r_seq = c // n_seq
    sh_seq = r_seq.bit_length() - 1
    row = lax.broadcasted_iota(jnp.int32, (c, 1), 0)
    lane = lax.broadcasted_iota(jnp.int32, (1, 128), 1)
    ri = lax.broadcasted_iota(jnp.int32, (c, c), 0)
    ci = lax.broadcasted_iota(jnp.int32, (c, c), 1)
    eye = (ri == ci).astype(F32)
    seq_of_row = row >> sh_seq
    if carry:
        same = None
        causal = ri >= ci
        strict = ri > ci
        valid = None
        merge_sizes = tuple(8 << i for i in range((c // 8).bit_length() - 1))
    else:
        same = (ri >> sh_seq) == (ci >> sh_seq)
        causal = (ri >= ci) & same
        strict = (ri > ci) & same
        rr = row & (r_seq - 1)
        valid = ((rr >= valid_lo) & (rr < valid_hi)).astype(F32)
        merge_sizes = ()
    causal_bf = _bf(causal.astype(F32))
    is_beta = lane < LANE_G
    is_g = (lane >= LANE_G) & (lane < LANE_DT)
    is_dt = (lane >= LANE_DT) & (lane < LANE_DT + H_B)
    streams = range(n_streams)

    def conv(st):
        for lo, n in CONV_RANGES:
            for c0 in range(lo, lo + n, 512):
                sl = slice(c0, c0 + 512)
                xc_ref[st, :, sl] = _conv_silu(proj_ref[st, :, sl], None, cw_ref, cb_ref, sl)

    def gates(st):
        raw = proj_ref[st, :, OFF_SMALL:OFF_SMALL + 128]
        sp = _softplus(raw + bias_ref[...])
        coef = -jnp.exp(alog_ref[...])
        gd = jnp.where(is_g | is_dt, sp * coef, 0.0)
        q1 = jnp.where(is_beta, jax.nn.sigmoid(raw), jnp.where(is_dt, sp, 0.0))
        if valid is not None:
            gd = gd * valid
            q1 = q1 * valid
        gc = _mm01_l(causal_bf, gd)
        if carry:
            tot = jnp.broadcast_to(gc[c - 1:c, :], (c, 128))
        else:
            tot = _mm01_l(_bf(same.astype(F32)), gd)
        eg = jnp.exp(gc)
        et = jnp.exp(tot - gc)
        xa = _mm01_r2(jnp.concatenate([q1, jnp.where(is_g, eg, 0.0), jnp.where(is_g, et, 0.0)], axis=0),
                      ea_ref[...])
        xb = _mm01_r2(jnp.concatenate([q1, eg, et], axis=0), eb_ref[...])
        etot = jnp.exp(tot[0:8] if carry else tot)
        return {
            "gc": gc, "gct": gc.T,
            "beta_x": xa[0:c], "egc_x": xa[c:2 * c], "tail_x": xa[2 * c:3 * c],
            "dt_x": xb[0:c], "edac_x": xb[c:2 * c], "tailb_x": xb[2 * c:3 * c],
            "dec_x": _mm01_r(etot, ec_ref[...]),
        }

    if not carry:
        for st in streams:
            conv(st)
    gts = [gates(st) for st in streams]

    pairs = [(st, h) for st in streams for h in range(H_A)]
    hsl = [slice(128 * h, 128 * (h + 1)) for h in range(H_A)]
    qs, ks, vs = [], [], []
    for st, h in pairs:
        q = xc_ref[st, :, OFF_QKV + 128 * h:OFF_QKV + 128 * (h + 1)]
        k = xc_ref[st, :, OFF_QKV + D_QK_A + 128 * h:OFF_QKV + D_QK_A + 128 * (h + 1)]
        qs.append(q * lax.rsqrt(jnp.sum(q * q, axis=-1, keepdims=True) + EPS) * (DK_A ** -0.5))
        ks.append(k * lax.rsqrt(jnp.sum(k * k, axis=-1, keepdims=True) + EPS))
        vs.append(xc_ref[st, :, OFF_QKV + 2 * D_QK_A + 128 * h:OFF_QKV + 2 * D_QK_A + 128 * (h + 1)])
    n_p = len(pairs)
    betas = [gts[st]["beta_x"][:, hsl[h]] for st, h in pairs]
    egcs = [gts[st]["egc_x"][:, hsl[h]] for st, h in pairs]
    kbs = [ks[i] * betas[i] for i in range(n_p)]
    nts = [_mm_nt(jnp.concatenate([kbs[i], qs[i]], axis=0), ks[i]) for i in range(n_p)]
    decs = [jnp.exp(jnp.where(causal, gts[st]["gc"][:, LANE_G + h:LANE_G + h + 1]
                              - gts[st]["gct"][LANE_G + h:LANE_G + h + 1, :], -jnp.inf)) for st, h in pairs]
    a_s = [jnp.where(strict, nts[i][:c] * decs[i], 0.0) for i in range(n_p)]
    qks = [nts[i][c:] * decs[i] for i in range(n_p)]
    x_invs = _inv_unit_lower(a_s, ri, ci, eye, merge_sizes)
    sols = [_mm(x_invs[i], jnp.concatenate([vs[i] * betas[i], kbs[i] * egcs[i]], axis=1)) for i in range(n_p)]
    qes = [qs[i] * egcs[i] for i in range(n_p)]
    if n_seq == 1:
        vnews = [sols[i][:, :128] - _mm(sols[i][:, 128:], s_rd[st, h]) for i, (st, h) in enumerate(pairs)]
        os_ = [_mm(jnp.concatenate([qes[i], qks[i]], axis=1),
                   jnp.concatenate([s_rd[st, h], vnews[i]], axis=0)) for i, (st, h) in enumerate(pairs)]
    else:
        wss = [[_mm(jnp.concatenate([sols[i][r_seq * s:r_seq * (s + 1), 128:], qes[i][r_seq * s:r_seq * (s + 1)]],
                                    axis=0), s_rd[st * n_seq + s, h]) for s in range(n_seq)]
               for i, (st, h) in enumerate(pairs)]
        vnews = [jnp.concatenate([sols[i][r_seq * s:r_seq * (s + 1), :128] - wss[i][s][:r_seq]
                                  for s in range(n_seq)], axis=0) for i in range(n_p)]
        os_ = [jnp.concatenate([wss[i][s][r_seq:] for s in range(n_seq)], axis=0) + _mm(qks[i], vnews[i])
               for i in range(n_p)]
    for i, (st, h) in enumerate(pairs):
        kt = ks[i] * gts[st]["tail_x"][:, hsl[h]]
        for s in range(n_seq):
            kts = kt if n_seq == 1 else jnp.where(seq_of_row == s, kt, 0.0)
            sq = st * n_seq + s
            s_wr[sq, h] = s_rd[sq, h] * gts[st]["dec_x"][r_seq * s:r_seq * s + 1, hsl[h]] + _mm_tn(kts, vnews[i])
    for i, (st, h) in enumerate(pairs):
        gate = proj_ref[st, :, OFF_GATE + 128 * h:OFF_GATE + 128 * (h + 1)]
        mix_ref[st, :, hsl[h]] = _bf(_rms(os_[i], na_ref[:, hsl[h]]) * _silu(gate))

    hg = H_B // G_B
    wg = hg * P_B
    off_x = OFF_XBC
    off_b = off_x + D_B
    off_c = off_b + G_B * N_B
    for st, g in [(st, g) for st in streams for g in range(G_B)]:
        gt = gts[st]
        gc, gct = gt["gc"], gt["gct"]
        gs = slice(wg * g, wg * (g + 1))
        bg = xc_ref[st, :, off_b + N_B * g:off_b + N_B * (g + 1)]
        cg = xc_ref[st, :, off_c + N_B * g:off_c + N_B * (g + 1)]
        xs_g = xc_ref[st, :, off_x + wg * g:off_x + wg * (g + 1)]
        xdt = xs_g * gt["dt_x"][:, gs]
        cb = _mm_nt(cg, bg)
        ypairs = []
        for p in range(hg // 2):
            xp = xdt[:, 128 * p:128 * (p + 1)]
            ms = []
            for a_ in (0, 1):
                ln = LANE_DT + hg * g + 2 * p + a_
                seg = jnp.exp(jnp.where(causal, gc[:, ln:ln + 1] - gct[ln:ln + 1, :], -jnp.inf))
                ms.append(cb * seg)
            xpp = jnp.concatenate([jnp.where(lane < P_B, xp, 0.0), jnp.where(lane >= P_B, xp, 0.0)], axis=0)
            ypairs.append(_mm(jnp.concatenate(ms, axis=1), xpp))
        y = jnp.concatenate(ypairs, axis=1)
        yoffs = []
        for s in range(n_seq):
            rows = slice(r_seq * s, r_seq * (s + 1))
            yoffs.append(_mm_nt(cg[rows], h_rd[st * n_seq + s, wg * g:wg * (g + 1), :]))
        yoff = yoffs[0] if n_seq == 1 else jnp.concatenate(yoffs, axis=0)
        y = y + yoff * gt["edac_x"][:, gs] + dsk_ref[:, gs] * xs_g
        xt = xdt * gt["tailb_x"][:, gs]
        for s in range(n_seq):
            xts = xt if n_seq == 1 else jnp.where(seq_of_row == s, xt, 0.0)
            stt = _mm_tn(xts, bg)
            sq = st * n_seq + s
            for j in range(hg):
                hh = hg * g + j
                rs = slice(P_B * hh, P_B * (hh + 1))
                drow = gt["dec_x"][r_seq * s:r_seq * s + 1, 128 * (H_A + hh):128 * (H_A + hh + 1)]
                h_wr[sq, rs, :] = h_rd[sq, rs, :] * drow + stt[P_B * j:P_B * (j + 1), :]
        z = proj_ref[st, :, OFF_Z + wg * g:OFF_Z + wg * (g + 1)]
        mix_ref[st, :, D_V_A + wg * g:D_V_A + wg * (g + 1)] = _bf(_rms(y * _silu(z), nb_ref[:, gs]))


def _expand_mats():
    ea = np.zeros((128, 128 * H_A), np.float32)
    for h in range(H_A):
        ea[h, 128 * h:128 * (h + 1)] = 1.0
        ea[LANE_G + h, 128 * h:128 * (h + 1)] = 1.0
    eb = np.zeros((128, D_B), np.float32)
    for h in range(H_B):
        eb[LANE_DT + h, P_B * h:P_B * (h + 1)] = 1.0
    ec = np.zeros((128, 128 * (H_A + H_B)), np.float32)
    for h in range(H_A):
        ec[LANE_G + h, 128 * h:128 * (h + 1)] = 1.0
    for h in range(H_B):
        ec[LANE_DT + h, 128 * (H_A + h):128 * (H_A + h + 1)] = 1.0
    return jnp.asarray(ea, BF16), jnp.asarray(eb, BF16), jnp.asarray(ec, BF16)


def _const_spec(shape):
    return pl.BlockSpec(shape, lambda *_: (0,) * len(shape))


def _mixer_params(p):
    ea, eb, ec = _expand_mats()
    return (p["conv_w"], p["conv_b"], p["bias_slab"], p["alog_slab"], p["norm_a_x"], p["norm_b"], p["dskip_x"],
            ea, eb, ec)


def _mixer_prompt(proj, p):
    b, l, n = proj.shape
    nb = PROMPT_STREAMS if b % PROMPT_STREAMS == 0 else 1
    consts = _mixer_params(p)
    const_specs = [_const_spec(a.shape) for a in consts]
    return pl.pallas_call(
        functools.partial(_mixer_kernel, PROMPT_CHUNK, nb, 1, 0, PROMPT_CHUNK),
        grid=(b // nb, l // PROMPT_CHUNK),
        in_specs=[pl.BlockSpec((nb, PROMPT_CHUNK, n), lambda i, t: (i, t, 0))] + const_specs,
        out_specs=[
            pl.BlockSpec((nb, PROMPT_CHUNK, D_V_A + D_B), lambda i, t: (i, t, 0)),
            pl.BlockSpec((nb, H_A, DK_A, DV_A), lambda i, t: (i, 0, 0, 0)),
            pl.BlockSpec((nb, D_B, N_B), lambda i, t: (i, 0, 0)),
        ],
        out_shape=[
            jax.ShapeDtypeStruct((b, l, D_V_A + D_B), BF16),
            jax.ShapeDtypeStruct((b, H_A, DK_A, DV_A), F32),
            jax.ShapeDtypeStruct((b, D_B, N_B), F32),
        ],
        compiler_params=pltpu.CompilerParams(dimension_semantics=("parallel", "arbitrary"),
                                             vmem_limit_bytes=VMEM_LIMIT),
        name="mixer_prompt",
    )(proj, *consts)


def _mixer_decode(proj, dconv, sconv, s_delta, s_ssm, p):
    bs = s_delta.shape[0]
    n = proj.shape[1]
    l_dec = proj.shape[0] // bs
    n_seq = CHUNK // SEQ_ROWS
    tiles = bs // n_seq
    consts = _mixer_params(p)
    const_specs = [_const_spec(a.shape) for a in consts]
    return pl.pallas_call(
        functools.partial(_mixer_kernel, CHUNK, 1, n_seq, PAD_FRONT, PAD_FRONT + l_dec),
        grid=(tiles,),
        in_specs=[pl.BlockSpec((n_seq * l_dec, n), lambda i: (i, 0))] + const_specs + [
            pl.BlockSpec((n_seq, CONV_W - 1, D_QKV_A), lambda i: (i, 0, 0)),
            pl.BlockSpec((n_seq, CONV_W - 1, D_XBC), lambda i: (i, 0, 0)),
            pl.BlockSpec((n_seq, H_A, DK_A, DV_A), lambda i: (i, 0, 0, 0)),
            pl.BlockSpec((n_seq, D_B, N_B), lambda i: (i, 0, 0)),
        ],
        out_specs=[
            pl.BlockSpec((1, CHUNK, D_V_A + D_B), lambda i: (i, 0, 0)),
            pl.BlockSpec((n_seq, H_A, DK_A, DV_A), lambda i: (i, 0, 0, 0)),
            pl.BlockSpec((n_seq, D_B, N_B), lambda i: (i, 0, 0)),
        ],
        out_shape=[
            jax.ShapeDtypeStruct((tiles, CHUNK, D_V_A + D_B), BF16),
            jax.ShapeDtypeStruct((bs, H_A, DK_A, DV_A), F32),
            jax.ShapeDtypeStruct((bs, D_B, N_B), F32),
        ],
        scratch_shapes=[pltpu.VMEM((1, CHUNK, n), F32), pltpu.VMEM((1, CHUNK, n), F32)],
        compiler_params=pltpu.CompilerParams(dimension_semantics=("parallel",), vmem_limit_bytes=VMEM_LIMIT),
        name="mixer_decode",
    )(proj, *consts, dconv, sconv, s_delta, s_ssm)


def _outproj_router_kernel(sub, mix_ref, x_ref, wo_ref, nf_ref, wr_ref, br_ref, xmid_ref, route_ref, meta_ref):
    xm = x_ref[...] + jnp.dot(mix_ref[...], wo_ref[...], preferred_element_type=F32)
    xmid_ref[...] = xm
    t = _rms(xm, nf_ref[...])
    t1 = _bf(t)
    t2 = _bf(t - t1.astype(F32))
    w = wr_ref[...]
    w1 = _bf(w)
    w2 = _bf(w - w1.astype(F32))
    d = functools.partial(jnp.dot, preferred_element_type=F32)
    logit = d(t1, w1) + d(t1, w2) + d(t2, w1) + br_ref[...]

    lane = lax.broadcasted_iota(jnp.int32, (1, 128), 1).astype(F32)
    neg = -jnp.inf
    big = 1e9
    is_grp = lane < N_GROUPS_E
    gl = jnp.where(is_grp, logit, neg)
    gmax = jnp.max(gl, axis=-1, keepdims=True)
    gsel = jnp.min(jnp.where(gl == gmax, lane, big), axis=-1, keepdims=True)
    gw = 1.0 / jnp.sum(jnp.exp(jnp.where(is_grp, logit - gmax, neg)), axis=-1, keepdims=True)
    lo = N_GROUPS_E + EXPERTS_PER_GROUP * gsel
    el = jnp.where((lane >= lo) & (lane < lo + EXPERTS_PER_GROUP), logit, neg)
    v1 = jnp.max(el, axis=-1, keepdims=True)
    i1 = jnp.min(jnp.where(el == v1, lane, big), axis=-1, keepdims=True)
    el2 = jnp.where(lane == i1, neg, el)
    v2 = jnp.max(el2, axis=-1, keepdims=True)
    i2 = jnp.min(jnp.where(el2 == v2, lane, big), axis=-1, keepdims=True)
    e = jnp.exp(v2 - v1)
    den = gw / (1.0 + e)
    comb = jnp.where(lane == i1, den, 0.0) + jnp.where(lane == i2, e * den, 0.0)

    tm = logit.shape[0]
    onehot = jnp.where((lane == gsel) & is_grp, 1.0, 0.0)
    ri = lax.broadcasted_iota(jnp.int32, (tm, tm), 0)
    ci = lax.broadcasted_iota(jnp.int32, (tm, tm), 1)
    ranks = jnp.dot(_bf((ri > ci).astype(F32)), _bf(onehot), preferred_element_type=F32)
    cnt = ranks[tm - 1:tm, :] + onehot[tm - 1:tm, :]
    ntile = jnp.floor((cnt + (sub - 1)) * (1.0 / sub))
    li = lax.broadcasted_iota(jnp.int32, (128, 128), 0)
    lj = lax.broadcasted_iota(jnp.int32, (128, 128), 1)
    off = _mm01_r(jnp.broadcast_to(ntile * sub, (8, 128)), _bf((li < lj).astype(F32)))[0:1]
    pos = jnp.sum(onehot * (off + ranks), axis=-1, keepdims=True)
    route_ref[...] = jnp.where(lane == 0.0, gsel, jnp.where(lane == 1.0, pos, comb))
    row8 = lax.broadcasted_iota(jnp.int32, (8, 128), 0)
    meta_ref[...] = jnp.where(row8 == 0, off, ntile)


def _outproj_router(mix, x2d, p, tm, sub):
    t, d = x2d.shape
    nb = t // tm
    return pl.pallas_call(
        functools.partial(_outproj_router_kernel, sub),
        grid=(nb,),
        in_specs=[
            pl.BlockSpec((tm, mix.shape[1]), lambda i: (i, 0)),
            pl.BlockSpec((tm, d), lambda i: (i, 0)),
            _const_spec(p["w_out"].shape),
            _const_spec((1, d)),
            _const_spec(p["w_router"].shape),
            _const_spec((1, 128)),
        ],
        out_specs=[pl.BlockSpec((tm, d), lambda i: (i, 0)), pl.BlockSpec((tm, 128), lambda i: (i, 0)),
                   pl.BlockSpec((None, 8, 128), lambda i: (i, 0, 0))],
        out_shape=[jax.ShapeDtypeStruct((t, d), F32), jax.ShapeDtypeStruct((t, 128), F32),
                   jax.ShapeDtypeStruct((nb, 8, 128), F32)],
        compiler_params=pltpu.CompilerParams(dimension_semantics=("parallel",), vmem_limit_bytes=VMEM_LIMIT),
        name="outproj_router",
    )(mix, x2d, p["w_out"], p["norm_ffn"], p["w_router"], p["b_router"])


def _moe_kernel(tb, meta_ref, pos_ref, xmid_ref, route_ref, nffn_ref, nfin_ref, wg_ref, wu_ref, wd_ref,
                y_ref, xg_scr, rg_scr, yg_scr):
    b = pl.program_id(0)
    step = pl.program_id(1)
    n_steps = N_EXPERTS // MOE_EXPERTS_PER_STEP
    steps_per_group = EXPERTS_PER_GROUP // MOE_EXPERTS_PER_STEP
    g = step // steps_per_group
    unroll = 8

    @pl.when(step == 0)
    def _():
        y_ref[...] = _rms(xmid_ref[...], nffn_ref[...])
        xg_scr[...] = jnp.zeros_like(xg_scr)
        rg_scr[...] = jnp.zeros_like(rg_scr)

        def dispatch(i, carry):
            base = pl.multiple_of(i * unroll, unroll)
            src = y_ref.at[pl.ds(base, unroll), :]
            rsrc = route_ref.at[pl.ds(base, unroll), :]
            for u in range(unroll):
                p = pos_ref[0, base + u]
                xg_scr[pl.ds(p, 1), :] = src[u:u + 1, :]
                rg_scr[pl.ds(p, 1), :] = rsrc[u:u + 1, :]
            return carry

        lax.fori_loop(0, tb // unroll, dispatch, 0)

    off = meta_ref[b, g]
    nq = meta_ref[b, N_GROUPS_E + g]
    lane = lax.broadcasted_iota(jnp.int32, (1, 128), 1)
    first = (step % steps_per_group) == 0
    d = functools.partial(jnp.dot, preferred_element_type=F32)

    def tile(r0, m):
        r0 = pl.multiple_of(r0, MOE_ALIGN)
        x = _bf(xg_scr[pl.ds(r0, m), :])
        rg = rg_scr[pl.ds(r0, m), :]
        contrib = None
        for j in range(MOE_EXPERTS_PER_STEP):
            e16 = step * MOE_EXPERTS_PER_STEP + j
            cw = jnp.sum(jnp.where(lane == N_GROUPS_E + e16, rg, 0.0), axis=-1, keepdims=True)
            hid = _silu(d(x, wg_ref[j])) * d(x, wu_ref[j]) * cw
            cj = d(_bf(hid), wd_ref[j])
            contrib = cj if contrib is None else contrib + cj

        @pl.when(first)
        def _():
            yg_scr[pl.ds(r0, m), :] = contrib

        @pl.when(jnp.logical_not(first))
        def _():
            yg_scr[pl.ds(r0, m), :] += contrib

    per_tile = MOE_SUB // MOE_ALIGN
    nloop = jnp.maximum(nq // per_tile - 1, 0)

    def full_tile(j, carry):
        tile(off + j * MOE_SUB, MOE_SUB)
        return carry

    lax.fori_loop(0, nloop, full_tile, 0)
    last = nq - nloop * per_tile
    r_last = off + nloop * MOE_SUB
    for q in range(1, 2 * per_tile):
        @pl.when(last == q)
        def _(q=q):
            tile(r_last, q * MOE_ALIGN)

    @pl.when(step == n_steps - 1)
    def _():
        def combine(i, carry):
            base = pl.multiple_of(i * unroll, unroll)
            dst = y_ref.at[pl.ds(base, unroll), :]
            for u in range(unroll):
                p = pos_ref[0, base + u]
                dst[u:u + 1, :] = yg_scr[pl.ds(p, 1), :]
            return carry

        lax.fori_loop(0, tb // unroll, combine, 0)
        y_ref[...] = _rms(xmid_ref[...] + y_ref[...], nfin_ref[...])


def _moe(xmid, route, meta, p, tb):
    t, d = xmid.shape
    nb = t // tb
    rows = tb + N_GROUPS_E * MOE_ALIGN
    pos = route[:, 1].astype(jnp.int32).reshape(nb, 1, tb)
    meta_i = jnp.concatenate([meta[:, 0, :N_GROUPS_E], meta[:, 1, :N_GROUPS_E]], axis=1).astype(jnp.int32)
    grid_spec = pltpu.PrefetchScalarGridSpec(
        num_scalar_prefetch=1,
        grid=(nb, N_EXPERTS // MOE_EXPERTS_PER_STEP),
        in_specs=[
            pl.BlockSpec((None, 1, tb), lambda i, e, m: (i, 0, 0), memory_space=pltpu.SMEM),
            pl.BlockSpec((tb, d), lambda i, e, m: (i, 0)),
            pl.BlockSpec((tb, 128), lambda i, e, m: (i, 0)),
            pl.BlockSpec((1, d), lambda i, e, m: (0, 0)),
            pl.BlockSpec((1, d), lambda i, e, m: (0, 0)),
            pl.BlockSpec((MOE_EXPERTS_PER_STEP, d, D_EXPERT), lambda i, e, m: (e, 0, 0)),
            pl.BlockSpec((MOE_EXPERTS_PER_STEP, d, D_EXPERT), lambda i, e, m: (e, 0, 0)),
            pl.BlockSpec((MOE_EXPERTS_PER_STEP, D_EXPERT, d), lambda i, e, m: (e, 0, 0)),
        ],
        out_specs=pl.BlockSpec((tb, d), lambda i, e, m: (i, 0)),
        scratch_shapes=[pltpu.VMEM((rows, d), F32), pltpu.VMEM((rows, 128), F32), pltpu.VMEM((rows, d), F32)],
    )
    return pl.pallas_call(
        functools.partial(_moe_kernel, tb),
        grid_spec=grid_spec,
        out_shape=jax.ShapeDtypeStruct((t, d), F32),
        compiler_params=pltpu.CompilerParams(dimension_semantics=("parallel", "arbitrary"),
                                             vmem_limit_bytes=VMEM_LIMIT),
        name="moe",
    )(meta_i, pos, xmid, route, p["norm_ffn"], p["norm_final"], p["w_gate"], p["w_up"], p["w_down"])


def _pick_tile(t, pref):
    tm = min(pref, t)
    while t % tm:
        tm //= 2
    return tm


def _prep_layer(l, norm_mix, w_in, conv_a_w, a_log_a, dt_bias_a, norm_a, conv_b_w, conv_b_b, a_log_b, dt_bias_b,
                d_skip_b, norm_b, w_out, norm_ffn, w_router_group, b_router_group, w_router_expert,
                b_router_expert, w_gate_e, w_up_e, w_down_e, norm_final):
    w = w_in[l]
    c_beta = D_QKV_A + D_V_A
    c_z = c_beta + 2 * H_A
    c_dt = c_z + D_B + D_XBC
    w_a = _bf(w[:, :c_beta])
    w_b = _bf(w[:, c_z:c_dt])
    w_s = _bf(jnp.concatenate([w[:, c_beta:c_z], w[:, c_dt:], jnp.zeros((D_MODEL, 128 - 2 * H_A - H_B), F32)], axis=1))

    def on_conv_columns(a_qkv, a_xbc):
        out = jnp.zeros(a_qkv.shape[:-1] + (D_PROJ,), F32)
        out = out.at[..., OFF_QKV:OFF_QKV + D_QKV_A].set(a_qkv.astype(F32))
        return out.at[..., OFF_XBC:OFF_XBC + D_XBC].set(a_xbc.astype(F32))

    def slab(a8, b16):
        return jnp.concatenate([jnp.zeros((H_A,), F32), a8.astype(F32), b16.astype(F32),
                                jnp.zeros((128 - 2 * H_A - H_B,), F32)]).reshape(1, 128)

    w_router = jnp.concatenate(
        [w_router_group[l].astype(F32), w_router_expert[l].reshape(D_MODEL, N_EXPERTS).astype(F32),
         jnp.zeros((D_MODEL, 128 - N_GROUPS_E - N_EXPERTS), F32)], axis=1)
    b_router = jnp.concatenate(
        [b_router_group[l].astype(F32), b_router_expert[l].reshape(N_EXPERTS).astype(F32),
         jnp.zeros((128 - N_GROUPS_E - N_EXPERTS,), F32)]).reshape(1, 128)
    return {
        "norm_mix": norm_mix[l].astype(F32).reshape(1, D_MODEL),
        "w_all": jnp.concatenate([w_a, w_b, w_s], axis=1),
        "conv_w": on_conv_columns(conv_a_w[l], conv_b_w[l]),
        "conv_b": on_conv_columns(jnp.zeros((1, D_QKV_A), F32), conv_b_b[l].reshape(1, D_XBC)),
        "bias_slab": slab(dt_bias_a[l], dt_bias_b[l]),
        "alog_slab": slab(a_log_a[l], a_log_b[l]),
        "norm_a_x": jnp.tile(norm_a[l].astype(F32), H_A).reshape(1, D_V_A),
        "norm_b": norm_b[l].astype(F32).reshape(1, D_B),
        "dskip_x": jnp.repeat(d_skip_b[l].astype(F32), P_B).reshape(1, D_B),
        "w_out": _bf(w_out[l]),
        "norm_ffn": norm_ffn[l].astype(F32).reshape(1, D_MODEL),
        "w_router": w_router,
        "b_router": b_router,
        "w_gate": _bf(w_gate_e[l].reshape(N_EXPERTS, D_MODEL, D_EXPERT)),
        "w_up": _bf(w_up_e[l].reshape(N_EXPERTS, D_MODEL, D_EXPERT)),
        "w_down": _bf(w_down_e[l].reshape(N_EXPERTS, D_EXPERT, D_MODEL)),
        "norm_final": norm_final.astype(F32).reshape(1, D_MODEL),
    }


def _ffn_tail(mix2d, x2d, p):
    t = x2d.shape[0]
    tb = _pick_tile(t, MOE_BLOCK)
    xmid, route, meta = _outproj_router(mix2d, x2d, p, tb, MOE_ALIGN)
    return _moe(xmid, route, meta, p, tb)


def kernel(x_prompt, x_sample, state_delta, state_delta_conv, state_ssm, state_ssm_conv, norm_mix, w_in, conv_a_w,
           a_log_a, dt_bias_a, norm_a, conv_b_w, conv_b_b, a_log_b, dt_bias_b, d_skip_b, norm_b, w_out, norm_ffn,
           w_router_group, b_router_group, w_router_expert, b_router_expert, w_gate_e, w_up_e, w_down_e,
           norm_final):
    depth = w_in.shape[0]
    assert depth == 1, "the fused final norm assumes a single layer"
    bp, lp, d = x_prompt.shape
    bs, ls, _ = x_sample.shape
    assert lp % PROMPT_CHUNK == 0 and lp >= CONV_W - 1
    assert PAD_FRONT + ls <= SEQ_ROWS and bs % (CHUNK // SEQ_ROWS) == 0 and (CHUNK // SEQ_ROWS * ls) % 8 == 0
    l = 0
    p = _prep_layer(l, norm_mix, w_in, conv_a_w, a_log_a, dt_bias_a, norm_a, conv_b_w, conv_b_b, a_log_b,
                    dt_bias_b, d_skip_b, norm_b, w_out, norm_ffn, w_router_group, b_router_group,
                    w_router_expert, b_router_expert, w_gate_e, w_up_e, w_down_e, norm_final)

    xp2 = x_prompt.reshape(bp * lp, d)
    tm_p = _pick_tile(lp, 256)
    proj_p, tails = _inproj_conv(xp2, p, tm_p, lp)
    mix_p, delta_p, ssm_p = _mixer_prompt(proj_p.reshape(bp, lp, D_PROJ), p)
    tails = tails.reshape(bp, lp // tm_p, 8, D_CONV)[:, -1, 8 - (CONV_W - 1):]
    dconv_p = tails[:, :, :D_QKV_A]
    sconv_p = tails[:, :, D_QKV_A:]
    y_p = _ffn_tail(mix_p.reshape(bp * lp, -1), xp2, p).reshape(bp, lp, d)

    xs2 = x_sample.reshape(bs * ls, d)
    proj_s = _inproj(xs2, p, _pick_tile(bs * ls, 256))
    dconv_in = state_delta_conv[l].astype(F32)
    sconv_in = state_ssm_conv[l].astype(F32)
    mix_s, delta_s, ssm_s = _mixer_decode(proj_s, dconv_in, sconv_in, state_delta[l].astype(F32),
                                          state_ssm[l].astype(F32).reshape(bs, D_B, N_B), p)
    mix_s = mix_s.reshape(bs, SEQ_ROWS, -1)[:, PAD_FRONT:PAD_FRONT + ls].reshape(bs * ls, -1)
    dconv_s = jnp.concatenate([dconv_in, proj_s[:, OFF_QKV:OFF_QKV + D_QKV_A].reshape(bs, ls, D_QKV_A)], axis=1)[:, ls:]
    sconv_s = jnp.concatenate([sconv_in, proj_s[:, OFF_XBC:OFF_XBC + D_XBC].reshape(bs, ls, D_XBC)], axis=1)[:, ls:]
    y_s = _ffn_tail(mix_s, xs2, p).reshape(bs, ls, d)

    return (y_p.astype(x_prompt.dtype), y_s.astype(x_sample.dtype),
            delta_p[None], dconv_p[None], ssm_p.reshape(bp, H_B, P_B, N_B)[None], sconv_p[None],
            delta_s[None], dconv_s[None], ssm_s.reshape(bs, H_B, P_B, N_B)[None], sconv_s[None])
```

```python
import functools

import numpy as np
import jax
import jax.numpy as jnp
from jax import lax
from jax.experimental import pallas as pl
from jax.experimental.pallas import tpu as pltpu

F32 = jnp.float32
BF16 = jnp.bfloat16

D_MODEL = 1024
H_A, DK_A, DV_A = 8, 128, 128
D_QK_A = H_A * DK_A
D_V_A = H_A * DV_A
D_QKV_A = 2 * D_QK_A + D_V_A
H_B, P_B, N_B, G_B = 16, 64, 128, 2
D_B = H_B * P_B
D_XBC = D_B + 2 * G_B * N_B
CONV_W = 4
N_GROUPS_E, EXPERTS_PER_GROUP = 4, 4
N_EXPERTS = N_GROUPS_E * EXPERTS_PER_GROUP
D_EXPERT = 512
EPS = 1e-6

OFF_QKV = 0
OFF_GATE = OFF_QKV + D_QKV_A
OFF_Z = OFF_GATE + D_V_A
OFF_XBC = OFF_Z + D_B
OFF_SMALL = OFF_XBC + D_XBC
D_PROJ = OFF_SMALL + 128
D_CONV = D_QKV_A + D_XBC
CONV_RANGES = ((OFF_QKV, D_QKV_A), (OFF_XBC, D_XBC))
LANE_G = H_A
LANE_DT = 2 * H_A

CHUNK = 64
PROMPT_CHUNK = 128
SEQ_ROWS = 8
PROMPT_STREAMS = 2
PAD_FRONT = CONV_W - 1
INPROJ_SPREAD = 1
INPROJ_COLS = 256
MOE_BLOCK = 1024
MOE_SUB = 256
MOE_EXPERTS_PER_STEP = 2
MOE_ALIGN = 64
VMEM_LIMIT = 52 * 1024 * 1024


def _bf(x):
    return x.astype(BF16)


def _mm(a, b):
    return jnp.dot(_bf(a), _bf(b), preferred_element_type=F32)


def _mm_nt(a, b):
    return lax.dot_general(_bf(a), _bf(b), (((1,), (1,)), ((), ())), preferred_element_type=F32)


def _mm_tn(a, b):
    return lax.dot_general(_bf(a), _bf(b), (((0,), (0,)), ((), ())), preferred_element_type=F32)


def _split3(x):
    x1 = _bf(x)
    r1 = x - x1.astype(F32)
    x2 = _bf(r1)
    x3 = _bf(r1 - x2.astype(F32))
    return x1, x2, x3


def _mm01_r(x, m01):
    x1, x2, x3 = _split3(x)
    d = functools.partial(jnp.dot, preferred_element_type=F32)
    return d(x1, m01) + d(x2, m01) + d(x3, m01)


def _mm01_r2(x, m01):
    x1 = _bf(x)
    x2 = _bf(x - x1.astype(F32))
    d = functools.partial(jnp.dot, preferred_element_type=F32)
    return d(x1, m01) + d(x2, m01)


def _mm01_l(m01, x):
    x1, x2, x3 = _split3(x)
    d = functools.partial(jnp.dot, preferred_element_type=F32)
    return d(m01, x1) + d(m01, x2) + d(m01, x3)


def _rms(x, w):
    return x * lax.rsqrt(jnp.mean(x * x, axis=-1, keepdims=True) + EPS) * w


def _silu(x):
    return x * jax.nn.sigmoid(x)


def _softplus(x):
    return jnp.maximum(x, 0.0) + jnp.log1p(jnp.exp(-jnp.abs(x)))


def _weight_blocks(width):
    return [slice(c0, min(c0 + width, D_PROJ)) for c0 in range(0, D_PROJ, width)]


def _is_conv_block(dst):
    return any(lo <= dst.start and dst.stop <= lo + n for lo, n in CONV_RANGES)


def _prep_w_in_kernel(w_ref, o_ref):
    c_beta = D_QKV_A + D_V_A
    c_z = c_beta + 2 * H_A
    c_dt = c_z + D_B + D_XBC
    o_ref[:, OFF_QKV:OFF_Z] = _bf(w_ref[:, 0:c_beta])
    o_ref[:, OFF_Z:OFF_SMALL] = _bf(w_ref[:, c_z:c_dt])
    small = jnp.concatenate([w_ref[:, c_beta:c_z], w_ref[:, c_dt:c_dt + H_B],
                             jnp.zeros((w_ref.shape[0], 128 - 2 * H_A - H_B), F32)], axis=1)
    o_ref[:, OFF_SMALL:D_PROJ] = _bf(small)


def _prep_w_in(w):
    d, n = w.shape
    rows = 128
    return pl.pallas_call(
        _prep_w_in_kernel,
        grid=(d // rows,),
        in_specs=[pl.BlockSpec((rows, n), lambda i: (i, 0))],
        out_specs=pl.BlockSpec((rows, D_PROJ), lambda i: (i, 0)),
        out_shape=jax.ShapeDtypeStruct((d, D_PROJ), BF16),
        compiler_params=pltpu.CompilerParams(dimension_semantics=("parallel",)),
        name="prep_w_in",
    )(w.astype(F32))


def _inproj_kernel(x_ref, nw_ref, w_ref, o_ref):
    h = _bf(_rms(x_ref[...], nw_ref[...]))
    o_ref[...] = jnp.dot(h, w_ref[...], preferred_element_type=F32)


def _weight_specs(p):
    return [pl.BlockSpec(p["w_all"].shape, lambda i: (0, 0), pipeline_mode=pl.Buffered(1))]


def _inproj(x2d, p, tm):
    t, d = x2d.shape
    return pl.pallas_call(
        _inproj_kernel,
        grid=(t // tm,),
        in_specs=[pl.BlockSpec((tm, d), lambda i: (i, 0)), pl.BlockSpec((1, d), lambda i: (0, 0))] + _weight_specs(p),
        out_specs=pl.BlockSpec((tm, D_PROJ), lambda i: (i, 0)),
        out_shape=jax.ShapeDtypeStruct((t, D_PROJ), F32),
        compiler_params=pltpu.CompilerParams(dimension_semantics=("parallel",), vmem_limit_bytes=VMEM_LIMIT),
        name="inproj",
    )(x2d, p["norm_mix"], p["w_all"])


def _conv_silu(u, prev8, cw_ref, cb_ref, sl):
    row8 = lax.broadcasted_iota(jnp.int32, (8, 1), 0)
    acc = u * cw_ref[CONV_W - 1:CONV_W, sl] + cb_ref[:, sl]
    for k in range(1, CONV_W):
        ru = pltpu.roll(u, k, 0)
        if prev8 is not None:
            head = jnp.where(row8 < k, pltpu.roll(prev8, k, 0), ru[0:8])
            ru = jnp.concatenate([head, ru[8:]], axis=0)
        acc = acc + ru * cw_ref[CONV_W - 1 - k:CONV_W - k, sl]
    return _silu(acc)


def _inproj_conv_kernel(tiles_per_seq, x_ref, nw_ref, w_ref, cw_ref, cb_ref, o_ref, tail_ref, prev_ref):
    @pl.when(pl.program_id(0) % tiles_per_seq == 0)
    def _():
        prev_ref[...] = jnp.zeros_like(prev_ref)

    h = _bf(_rms(x_ref[...], nw_ref[...]))
    tm = h.shape[0]
    blocks = _weight_blocks(INPROJ_COLS)
    conv_blocks = [b for b in blocks if _is_conv_block(b)]
    plain_blocks = [b for b in blocks if not _is_conv_block(b)]
    blocks = []
    for j, b in enumerate(conv_blocks):
        blocks.append(b)
        if j % INPROJ_SPREAD == INPROJ_SPREAD - 1 and plain_blocks:
            blocks.append(plain_blocks.pop(0))
    blocks += plain_blocks
    for dst in blocks:
        u = jnp.dot(h, w_ref[:, dst], preferred_element_type=F32)
        if _is_conv_block(dst):
            o_ref[:, dst] = _conv_silu(u, prev_ref[:, dst], cw_ref, cb_ref, dst)
            prev_ref[:, dst] = u[tm - 8:tm]
            t0 = dst.start - OFF_QKV if dst.start < OFF_XBC else dst.start - OFF_XBC + D_QKV_A
            tail_ref[:, t0:t0 + (dst.stop - dst.start)] = u[tm - 8:tm]
        else:
            o_ref[:, dst] = u


def _inproj_conv(x2d, p, tm, seq_len):
    t, d = x2d.shape
    assert seq_len % tm == 0
    return pl.pallas_call(
        functools.partial(_inproj_conv_kernel, seq_len // tm),
        grid=(t // tm,),
        in_specs=[pl.BlockSpec((tm, d), lambda i: (i, 0)), pl.BlockSpec((1, d), lambda i: (0, 0))] + _weight_specs(p)
        + [_const_spec(p["conv_w"].shape), _const_spec(p["conv_b"].shape)],
        out_specs=[pl.BlockSpec((tm, D_PROJ), lambda i: (i, 0)), pl.BlockSpec((None, 8, D_CONV), lambda i: (i, 0, 0))],
        out_shape=[jax.ShapeDtypeStruct((t, D_PROJ), F32), jax.ShapeDtypeStruct((t // tm, 8, D_CONV), F32)],
        scratch_shapes=[pltpu.VMEM((8, D_PROJ), F32)],
        compiler_params=pltpu.CompilerParams(dimension_semantics=("arbitrary",), vmem_limit_bytes=VMEM_LIMIT),
        name="inproj_conv",
    )(x2d, p["norm_mix"], p["w_all"], p["conv_w"], p["conv_b"])


def _inv_unit_lower(mats, ri, ci, eye, merge_sizes):
    blk = (ri >> 3) == (ci >> 3)
    ads = [jnp.where(blk, a, 0.0) for a in mats]
    xs = [eye - ad for ad in ads]
    n = eye.shape[0]
    a2s = [_mm(ad, ad) for ad in ads]
    st = [_mm(jnp.concatenate([x, a2], axis=0), a2) for x, a2 in zip(xs, a2s)]
    xs = [x + t[:n] for x, t in zip(xs, st)]
    xs = [x + _mm(x, t[n:]) for x, t in zip(xs, st)]
    for s in merge_sizes:
        sh = s.bit_length() - 1
        rb = ri >> sh
        sel = ((rb & 1) == 1) & ((ci >> sh) == rb - 1)
        odd = [slice(r, r + s) for r in range(s, n, 2 * s)]
        even = [slice(r, r + s) for r in range(0, n, 2 * s)]
        xo = [jnp.concatenate([x[sl] for sl in odd], axis=0) for x in xs]
        ts = [_mm(o, jnp.where(sel, a, 0.0)) for o, a in zip(xo, mats)]
        xo = [o - _mm(t, x) for o, t, x in zip(xo, ts, xs)]
        xs = [jnp.concatenate([blk for i, ev in enumerate(even) for blk in (x[ev], o[i * s:(i + 1) * s])], axis=0)
              for x, o in zip(xs, xo)]
    return xs


def _mixer_kernel(c, n_streams, n_seq, valid_lo, valid_hi, *refs):
    carry = n_seq == 1
    (proj_ref, cw_ref, cb_ref, bias_ref, alog_ref, na_ref, nb_ref, dsk_ref,
     ea_ref, eb_ref, ec_ref) = refs[:11]
    if carry:
        mix_ref, s_wr, h_wr = refs[11:]
        s_rd, h_rd = s_wr, h_wr
        xc_ref = proj_ref

        @pl.when(pl.program_id(1) == 0)
        def _():
            s_wr[...] = jnp.zeros_like(s_wr)
            h_wr[...] = jnp.zeros_like(h_wr)
    else:
        dconv_ref, sconv_ref, s_rd, h_rd, mix_ref, s_wr, h_wr, xc_ref, tile_ref = refs[11:]
        tok_ref = proj_ref
        l_tok = tok_ref.shape[0] // n_seq
        tile_ref[...] = jnp.zeros_like(tile_ref)
        for s in range(n_seq):
            r0 = (c // n_seq) * s
            tile_ref[0, r0:r0 + CONV_W - 1, OFF_QKV:OFF_QKV + D_QKV_A] = dconv_ref[s]
            tile_ref[0, r0:r0 + CONV_W - 1, OFF_XBC:OFF_XBC + D_XBC] = sconv_ref[s]
            tile_ref[0, r0 + valid_lo:r0 + valid_lo + l_tok, :] = tok_ref[l_tok * s:l_tok * (s + 1), :]
        proj_ref = tile_ref

    r_seq = c // n_seq
    sh_seq = r_seq.bit_length() - 1
    row = lax.broadcasted_iota(jnp.int32, (c, 1), 0)
    lane = lax.broadcasted_iota(jnp.int32, (1, 128), 1)
    ri = lax.broadcasted_iota(jnp.int32, (c, c), 0)
    ci = lax.broadcasted_iota(jnp.int32, (c, c), 1)
    eye = (ri == ci).astype(F32)
    seq_of_row = row >> sh_seq
    if carry:
        same = None
        causal = ri >= ci
        strict = ri > ci
        valid = None
        merge_sizes = tuple(8 << i for i in range((c // 8).bit_length() - 1))
    else:
        same = (ri >> sh_seq) == (ci >> sh_seq)
        causal = (ri >= ci) & same
        strict = (ri > ci) & same
        rr = row & (r_seq - 1)
        valid = ((rr >= valid_lo) & (rr < valid_hi)).astype(F32)
        merge_sizes = ()
    causal_bf = _bf(causal.astype(F32))
    is_beta = lane < LANE_G
    is_g = (lane >= LANE_G) & (lane < LANE_DT)
    is_dt = (lane >= LANE_DT) & (lane < LANE_DT + H_B)
    streams = range(n_streams)

    def conv(st):
        for lo, n in CONV_RANGES:
            for c0 in range(lo, lo + n, 512):
                sl = slice(c0, c0 + 512)
                xc_ref[st, :, sl] = _conv_silu(proj_ref[st, :, sl], None, cw_ref, cb_ref, sl)

    def gates(st):
        raw = proj_ref[st, :, OFF_SMALL:OFF_SMALL + 128]
        sp = _softplus(raw + bias_ref[...])
        coef = -jnp.exp(alog_ref[...])
        gd = jnp.where(is_g | is_dt, sp * coef, 0.0)
        q1 = jnp.where(is_beta, jax.nn.sigmoid(raw), jnp.where(is_dt, sp, 0.0))
        if valid is not None:
            gd = gd * valid
            q1 = q1 * valid
        gc = _mm01_l(causal_bf, gd)
        if carry:
            tot = jnp.broadcast_to(gc[c - 1:c, :], (c, 128))
        else:
            tot = _mm01_l(_bf(same.astype(F32)), gd)
        eg = jnp.exp(gc)
        et = jnp.exp(tot - gc)
        xa = _mm01_r2(jnp.concatenate([q1, jnp.where(is_g, eg, 0.0), jnp.where(is_g, et, 0.0)], axis=0),
                      ea_ref[...])
        xb = _mm01_r2(jnp.concatenate([q1, eg, et], axis=0), eb_ref[...])
        etot = jnp.exp(tot[0:8] if carry else tot)
        return {
            "gc": gc, "gct": gc.T,
            "beta_x": xa[0:c], "egc_x": xa[c:2 * c], "tail_x": xa[2 * c:3 * c],
            "dt_x": xb[0:c], "edac_x": xb[c:2 * c], "tailb_x": xb[2 * c:3 * c],
            "dec_x": _mm01_r(etot, ec_ref[...]),
        }

    if not carry:
        for st in streams:
            conv(st)
    gts = [gates(st) for st in streams]

    pairs = [(st, h) for st in streams for h in range(H_A)]
    hsl = [slice(128 * h, 128 * (h + 1)) for h in range(H_A)]
    qs, ks, vs = [], [], []
    for st, h in pairs:
        q = xc_ref[st, :, OFF_QKV + 128 * h:OFF_QKV + 128 * (h + 1)]
        k = xc_ref[st, :, OFF_QKV + D_QK_A + 128 * h:OFF_QKV + D_QK_A + 128 * (h + 1)]
        qs.append(q * lax.rsqrt(jnp.sum(q * q, axis=-1, keepdims=True) + EPS) * (DK_A ** -0.5))
        ks.append(k * lax.rsqrt(jnp.sum(k * k, axis=-1, keepdims=True) + EPS))
        vs.append(xc_ref[st, :, OFF_QKV + 2 * D_QK_A + 128 * h:OFF_QKV + 2 * D_QK_A + 128 * (h + 1)])
    n_p = len(pairs)
    betas = [gts[st]["beta_x"][:, hsl[h]] for st, h in pairs]
    egcs = [gts[st]["egc_x"][:, hsl[h]] for st, h in pairs]
    kbs = [ks[i] * betas[i] for i in range(n_p)]
    nts = [_mm_nt(jnp.concatenate([kbs[i], qs[i]], axis=0), ks[i]) for i in range(n_p)]
    decs = [jnp.exp(jnp.where(causal, gts[st]["gc"][:, LANE_G + h:LANE_G + h + 1]
                              - gts[st]["gct"][LANE_G + h:LANE_G + h + 1, :], -jnp.inf)) for st, h in pairs]
    a_s = [jnp.where(strict, nts[i][:c] * decs[i], 0.0) for i in range(n_p)]
    qks = [nts[i][c:] * decs[i] for i in range(n_p)]
    x_invs = _inv_unit_lower(a_s, ri, ci, eye, merge_sizes)
    sols = [_mm(x_invs[i], jnp.concatenate([vs[i] * betas[i], kbs[i] * egcs[i]], axis=1)) for i in range(n_p)]
    qes = [qs[i] * egcs[i] for i in range(n_p)]
    if n_seq == 1:
        vnews = [sols[i][:, :128] - _mm(sols[i][:, 128:], s_rd[st, h]) for i, (st, h) in enumerate(pairs)]
        os_ = [_mm(jnp.concatenate([qes[i], qks[i]], axis=1),
                   jnp.concatenate([s_rd[st, h], vnews[i]], axis=0)) for i, (st, h) in enumerate(pairs)]
    else:
        wss = [[_mm(jnp.concatenate([sols[i][r_seq * s:r_seq * (s + 1), 128:], qes[i][r_seq * s:r_seq * (s + 1)]],
                                    axis=0), s_rd[st * n_seq + s, h]) for s in range(n_seq)]
               for i, (st, h) in enumerate(pairs)]
        vnews = [jnp.concatenate([sols[i][r_seq * s:r_seq * (s + 1), :128] - wss[i][s][:r_seq]
                                  for s in range(n_seq)], axis=0) for i in range(n_p)]
        os_ = [jnp.concatenate([wss[i][s][r_seq:] for s in range(n_seq)], axis=0) + _mm(qks[i], vnews[i])
               for i in range(n_p)]
    for i, (st, h) in enumerate(pairs):
        kt = ks[i] * gts[st]["tail_x"][:, hsl[h]]
        for s in range(n_seq):
            kts = kt if n_seq == 1 else jnp.where(seq_of_row == s, kt, 0.0)
            sq = st * n_seq + s
            s_wr[sq, h] = s_rd[sq, h] * gts[st]["dec_x"][r_seq * s:r_seq * s + 1, hsl[h]] + _mm_tn(kts, vnews[i])
    for i, (st, h) in enumerate(pairs):
        gate = proj_ref[st, :, OFF_GATE + 128 * h:OFF_GATE + 128 * (h + 1)]
        mix_ref[st, :, hsl[h]] = _bf(_rms(os_[i], na_ref[:, hsl[h]]) * _silu(gate))

    hg = H_B // G_B
    wg = hg * P_B
    off_x = OFF_XBC
    off_b = off_x + D_B
    off_c = off_b + G_B * N_B
    for st, g in [(st, g) for st in streams for g in range(G_B)]:
        gt = gts[st]
        gc, gct = gt["gc"], gt["gct"]
        gs = slice(wg * g, wg * (g + 1))
        bg = xc_ref[st, :, off_b + N_B * g:off_b + N_B * (g + 1)]
        cg = xc_ref[st, :, off_c + N_B * g:off_c + N_B * (g + 1)]
        xs_g = xc_ref[st, :, off_x + wg * g:off_x + wg * (g + 1)]
        xdt = xs_g * gt["dt_x"][:, gs]
        cb = _mm_nt(cg, bg)
        ypairs = []
        for p in range(hg // 2):
            xp = xdt[:, 128 * p:128 * (p + 1)]
            ms = []
            for a_ in (0, 1):
                ln = LANE_DT + hg * g + 2 * p + a_
                seg = jnp.exp(jnp.where(causal, gc[:, ln:ln + 1] - gct[ln:ln + 1, :], -jnp.inf))
                ms.append(cb * seg)
            xpp = jnp.concatenate([jnp.where(lane < P_B, xp, 0.0), jnp.where(lane >= P_B, xp, 0.0)], axis=0)
            ypairs.append(_mm(jnp.concatenate(ms, axis=1), xpp))
        y = jnp.concatenate(ypairs, axis=1)
        yoffs = []
        for s in range(n_seq):
            rows = slice(r_seq * s, r_seq * (s + 1))
            yoffs.append(_mm_nt(cg[rows], h_rd[st * n_seq + s, wg * g:wg * (g + 1), :]))
        yoff = yoffs[0] if n_seq == 1 else jnp.concatenate(yoffs, axis=0)
        y = y + yoff * gt["edac_x"][:, gs] + dsk_ref[:, gs] * xs_g
        xt = xdt * gt["tailb_x"][:, gs]
        for s in range(n_seq):
            xts = xt if n_seq == 1 else jnp.where(seq_of_row == s, xt, 0.0)
            stt = _mm_tn(xts, bg)
            sq = st * n_seq + s
            for j in range(hg):
                hh = hg * g + j
                rs = slice(P_B * hh, P_B * (hh + 1))
                drow = gt["dec_x"][r_seq * s:r_seq * s + 1, 128 * (H_A + hh):128 * (H_A + hh + 1)]
                h_wr[sq, rs, :] = h_rd[sq, rs, :] * drow + stt[P_B * j:P_B * (j + 1), :]
        z = proj_ref[st, :, OFF_Z + wg * g:OFF_Z + wg * (g + 1)]
        mix_ref[st, :, D_V_A + wg * g:D_V_A + wg * (g + 1)] = _bf(_rms(y * _silu(z), nb_ref[:, gs]))


def _expand_mats():
    ea = np.zeros((128, 128 * H_A), np.float32)
    for h in range(H_A):
        ea[h, 128 * h:128 * (h + 1)] = 1.0
        ea[LANE_G + h, 128 * h:128 * (h + 1)] = 1.0
    eb = np.zeros((128, D_B), np.float32)
    for h in range(H_B):
        eb[LANE_DT + h, P_B * h:P_B * (h + 1)] = 1.0
    ec = np.zeros((128, 128 * (H_A + H_B)), np.float32)
    for h in range(H_A):
        ec[LANE_G + h, 128 * h:128 * (h + 1)] = 1.0
    for h in range(H_B):
        ec[LANE_DT + h, 128 * (H_A + h):128 * (H_A + h + 1)] = 1.0
    return jnp.asarray(ea, BF16), jnp.asarray(eb, BF16), jnp.asarray(ec, BF16)


def _const_spec(shape):
    return pl.BlockSpec(shape, lambda *_: (0,) * len(shape))


def _mixer_params(p):
    ea, eb, ec = _expand_mats()
    return (p["conv_w"], p["conv_b"], p["bias_slab"], p["alog_slab"], p["norm_a_x"], p["norm_b"], p["dskip_x"],
            ea, eb, ec)


def _mixer_prompt(proj, p):
    b, l, n = proj.shape
    nb = PROMPT_STREAMS if b % PROMPT_STREAMS == 0 else 1
    consts = _mixer_params(p)
    const_specs = [_const_spec(a.shape) for a in consts]
    return pl.pallas_call(
        functools.partial(_mixer_kernel, PROMPT_CHUNK, nb, 1, 0, PROMPT_CHUNK),
        grid=(b // nb, l // PROMPT_CHUNK),
        in_specs=[pl.BlockSpec((nb, PROMPT_CHUNK, n), lambda i, t: (i, t, 0))] + const_specs,
        out_specs=[
            pl.BlockSpec((nb, PROMPT_CHUNK, D_V_A + D_B), lambda i, t: (i, t, 0)),
            pl.BlockSpec((nb, H_A, DK_A, DV_A), lambda i, t: (i, 0, 0, 0)),
            pl.BlockSpec((nb, D_B, N_B), lambda i, t: (i, 0, 0)),
        ],
        out_shape=[
            jax.ShapeDtypeStruct((b, l, D_V_A + D_B), BF16),
            jax.ShapeDtypeStruct((b, H_A, DK_A, DV_A), F32),
            jax.ShapeDtypeStruct((b, D_B, N_B), F32),
        ],
        compiler_params=pltpu.CompilerParams(dimension_semantics=("parallel", "arbitrary"),
                                             vmem_limit_bytes=VMEM_LIMIT),
        name="mixer_prompt",
    )(proj, *consts)


def _mixer_decode(proj, dconv, sconv, s_delta, s_ssm, p):
    bs = s_delta.shape[0]
    n = proj.shape[1]
    l_dec = proj.shape[0] // bs
    n_seq = CHUNK // SEQ_ROWS
    tiles = bs // n_seq
    consts = _mixer_params(p)
    const_specs = [_const_spec(a.shape) for a in consts]
    return pl.pallas_call(
        functools.partial(_mixer_kernel, CHUNK, 1, n_seq, PAD_FRONT, PAD_FRONT + l_dec),
        grid=(tiles,),
        in_specs=[pl.BlockSpec((n_seq * l_dec, n), lambda i: (i, 0))] + const_specs + [
            pl.BlockSpec((n_seq, CONV_W - 1, D_QKV_A), lambda i: (i, 0, 0)),
            pl.BlockSpec((n_seq, CONV_W - 1, D_XBC), lambda i: (i, 0, 0)),
            pl.BlockSpec((n_seq, H_A, DK_A, DV_A), lambda i: (i, 0, 0, 0)),
            pl.BlockSpec((n_seq, D_B, N_B), lambda i: (i, 0, 0)),
        ],
        out_specs=[
            pl.BlockSpec((1, CHUNK, D_V_A + D_B), lambda i: (i, 0, 0)),
            pl.BlockSpec((n_seq, H_A, DK_A, DV_A), lambda i: (i, 0, 0, 0)),
            pl.BlockSpec((n_seq, D_B, N_B), lambda i: (i, 0, 0)),
        ],
        out_shape=[
            jax.ShapeDtypeStruct((tiles, CHUNK, D_V_A + D_B), BF16),
            jax.ShapeDtypeStruct((bs, H_A, DK_A, DV_A), F32),
            jax.ShapeDtypeStruct((bs, D_B, N_B), F32),
        ],
        scratch_shapes=[pltpu.VMEM((1, CHUNK, n), F32), pltpu.VMEM((1, CHUNK, n), F32)],
        compiler_params=pltpu.CompilerParams(dimension_semantics=("parallel",), vmem_limit_bytes=VMEM_LIMIT),
        name="mixer_decode",
    )(proj, *consts, dconv, sconv, s_delta, s_ssm)


def _outproj_router_kernel(sub, mix_ref, x_ref, wo_ref, nf_ref, wr_ref, br_ref, xmid_ref, route_ref, meta_ref):
    xm = x_ref[...] + jnp.dot(mix_ref[...], wo_ref[...], preferred_element_type=F32)
    xmid_ref[...] = xm
    t = _rms(xm, nf_ref[...])
    t1 = _bf(t)
    t2 = _bf(t - t1.astype(F32))
    w = wr_ref[...]
    w1 = _bf(w)
    w2 = _bf(w - w1.astype(F32))
    d = functools.partial(jnp.dot, preferred_element_type=F32)
    logit = d(t1, w1) + d(t1, w2) + d(t2, w1) + br_ref[...]

    lane = lax.broadcasted_iota(jnp.int32, (1, 128), 1).astype(F32)
    neg = -jnp.inf
    big = 1e9
    is_grp = lane < N_GROUPS_E
    gl = jnp.where(is_grp, logit, neg)
    gmax = jnp.max(gl, axis=-1, keepdims=True)
    gsel = jnp.min(jnp.where(gl == gmax, lane, big), axis=-1, keepdims=True)
    gw = 1.0 / jnp.sum(jnp.exp(jnp.where(is_grp, logit - gmax, neg)), axis=-1, keepdims=True)
    lo = N_GROUPS_E + EXPERTS_PER_GROUP * gsel
    el = jnp.where((lane >= lo) & (lane < lo + EXPERTS_PER_GROUP), logit, neg)
    v1 = jnp.max(el, axis=-1, keepdims=True)
    i1 = jnp.min(jnp.where(el == v1, lane, big), axis=-1, keepdims=True)
    el2 = jnp.where(lane == i1, neg, el)
    v2 = jnp.max(el2, axis=-1, keepdims=True)
    i2 = jnp.min(jnp.where(el2 == v2, lane, big), axis=-1, keepdims=True)
    e = jnp.exp(v2 - v1)
    den = gw / (1.0 + e)
    comb = jnp.where(lane == i1, den, 0.0) + jnp.where(lane == i2, e * den, 0.0)

    tm = logit.shape[0]
    onehot = jnp.where((lane == gsel) & is_grp, 1.0, 0.0)
    ri = lax.broadcasted_iota(jnp.int32, (tm, tm), 0)
    ci = lax.broadcasted_iota(jnp.int32, (tm, tm), 1)
    ranks = jnp.dot(_bf((ri > ci).astype(F32)), _bf(onehot), preferred_element_type=F32)
    cnt = ranks[tm - 1:tm, :] + onehot[tm - 1:tm, :]
    ntile = jnp.floor((cnt + (sub - 1)) * (1.0 / sub))
    li = lax.broadcasted_iota(jnp.int32, (128, 128), 0)
    lj = lax.broadcasted_iota(jnp.int32, (128, 128), 1)
    off = _mm01_r(jnp.broadcast_to(ntile * sub, (8, 128)), _bf((li < lj).astype(F32)))[0:1]
    pos = jnp.sum(onehot * (off + ranks), axis=-1, keepdims=True)
    route_ref[...] = jnp.where(lane == 0.0, gsel, jnp.where(lane == 1.0, pos, comb))
    row8 = lax.broadcasted_iota(jnp.int32, (8, 128), 0)
    meta_ref[...] = jnp.where(row8 == 0, off, ntile)


def _outproj_router(mix, x2d, p, tm, sub):
    t, d = x2d.shape
    nb = t // tm
    return pl.pallas_call(
        functools.partial(_outproj_router_kernel, sub),
        grid=(nb,),
        in_specs=[
            pl.BlockSpec((tm, mix.shape[1]), lambda i: (i, 0)),
            pl.BlockSpec((tm, d), lambda i: (i, 0)),
            _const_spec(p["w_out"].shape),
            _const_spec((1, d)),
            _const_spec(p["w_router"].shape),
            _const_spec((1, 128)),
        ],
        out_specs=[pl.BlockSpec((tm, d), lambda i: (i, 0)), pl.BlockSpec((tm, 128), lambda i: (i, 0)),
                   pl.BlockSpec((None, 8, 128), lambda i: (i, 0, 0))],
        out_shape=[jax.ShapeDtypeStruct((t, d), F32), jax.ShapeDtypeStruct((t, 128), F32),
                   jax.ShapeDtypeStruct((nb, 8, 128), F32)],
        compiler_params=pltpu.CompilerParams(dimension_semantics=("parallel",), vmem_limit_bytes=VMEM_LIMIT),
        name="outproj_router",
    )(mix, x2d, p["w_out"], p["norm_ffn"], p["w_router"], p["b_router"])


def _moe_kernel(tb, meta_ref, pos_ref, xmid_ref, route_ref, nffn_ref, nfin_ref, wg_ref, wu_ref, wd_ref,
                y_ref, xg_scr, rg_scr, yg_scr):
    b = pl.program_id(0)
    step = pl.program_id(1)
    n_steps = N_EXPERTS // MOE_EXPERTS_PER_STEP
    steps_per_group = EXPERTS_PER_GROUP // MOE_EXPERTS_PER_STEP
    g = step // steps_per_group
    unroll = 8

    @pl.when(step == 0)
    def _():
        y_ref[...] = _rms(xmid_ref[...], nffn_ref[...])
        for gg in range(N_GROUPS_E):
            nq_g = meta_ref[b, N_GROUPS_E + gg]

            @pl.when(nq_g > 0)
            def _(gg=gg, nq_g=nq_g):
                r = pl.multiple_of(meta_ref[b, gg] + (nq_g - 1) * MOE_ALIGN, MOE_ALIGN)
                xg_scr[pl.ds(r, MOE_ALIGN), :] = jnp.zeros((MOE_ALIGN, D_MODEL), F32)
                rg_scr[pl.ds(r, MOE_ALIGN), :] = jnp.zeros((MOE_ALIGN, 128), F32)

        def dispatch(i, carry):
            base = pl.multiple_of(i * unroll, unroll)
            src = y_ref.at[pl.ds(base, unroll), :]
            rsrc = route_ref.at[pl.ds(base, unroll), :]
            for u in range(unroll):
                p = pos_ref[0, base + u]
                xg_scr[pl.ds(p, 1), :] = src[u:u + 1, :]
                rg_scr[pl.ds(p, 1), :] = rsrc[u:u + 1, :]
            return carry

        lax.fori_loop(0, tb // unroll, dispatch, 0)

    off = meta_ref[b, g]
    nq = meta_ref[b, N_GROUPS_E + g]
    lane = lax.broadcasted_iota(jnp.int32, (1, 128), 1)
    first = (step % steps_per_group) == 0
    d = functools.partial(jnp.dot, preferred_element_type=F32)

    def tile(r0, m):
        r0 = pl.multiple_of(r0, MOE_ALIGN)
        x = _bf(xg_scr[pl.ds(r0, m), :])
        rg = rg_scr[pl.ds(r0, m), :]
        contrib = None
        for j in range(MOE_EXPERTS_PER_STEP):
            e16 = step * MOE_EXPERTS_PER_STEP + j
            cw = jnp.sum(jnp.where(lane == N_GROUPS_E + e16, rg, 0.0), axis=-1, keepdims=True)
            hid = _silu(d(x, wg_ref[j])) * d(x, wu_ref[j]) * cw
            cj = d(_bf(hid), wd_ref[j])
            contrib = cj if contrib is None else contrib + cj

        @pl.when(first)
        def _():
            yg_scr[pl.ds(r0, m), :] = contrib

        @pl.when(jnp.logical_not(first))
        def _():
            yg_scr[pl.ds(r0, m), :] += contrib

    per_tile = MOE_SUB // MOE_ALIGN
    nloop = jnp.maximum(nq // per_tile - 1, 0)

    def full_tile(j, carry):
        tile(off + j * MOE_SUB, MOE_SUB)
        return carry

    lax.fori_loop(0, nloop, full_tile, 0)
    last = nq - nloop * per_tile
    r_last = off + nloop * MOE_SUB
    for q in range(1, 2 * per_tile):
        @pl.when(last == q)
        def _(q=q):
            tile(r_last, q * MOE_ALIGN)

    @pl.when(step == n_steps - 1)
    def _():
        def combine(i, carry):
            base = pl.multiple_of(i * unroll, unroll)
            dst = y_ref.at[pl.ds(base, unroll), :]
            for u in range(unroll):
                p = pos_ref[0, base + u]
                dst[u:u + 1, :] = yg_scr[pl.ds(p, 1), :]
            return carry

        lax.fori_loop(0, tb // unroll, combine, 0)
        y_ref[...] = _rms(xmid_ref[...] + y_ref[...], nfin_ref[...])


def _moe(xmid, route, meta, p, tb):
    t, d = xmid.shape
    nb = t // tb
    rows = tb + N_GROUPS_E * MOE_ALIGN
    pos = route[:, 1].astype(jnp.int32).reshape(nb, 1, tb)
    meta_i = jnp.concatenate([meta[:, 0, :N_GROUPS_E], meta[:, 1, :N_GROUPS_E]], axis=1).astype(jnp.int32)
    grid_spec = pltpu.PrefetchScalarGridSpec(
        num_scalar_prefetch=1,
        grid=(nb, N_EXPERTS // MOE_EXPERTS_PER_STEP),
        in_specs=[
            pl.BlockSpec((None, 1, tb), lambda i, e, m: (i, 0, 0), memory_space=pltpu.SMEM),
            pl.BlockSpec((tb, d), lambda i, e, m: (i, 0)),
            pl.BlockSpec((tb, 128), lambda i, e, m: (i, 0)),
            pl.BlockSpec((1, d), lambda i, e, m: (0, 0)),
            pl.BlockSpec((1, d), lambda i, e, m: (0, 0)),
            pl.BlockSpec((MOE_EXPERTS_PER_STEP, d, D_EXPERT), lambda i, e, m: (e, 0, 0)),
            pl.BlockSpec((MOE_EXPERTS_PER_STEP, d, D_EXPERT), lambda i, e, m: (e, 0, 0)),
            pl.BlockSpec((MOE_EXPERTS_PER_STEP, D_EXPERT, d), lambda i, e, m: (e, 0, 0)),
        ],
        out_specs=pl.BlockSpec((tb, d), lambda i, e, m: (i, 0)),
        scratch_shapes=[pltpu.VMEM((rows, d), F32), pltpu.VMEM((rows, 128), F32), pltpu.VMEM((rows, d), F32)],
    )
    return pl.pallas_call(
        functools.partial(_moe_kernel, tb),
        grid_spec=grid_spec,
        out_shape=jax.ShapeDtypeStruct((t, d), F32),
        compiler_params=pltpu.CompilerParams(dimension_semantics=("parallel", "arbitrary"),
                                             vmem_limit_bytes=VMEM_LIMIT),
        name="moe",
    )(meta_i, pos, xmid, route, p["norm_ffn"], p["norm_final"], p["w_gate"], p["w_up"], p["w_down"])


def _pick_tile(t, pref):
    tm = min(pref, t)
    while t % tm:
        tm //= 2
    return tm


def _prep_layer(l, norm_mix, w_in, conv_a_w, a_log_a, dt_bias_a, norm_a, conv_b_w, conv_b_b, a_log_b, dt_bias_b,
                d_skip_b, norm_b, w_out, norm_ffn, w_router_group, b_router_group, w_router_expert,
                b_router_expert, w_gate_e, w_up_e, w_down_e, norm_final):
    def on_conv_columns(a_qkv, a_xbc):
        out = jnp.zeros(a_qkv.shape[:-1] + (D_PROJ,), F32)
        out = out.at[..., OFF_QKV:OFF_QKV + D_QKV_A].set(a_qkv.astype(F32))
        return out.at[..., OFF_XBC:OFF_XBC + D_XBC].set(a_xbc.astype(F32))

    def slab(a8, b16):
        return jnp.concatenate([jnp.zeros((H_A,), F32), a8.astype(F32), b16.astype(F32),
                                jnp.zeros((128 - 2 * H_A - H_B,), F32)]).reshape(1, 128)

    w_router = jnp.concatenate(
        [w_router_group[l].astype(F32), w_router_expert[l].reshape(D_MODEL, N_EXPERTS).astype(F32),
         jnp.zeros((D_MODEL, 128 - N_GROUPS_E - N_EXPERTS), F32)], axis=1)
    b_router = jnp.concatenate(
        [b_router_group[l].astype(F32), b_router_expert[l].reshape(N_EXPERTS).astype(F32),
         jnp.zeros((128 - N_GROUPS_E - N_EXPERTS,), F32)]).reshape(1, 128)
    return {
        "norm_mix": norm_mix[l].astype(F32).reshape(1, D_MODEL),
        "w_all": _prep_w_in(w_in[l]),
        "conv_w": on_conv_columns(conv_a_w[l], conv_b_w[l]),
        "conv_b": on_conv_columns(jnp.zeros((1, D_QKV_A), F32), conv_b_b[l].reshape(1, D_XBC)),
        "bias_slab": slab(dt_bias_a[l], dt_bias_b[l]),
        "alog_slab": slab(a_log_a[l], a_log_b[l]),
        "norm_a_x": jnp.tile(norm_a[l].astype(F32), H_A).reshape(1, D_V_A),
        "norm_b": norm_b[l].astype(F32).reshape(1, D_B),
        "dskip_x": jnp.repeat(d_skip_b[l].astype(F32), P_B).reshape(1, D_B),
        "w_out": _bf(w_out[l]),
        "norm_ffn": norm_ffn[l].astype(F32).reshape(1, D_MODEL),
        "w_router": w_router,
        "b_router": b_router,
        "w_gate": _bf(w_gate_e[l].reshape(N_EXPERTS, D_MODEL, D_EXPERT)),
        "w_up": _bf(w_up_e[l].reshape(N_EXPERTS, D_MODEL, D_EXPERT)),
        "w_down": _bf(w_down_e[l].reshape(N_EXPERTS, D_EXPERT, D_MODEL)),
        "norm_final": norm_final.astype(F32).reshape(1, D_MODEL),
    }


def _ffn_tail(mix2d, x2d, p):
    t = x2d.shape[0]
    tb = _pick_tile(t, MOE_BLOCK)
    xmid, route, meta = _outproj_router(mix2d, x2d, p, tb, MOE_ALIGN)
    return _moe(xmid, route, meta, p, tb)


def kernel(x_prompt, x_sample, state_delta, state_delta_conv, state_ssm, state_ssm_conv, norm_mix, w_in, conv_a_w,
           a_log_a, dt_bias_a, norm_a, conv_b_w, conv_b_b, a_log_b, dt_bias_b, d_skip_b, norm_b, w_out, norm_ffn,
           w_router_group, b_router_group, w_router_expert, b_router_expert, w_gate_e, w_up_e, w_down_e,
           norm_final):
    depth = w_in.shape[0]
    assert depth == 1, "the fused final norm assumes a single layer"
    bp, lp, d = x_prompt.shape
    bs, ls, _ = x_sample.shape
    assert lp % PROMPT_CHUNK == 0 and lp >= CONV_W - 1
    assert PAD_FRONT + ls <= SEQ_ROWS and bs % (CHUNK // SEQ_ROWS) == 0 and (CHUNK // SEQ_ROWS * ls) % 8 == 0
    l = 0
    p = _prep_layer(l, norm_mix, w_in, conv_a_w, a_log_a, dt_bias_a, norm_a, conv_b_w, conv_b_b, a_log_b,
                    dt_bias_b, d_skip_b, norm_b, w_out, norm_ffn, w_router_group, b_router_group,
                    w_router_expert, b_router_expert, w_gate_e, w_up_e, w_down_e, norm_final)

    xp2 = x_prompt.reshape(bp * lp, d)
    tm_p = _pick_tile(lp, 256)
    proj_p, tails = _inproj_conv(xp2, p, tm_p, lp)
    mix_p, delta_p, ssm_p = _mixer_prompt(proj_p.reshape(bp, lp, D_PROJ), p)
    tails = tails.reshape(bp, lp // tm_p, 8, D_CONV)[:, -1, 8 - (CONV_W - 1):]
    dconv_p = tails[:, :, :D_QKV_A]
    sconv_p = tails[:, :, D_QKV_A:]
    y_p = _ffn_tail(mix_p.reshape(bp * lp, -1), xp2, p).reshape(bp, lp, d)

    xs2 = x_sample.reshape(bs * ls, d)
    proj_s = _inproj(xs2, p, _pick_tile(bs * ls, 256))
    dconv_in = state_delta_conv[l].astype(F32)
    sconv_in = state_ssm_conv[l].astype(F32)
    mix_s, delta_s, ssm_s = _mixer_decode(proj_s, dconv_in, sconv_in, state_delta[l].astype(F32),
                                          state_ssm[l].astype(F32).reshape(bs, D_B, N_B), p)
    mix_s = mix_s.reshape(bs, SEQ_ROWS, -1)[:, PAD_FRONT:PAD_FRONT + ls].reshape(bs * ls, -1)
    dconv_s = jnp.concatenate([dconv_in, proj_s[:, OFF_QKV:OFF_QKV + D_QKV_A].reshape(bs, ls, D_QKV_A)], axis=1)[:, ls:]
    sconv_s = jnp.concatenate([sconv_in, proj_s[:, OFF_XBC:OFF_XBC + D_XBC].reshape(bs, ls, D_XBC)], axis=1)[:, ls:]
    y_s = _ffn_tail(mix_s, xs2, p).reshape(bs, ls, d)

    return (y_p.astype(x_prompt.dtype), y_s.astype(x_sample.dtype),
            delta_p[None], dconv_p[None], ssm_p.reshape(bp, H_B, P_B, N_B)[None], sconv_p[None],
            delta_s[None], dconv_s[None], ssm_s.reshape(bs, H_B, P_B, N_B)[None], sconv_s[None])
```

```python
import functools

import numpy as np
import jax
import jax.numpy as jnp
from jax import lax
from jax.experimental import pallas as pl
from jax.experimental.pallas import tpu as pltpu

F32 = jnp.float32
BF16 = jnp.bfloat16

D_MODEL = 1024
H_A, DK_A, DV_A = 8, 128, 128
D_QK_A = H_A * DK_A
D_V_A = H_A * DV_A
D_QKV_A = 2 * D_QK_A + D_V_A
H_B, P_B, N_B, G_B = 16, 64, 128, 2
D_B = H_B * P_B
D_XBC = D_B + 2 * G_B * N_B
CONV_W = 4
N_GROUPS_E, EXPERTS_PER_GROUP = 4, 4
N_EXPERTS = N_GROUPS_E * EXPERTS_PER_GROUP
D_EXPERT = 512
EPS = 1e-6

OFF_QKV = 0
OFF_GATE = OFF_QKV + D_QKV_A
OFF_Z = OFF_GATE + D_V_A
OFF_XBC = OFF_Z + D_B
OFF_SMALL = OFF_XBC + D_XBC
D_PROJ = OFF_SMALL + 128
D_CONV = D_QKV_A + D_XBC
CONV_RANGES = ((OFF_QKV, D_QKV_A), (OFF_XBC, D_XBC))
LANE_G = H_A
LANE_DT = 2 * H_A

CHUNK = 64
PROMPT_CHUNK = 128
SEQ_ROWS = 8
PROMPT_STREAMS = 2
PAD_FRONT = CONV_W - 1
INPROJ_SPREAD = 1
INPROJ_COLS = 256
MOE_BLOCK = 1024
MOE_SUB = 256
MOE_EXPERTS_PER_STEP = 2
MOE_ALIGN = 64
VMEM_LIMIT = 52 * 1024 * 1024


def _bf(x):
    return x.astype(BF16)


def _mm(a, b):
    return jnp.dot(_bf(a), _bf(b), preferred_element_type=F32)


def _mm_nt(a, b):
    return lax.dot_general(_bf(a), _bf(b), (((1,), (1,)), ((), ())), preferred_element_type=F32)


def _mm_tn(a, b):
    return lax.dot_general(_bf(a), _bf(b), (((0,), (0,)), ((), ())), preferred_element_type=F32)


def _split3(x):
    x1 = _bf(x)
    r1 = x - x1.astype(F32)
    x2 = _bf(r1)
    x3 = _bf(r1 - x2.astype(F32))
    return x1, x2, x3


def _mm01_r(x, m01):
    x1, x2, x3 = _split3(x)
    d = functools.partial(jnp.dot, preferred_element_type=F32)
    return d(x1, m01) + d(x2, m01) + d(x3, m01)


def _mm01_r2(x, m01):
    x1 = _bf(x)
    x2 = _bf(x - x1.astype(F32))
    d = functools.partial(jnp.dot, preferred_element_type=F32)
    return d(x1, m01) + d(x2, m01)


def _mm01_l(m01, x):
    x1, x2, x3 = _split3(x)
    d = functools.partial(jnp.dot, preferred_element_type=F32)
    return d(m01, x1) + d(m01, x2) + d(m01, x3)


def _rms(x, w):
    return x * lax.rsqrt(jnp.mean(x * x, axis=-1, keepdims=True) + EPS) * w


def _silu(x):
    return x * jax.nn.sigmoid(x)


def _softplus(x):
    return jnp.maximum(x, 0.0) + jnp.log1p(jnp.exp(-jnp.abs(x)))


def _weight_blocks(width):
    return [slice(c0, min(c0 + width, D_PROJ)) for c0 in range(0, D_PROJ, width)]


def _is_conv_block(dst):
    return any(lo <= dst.start and dst.stop <= lo + n for lo, n in CONV_RANGES)


def _prep_w_in_kernel(w_ref, o_ref):
    c_beta = D_QKV_A + D_V_A
    c_z = c_beta + 2 * H_A
    c_dt = c_z + D_B + D_XBC
    o_ref[:, OFF_QKV:OFF_Z] = _bf(w_ref[:, 0:c_beta])
    o_ref[:, OFF_Z:OFF_SMALL] = _bf(w_ref[:, c_z:c_dt])
    small = jnp.concatenate([w_ref[:, c_beta:c_z], w_ref[:, c_dt:c_dt + H_B],
                             jnp.zeros((w_ref.shape[0], 128 - 2 * H_A - H_B), F32)], axis=1)
    o_ref[:, OFF_SMALL:D_PROJ] = _bf(small)


def _prep_w_in(w, l):
    _, d, n = w.shape
    rows = 128
    return pl.pallas_call(
        _prep_w_in_kernel,
        grid=(d // rows,),
        in_specs=[pl.BlockSpec((None, rows, n), lambda i: (l, i, 0))],
        out_specs=pl.BlockSpec((rows, D_PROJ), lambda i: (i, 0)),
        out_shape=jax.ShapeDtypeStruct((d, D_PROJ), BF16),
        compiler_params=pltpu.CompilerParams(dimension_semantics=("parallel",)),
        name="prep_w_in",
    )(w.astype(F32))


def _inproj_kernel(x_ref, nw_ref, w_ref, o_ref):
    h = _bf(_rms(x_ref[...], nw_ref[...]))
    o_ref[...] = jnp.dot(h, w_ref[...], preferred_element_type=F32)


def _weight_specs(p):
    return [pl.BlockSpec(p["w_all"].shape, lambda i: (0, 0), pipeline_mode=pl.Buffered(1))]


def _inproj(x2d, p, tm):
    t, d = x2d.shape
    return pl.pallas_call(
        _inproj_kernel,
        grid=(t // tm,),
        in_specs=[pl.BlockSpec((tm, d), lambda i: (i, 0)), pl.BlockSpec((1, d), lambda i: (0, 0))] + _weight_specs(p),
        out_specs=pl.BlockSpec((tm, D_PROJ), lambda i: (i, 0)),
        out_shape=jax.ShapeDtypeStruct((t, D_PROJ), F32),
        compiler_params=pltpu.CompilerParams(dimension_semantics=("parallel",), vmem_limit_bytes=VMEM_LIMIT),
        name="inproj",
    )(x2d, p["norm_mix"], p["w_all"])


def _conv_silu(u, prev8, cw_ref, cb_ref, sl):
    row8 = lax.broadcasted_iota(jnp.int32, (8, 1), 0)
    acc = u * cw_ref[CONV_W - 1:CONV_W, sl] + cb_ref[:, sl]
    for k in range(1, CONV_W):
        ru = pltpu.roll(u, k, 0)
        if prev8 is not None:
            head = jnp.where(row8 < k, pltpu.roll(prev8, k, 0), ru[0:8])
            ru = jnp.concatenate([head, ru[8:]], axis=0)
        acc = acc + ru * cw_ref[CONV_W - 1 - k:CONV_W - k, sl]
    return _silu(acc)


def _inproj_conv_kernel(tiles_per_seq, x_ref, nw_ref, w_ref, cw_ref, cb_ref, o_ref, tail_ref, prev_ref):
    @pl.when(pl.program_id(0) % tiles_per_seq == 0)
    def _():
        prev_ref[...] = jnp.zeros_like(prev_ref)

    h = _bf(_rms(x_ref[...], nw_ref[...]))
    tm = h.shape[0]
    blocks = _weight_blocks(INPROJ_COLS)
    conv_blocks = [b for b in blocks if _is_conv_block(b)]
    plain_blocks = [b for b in blocks if not _is_conv_block(b)]
    blocks = []
    for j, b in enumerate(conv_blocks):
        blocks.append(b)
        if j % INPROJ_SPREAD == INPROJ_SPREAD - 1 and plain_blocks:
            blocks.append(plain_blocks.pop(0))
    blocks += plain_blocks
    for dst in blocks:
        u = jnp.dot(h, w_ref[:, dst], preferred_element_type=F32)
        if _is_conv_block(dst):
            o_ref[:, dst] = _conv_silu(u, prev_ref[:, dst], cw_ref, cb_ref, dst)
            prev_ref[:, dst] = u[tm - 8:tm]
            t0 = dst.start - OFF_QKV if dst.start < OFF_XBC else dst.start - OFF_XBC + D_QKV_A
            tail_ref[:, t0:t0 + (dst.stop - dst.start)] = u[tm - 8:tm]
        else:
            o_ref[:, dst] = u


def _inproj_conv(x2d, p, tm, seq_len):
    t, d = x2d.shape
    assert seq_len % tm == 0
    return pl.pallas_call(
        functools.partial(_inproj_conv_kernel, seq_len // tm),
        grid=(t // tm,),
        in_specs=[pl.BlockSpec((tm, d), lambda i: (i, 0)), pl.BlockSpec((1, d), lambda i: (0, 0))] + _weight_specs(p)
        + [_const_spec(p["conv_w"].shape), _const_spec(p["conv_b"].shape)],
        out_specs=[pl.BlockSpec((tm, D_PROJ), lambda i: (i, 0)), pl.BlockSpec((None, 8, D_CONV), lambda i: (i, 0, 0))],
        out_shape=[jax.ShapeDtypeStruct((t, D_PROJ), F32), jax.ShapeDtypeStruct((t // tm, 8, D_CONV), F32)],
        scratch_shapes=[pltpu.VMEM((8, D_PROJ), F32)],
        compiler_params=pltpu.CompilerParams(dimension_semantics=("arbitrary",), vmem_limit_bytes=VMEM_LIMIT),
        name="inproj_conv",
    )(x2d, p["norm_mix"], p["w_all"], p["conv_w"], p["conv_b"])


def _inv_unit_lower(mats, ri, ci, eye, merge_sizes):
    blk = (ri >> 3) == (ci >> 3)
    ads = [jnp.where(blk, a, 0.0) for a in mats]
    xs = [eye - ad for ad in ads]
    n = eye.shape[0]
    a2s = [_mm(ad, ad) for ad in ads]
    st = [_mm(jnp.concatenate([x, a2], axis=0), a2) for x, a2 in zip(xs, a2s)]
    xs = [x + t[:n] for x, t in zip(xs, st)]
    xs = [x + _mm(x, t[n:]) for x, t in zip(xs, st)]
    for s in merge_sizes:
        sh = s.bit_length() - 1
        rb = ri >> sh
        sel = ((rb & 1) == 1) & ((ci >> sh) == rb - 1)
        odd = [slice(r, r + s) for r in range(s, n, 2 * s)]
        even = [slice(r, r + s) for r in range(0, n, 2 * s)]
        xo = [jnp.concatenate([x[sl] for sl in odd], axis=0) for x in xs]
        ts = [_mm(o, jnp.where(sel, a, 0.0)) for o, a in zip(xo, mats)]
        xo = [o - _mm(t, x) for o, t, x in zip(xo, ts, xs)]
        xs = [jnp.concatenate([blk for i, ev in enumerate(even) for blk in (x[ev], o[i * s:(i + 1) * s])], axis=0)
              for x, o in zip(xs, xo)]
    return xs


def _mixer_kernel(c, n_streams, n_seq, valid_lo, valid_hi, *refs):
    carry = n_seq == 1
    (proj_ref, cw_ref, cb_ref, bias_ref, alog_ref, na_ref, nb_ref, dsk_ref,
     ea_ref, eb_ref, ec_ref) = refs[:11]
    if carry:
        mix_ref, s_wr, h_wr = refs[11:]
        s_rd, h_rd = s_wr, h_wr
        xc_ref = proj_ref

        @pl.when(pl.program_id(1) == 0)
        def _():
            s_wr[...] = jnp.zeros_like(s_wr)
            h_wr[...] = jnp.zeros_like(h_wr)
    else:
        dconv_ref, sconv_ref, s_rd, h_rd, mix_ref, s_wr, h_wr, xc_ref, tile_ref = refs[11:]
        tok_ref = proj_ref
        l_tok = tok_ref.shape[0] // n_seq
        tile_ref[...] = jnp.zeros_like(tile_ref)
        for s in range(n_seq):
            r0 = (c // n_seq) * s
            tile_ref[0, r0:r0 + CONV_W - 1, OFF_QKV:OFF_QKV + D_QKV_A] = dconv_ref[s]
            tile_ref[0, r0:r0 + CONV_W - 1, OFF_XBC:OFF_XBC + D_XBC] = sconv_ref[s]
            tile_ref[0, r0 + valid_lo:r0 + valid_lo + l_tok, :] = tok_ref[l_tok * s:l_tok * (s + 1), :]
        proj_ref = tile_ref

    r_seq = c // n_seq
    sh_seq = r_seq.bit_length() - 1
    row = lax.broadcasted_iota(jnp.int32, (c, 1), 0)
    lane = lax.broadcasted_iota(jnp.int32, (1, 128), 1)
    ri = lax.broadcasted_iota(jnp.int32, (c, c), 0)
    ci = lax.broadcasted_iota(jnp.int32, (c, c), 1)
    eye = (ri == ci).astype(F32)
    seq_of_row = row >> sh_seq
    if carry:
        same = None
        causal = ri >= ci
        strict = ri > ci
        valid = None
        merge_sizes = tuple(8 << i for i in range((c // 8).bit_length() - 1))
    else:
        same = (ri >> sh_seq) == (ci >> sh_seq)
        causal = (ri >= ci) & same
        strict = (ri > ci) & same
        rr = row & (r_seq - 1)
        valid = ((rr >= valid_lo) & (rr < valid_hi)).astype(F32)
        merge_sizes = ()
    causal_bf = _bf(causal.astype(F32))
    is_beta = lane < LANE_G
    is_g = (lane >= LANE_G) & (lane < LANE_DT)
    is_dt = (lane >= LANE_DT) & (lane < LANE_DT + H_B)
    streams = range(n_streams)

    def conv(st):
        for lo, n in CONV_RANGES:
            for c0 in range(lo, lo + n, 512):
                sl = slice(c0, c0 + 512)
                xc_ref[st, :, sl] = _conv_silu(proj_ref[st, :, sl], None, cw_ref, cb_ref, sl)

    def gates(st):
        raw = proj_ref[st, :, OFF_SMALL:OFF_SMALL + 128]
        sp = _softplus(raw + bias_ref[...])
        coef = -jnp.exp(alog_ref[...])
        gd = jnp.where(is_g | is_dt, sp * coef, 0.0)
        q1 = jnp.where(is_beta, jax.nn.sigmoid(raw), jnp.where(is_dt, sp, 0.0))
        if valid is not None:
            gd = gd * valid
            q1 = q1 * valid
        gc = _mm01_l(causal_bf, gd)
        if carry:
            tot = jnp.broadcast_to(gc[c - 1:c, :], (c, 128))
        else:
            tot = _mm01_l(_bf(same.astype(F32)), gd)
        eg = jnp.exp(gc)
        et = jnp.exp(tot - gc)
        xa = _mm01_r2(jnp.concatenate([q1, jnp.where(is_g, eg, 0.0), jnp.where(is_g, et, 0.0)], axis=0),
                      ea_ref[...])
        xb = _mm01_r2(jnp.concatenate([q1, eg, et], axis=0), eb_ref[...])
        etot = jnp.exp(tot[0:8] if carry else tot)
        return {
            "gc": gc, "gct": gc.T,
            "beta_x": xa[0:c], "egc_x": xa[c:2 * c], "tail_x": xa[2 * c:3 * c],
            "dt_x": xb[0:c], "edac_x": xb[c:2 * c], "tailb_x": xb[2 * c:3 * c],
            "dec_x": _mm01_r(etot, ec_ref[...]),
        }

    if not carry:
        for st in streams:
            conv(st)
    gts = [gates(st) for st in streams]

    pairs = [(st, h) for st in streams for h in range(H_A)]
    hsl = [slice(128 * h, 128 * (h + 1)) for h in range(H_A)]
    qs, ks, vs = [], [], []
    for st, h in pairs:
        q = xc_ref[st, :, OFF_QKV + 128 * h:OFF_QKV + 128 * (h + 1)]
        k = xc_ref[st, :, OFF_QKV + D_QK_A + 128 * h:OFF_QKV + D_QK_A + 128 * (h + 1)]
        qs.append(q * lax.rsqrt(jnp.sum(q * q, axis=-1, keepdims=True) + EPS) * (DK_A ** -0.5))
        ks.append(k * lax.rsqrt(jnp.sum(k * k, axis=-1, keepdims=True) + EPS))
        vs.append(xc_ref[st, :, OFF_QKV + 2 * D_QK_A + 128 * h:OFF_QKV + 2 * D_QK_A + 128 * (h + 1)])
    n_p = len(pairs)
    betas = [gts[st]["beta_x"][:, hsl[h]] for st, h in pairs]
    egcs = [gts[st]["egc_x"][:, hsl[h]] for st, h in pairs]
    kbs = [ks[i] * betas[i] for i in range(n_p)]
    nts = [_mm_nt(jnp.concatenate([kbs[i], qs[i]], axis=0), ks[i]) for i in range(n_p)]
    decs = [jnp.exp(jnp.where(causal, gts[st]["gc"][:, LANE_G + h:LANE_G + h + 1]
                              - gts[st]["gct"][LANE_G + h:LANE_G + h + 1, :], -jnp.inf)) for st, h in pairs]
    a_s = [jnp.where(strict, nts[i][:c] * decs[i], 0.0) for i in range(n_p)]
    qks = [nts[i][c:] * decs[i] for i in range(n_p)]
    x_invs = _inv_unit_lower(a_s, ri, ci, eye, merge_sizes)
    sols = [_mm(x_invs[i], jnp.concatenate([vs[i] * betas[i], kbs[i] * egcs[i]], axis=1)) for i in range(n_p)]
    qes = [qs[i] * egcs[i] for i in range(n_p)]
    if n_seq == 1:
        vnews = [sols[i][:, :128] - _mm(sols[i][:, 128:], s_rd[st, h]) for i, (st, h) in enumerate(pairs)]
        os_ = [_mm(jnp.concatenate([qes[i], qks[i]], axis=1),
                   jnp.concatenate([s_rd[st, h], vnews[i]], axis=0)) for i, (st, h) in enumerate(pairs)]
    else:
        wss = [[_mm(jnp.concatenate([sols[i][r_seq * s:r_seq * (s + 1), 128:], qes[i][r_seq * s:r_seq * (s + 1)]],
                                    axis=0), s_rd[st * n_seq + s, h]) for s in range(n_seq)]
               for i, (st, h) in enumerate(pairs)]
        vnews = [jnp.concatenate([sols[i][r_seq * s:r_seq * (s + 1), :128] - wss[i][s][:r_seq]
                                  for s in range(n_seq)], axis=0) for i in range(n_p)]
        os_ = [jnp.concatenate([wss[i][s][r_seq:] for s in range(n_seq)], axis=0) + _mm(qks[i], vnews[i])
               for i in range(n_p)]
    for i, (st, h) in enumerate(pairs):
        kt = ks[i] * gts[st]["tail_x"][:, hsl[h]]
        for s in range(n_seq):
            kts = kt if n_seq == 1 else jnp.where(seq_of_row == s, kt, 0.0)
            sq = st * n_seq + s
            s_wr[sq, h] = s_rd[sq, h] * gts[st]["dec_x"][r_seq * s:r_seq * s + 1, hsl[h]] + _mm_tn(kts, vnews[i])
    for i, (st, h) in enumerate(pairs):
        gate = proj_ref[st, :, OFF_GATE + 128 * h:OFF_GATE + 128 * (h + 1)]
        mix_ref[st, :, hsl[h]] = _bf(_rms(os_[i], na_ref[:, hsl[h]]) * _silu(gate))

    hg = H_B // G_B
    wg = hg * P_B
    off_x = OFF_XBC
    off_b = off_x + D_B
    off_c = off_b + G_B * N_B
    for st, g in [(st, g) for st in streams for g in range(G_B)]:
        gt = gts[st]
        gc, gct = gt["gc"], gt["gct"]
        gs = slice(wg * g, wg * (g + 1))
        bg = xc_ref[st, :, off_b + N_B * g:off_b + N_B * (g + 1)]
        cg = xc_ref[st, :, off_c + N_B * g:off_c + N_B * (g + 1)]
        xs_g = xc_ref[st, :, off_x + wg * g:off_x + wg * (g + 1)]
        xdt = xs_g * gt["dt_x"][:, gs]
        cb = _mm_nt(cg, bg)
        ypairs = []
        for p in range(hg // 2):
            xp = xdt[:, 128 * p:128 * (p + 1)]
            ms = []
            for a_ in (0, 1):
                ln = LANE_DT + hg * g + 2 * p + a_
                seg = jnp.exp(jnp.where(causal, gc[:, ln:ln + 1] - gct[ln:ln + 1, :], -jnp.inf))
                ms.append(cb * seg)
            xpp = jnp.concatenate([jnp.where(lane < P_B, xp, 0.0), jnp.where(lane >= P_B, xp, 0.0)], axis=0)
            ypairs.append(_mm(jnp.concatenate(ms, axis=1), xpp))
        y = jnp.concatenate(ypairs, axis=1)
        yoffs = []
        for s in range(n_seq):
            rows = slice(r_seq * s, r_seq * (s + 1))
            yoffs.append(_mm_nt(cg[rows], h_rd[st * n_seq + s, wg * g:wg * (g + 1), :]))
        yoff = yoffs[0] if n_seq == 1 else jnp.concatenate(yoffs, axis=0)
        y = y + yoff * gt["edac_x"][:, gs] + dsk_ref[:, gs] * xs_g
        xt = xdt * gt["tailb_x"][:, gs]
        for s in range(n_seq):
            xts = xt if n_seq == 1 else jnp.where(seq_of_row == s, xt, 0.0)
            stt = _mm_tn(xts, bg)
            sq = st * n_seq + s
            for j in range(hg):
                hh = hg * g + j
                rs = slice(P_B * hh, P_B * (hh + 1))
                drow = gt["dec_x"][r_seq * s:r_seq * s + 1, 128 * (H_A + hh):128 * (H_A + hh + 1)]
                h_wr[sq, rs, :] = h_rd[sq, rs, :] * drow + stt[P_B * j:P_B * (j + 1), :]
        z = proj_ref[st, :, OFF_Z + wg * g:OFF_Z + wg * (g + 1)]
        mix_ref[st, :, D_V_A + wg * g:D_V_A + wg * (g + 1)] = _bf(_rms(y * _silu(z), nb_ref[:, gs]))


def _expand_mats():
    ea = np.zeros((128, 128 * H_A), np.float32)
    for h in range(H_A):
        ea[h, 128 * h:128 * (h + 1)] = 1.0
        ea[LANE_G + h, 128 * h:128 * (h + 1)] = 1.0
    eb = np.zeros((128, D_B), np.float32)
    for h in range(H_B):
        eb[LANE_DT + h, P_B * h:P_B * (h + 1)] = 1.0
    ec = np.zeros((128, 128 * (H_A + H_B)), np.float32)
    for h in range(H_A):
        ec[LANE_G + h, 128 * h:128 * (h + 1)] = 1.0
    for h in range(H_B):
        ec[LANE_DT + h, 128 * (H_A + h):128 * (H_A + h + 1)] = 1.0
    return jnp.asarray(ea, BF16), jnp.asarray(eb, BF16), jnp.asarray(ec, BF16)


def _const_spec(shape):
    return pl.BlockSpec(shape, lambda *_: (0,) * len(shape))


def _mixer_params(p):
    ea, eb, ec = _expand_mats()
    return (p["conv_w"], p["conv_b"], p["bias_slab"], p["alog_slab"], p["norm_a_x"], p["norm_b"], p["dskip_x"],
            ea, eb, ec)


def _mixer_prompt(proj, p):
    b, l, n = proj.shape
    nb = PROMPT_STREAMS if b % PROMPT_STREAMS == 0 else 1
    consts = _mixer_params(p)
    const_specs = [_const_spec(a.shape) for a in consts]
    return pl.pallas_call(
        functools.partial(_mixer_kernel, PROMPT_CHUNK, nb, 1, 0, PROMPT_CHUNK),
        grid=(b // nb, l // PROMPT_CHUNK),
        in_specs=[pl.BlockSpec((nb, PROMPT_CHUNK, n), lambda i, t: (i, t, 0))] + const_specs,
        out_specs=[
            pl.BlockSpec((nb, PROMPT_CHUNK, D_V_A + D_B), lambda i, t: (i, t, 0)),
            pl.BlockSpec((nb, H_A, DK_A, DV_A), lambda i, t: (i, 0, 0, 0)),
            pl.BlockSpec((nb, D_B, N_B), lambda i, t: (i, 0, 0)),
        ],
        out_shape=[
            jax.ShapeDtypeStruct((b, l, D_V_A + D_B), BF16),
            jax.ShapeDtypeStruct((b, H_A, DK_A, DV_A), F32),
            jax.ShapeDtypeStruct((b, D_B, N_B), F32),
        ],
        compiler_params=pltpu.CompilerParams(dimension_semantics=("parallel", "arbitrary"),
                                             vmem_limit_bytes=VMEM_LIMIT),
        name="mixer_prompt",
    )(proj, *consts)


def _mixer_decode(proj, dconv, sconv, s_delta, s_ssm, p):
    bs = s_delta.shape[0]
    n = proj.shape[1]
    l_dec = proj.shape[0] // bs
    n_seq = CHUNK // SEQ_ROWS
    tiles = bs // n_seq
    consts = _mixer_params(p)
    const_specs = [_const_spec(a.shape) for a in consts]
    return pl.pallas_call(
        functools.partial(_mixer_kernel, CHUNK, 1, n_seq, PAD_FRONT, PAD_FRONT + l_dec),
        grid=(tiles,),
        in_specs=[pl.BlockSpec((n_seq * l_dec, n), lambda i: (i, 0))] + const_specs + [
            pl.BlockSpec((n_seq, CONV_W - 1, D_QKV_A), lambda i: (i, 0, 0)),
            pl.BlockSpec((n_seq, CONV_W - 1, D_XBC), lambda i: (i, 0, 0)),
            pl.BlockSpec((n_seq, H_A, DK_A, DV_A), lambda i: (i, 0, 0, 0)),
            pl.BlockSpec((n_seq, D_B, N_B), lambda i: (i, 0, 0)),
        ],
        out_specs=[
            pl.BlockSpec((1, CHUNK, D_V_A + D_B), lambda i: (i, 0, 0)),
            pl.BlockSpec((n_seq, H_A, DK_A, DV_A), lambda i: (i, 0, 0, 0)),
            pl.BlockSpec((n_seq, D_B, N_B), lambda i: (i, 0, 0)),
        ],
        out_shape=[
            jax.ShapeDtypeStruct((tiles, CHUNK, D_V_A + D_B), BF16),
            jax.ShapeDtypeStruct((bs, H_A, DK_A, DV_A), F32),
            jax.ShapeDtypeStruct((bs, D_B, N_B), F32),
        ],
        scratch_shapes=[pltpu.VMEM((1, CHUNK, n), F32), pltpu.VMEM((1, CHUNK, n), F32)],
        compiler_params=pltpu.CompilerParams(dimension_semantics=("parallel",), vmem_limit_bytes=VMEM_LIMIT),
        name="mixer_decode",
    )(proj, *consts, dconv, sconv, s_delta, s_ssm)


def _outproj_router_kernel(sub, mix_ref, x_ref, wo_ref, nf_ref, wr_ref, br_ref, tril_ref, xmid_ref, route_ref,
                           meta_ref):
    xm = x_ref[...] + jnp.dot(mix_ref[...], wo_ref[...], preferred_element_type=F32)
    xmid_ref[...] = xm
    t = _rms(xm, nf_ref[...])
    t1 = _bf(t)
    t2 = _bf(t - t1.astype(F32))
    d = functools.partial(jnp.dot, preferred_element_type=F32)
    hl = d(t1, wr_ref[...])
    logit = hl[:, :128] + hl[:, 128:] + d(t2, wr_ref[:, :128]) + br_ref[...]

    lane = lax.broadcasted_iota(jnp.int32, (1, 128), 1).astype(F32)
    neg = -jnp.inf
    big = 1e9
    is_grp = lane < N_GROUPS_E
    gl = jnp.where(is_grp, logit, neg)
    gmax = jnp.max(gl, axis=-1, keepdims=True)
    gsel = jnp.min(jnp.where(gl == gmax, lane, big), axis=-1, keepdims=True)
    gw = 1.0 / jnp.sum(jnp.exp(jnp.where(is_grp, logit - gmax, neg)), axis=-1, keepdims=True)
    lo = N_GROUPS_E + EXPERTS_PER_GROUP * gsel
    el = jnp.where((lane >= lo) & (lane < lo + EXPERTS_PER_GROUP), logit, neg)
    v1 = jnp.max(el, axis=-1, keepdims=True)
    i1 = jnp.min(jnp.where(el == v1, lane, big), axis=-1, keepdims=True)
    el2 = jnp.where(lane == i1, neg, el)
    v2 = jnp.max(el2, axis=-1, keepdims=True)
    i2 = jnp.min(jnp.where(el2 == v2, lane, big), axis=-1, keepdims=True)
    e = jnp.exp(v2 - v1)
    den = gw / (1.0 + e)
    comb = jnp.where(lane == i1, den, 0.0) + jnp.where(lane == i2, e * den, 0.0)

    tm = logit.shape[0]
    onehot = jnp.where((lane == gsel) & is_grp, 1.0, 0.0)
    ranks = jnp.dot(tril_ref[...], _bf(onehot), preferred_element_type=F32)
    cnt = ranks[tm - 1:tm, :] + onehot[tm - 1:tm, :]
    ntile = jnp.floor((cnt + (sub - 1)) * (1.0 / sub))
    li = lax.broadcasted_iota(jnp.int32, (128, 128), 0)
    lj = lax.broadcasted_iota(jnp.int32, (128, 128), 1)
    off = _mm01_r(jnp.broadcast_to(ntile * sub, (8, 128)), _bf((li < lj).astype(F32)))[0:1]
    pos = jnp.sum(onehot * (off + ranks), axis=-1, keepdims=True)
    route_ref[...] = jnp.where(lane == 0.0, gsel, jnp.where(lane == 1.0, pos, comb))
    row8 = lax.broadcasted_iota(jnp.int32, (8, 128), 0)
    meta_ref[...] = jnp.where(row8 == 0, off, ntile)


def _outproj_router(mix, x2d, p, tm, sub):
    t, d = x2d.shape
    nb = t // tm
    tril = jnp.asarray(np.tri(tm, tm, -1, dtype=np.float32), BF16)
    return pl.pallas_call(
        functools.partial(_outproj_router_kernel, sub),
        grid=(nb,),
        in_specs=[
            pl.BlockSpec((tm, mix.shape[1]), lambda i: (i, 0)),
            pl.BlockSpec((tm, d), lambda i: (i, 0)),
            _const_spec(p["w_out"].shape),
            _const_spec((1, d)),
            _const_spec(p["w_router"].shape),
            _const_spec((1, 128)),
            _const_spec((tm, tm)),
        ],
        out_specs=[pl.BlockSpec((tm, d), lambda i: (i, 0)), pl.BlockSpec((tm, 128), lambda i: (i, 0)),
                   pl.BlockSpec((None, 8, 128), lambda i: (i, 0, 0))],
        out_shape=[jax.ShapeDtypeStruct((t, d), F32), jax.ShapeDtypeStruct((t, 128), F32),
                   jax.ShapeDtypeStruct((nb, 8, 128), F32)],
        compiler_params=pltpu.CompilerParams(dimension_semantics=("parallel",), vmem_limit_bytes=VMEM_LIMIT),
        name="outproj_router",
    )(mix, x2d, p["w_out"], p["norm_ffn"], p["w_router"], p["b_router"], tril)


def _moe_kernel(tb, meta_ref, pos_ref, xmid_ref, route_ref, nffn_ref, nfin_ref, wg_ref, wu_ref, wd_ref,
                y_ref, xg_scr, rg_scr, yg_scr):
    b = pl.program_id(0)
    step = pl.program_id(1)
    n_steps = N_EXPERTS // MOE_EXPERTS_PER_STEP
    steps_per_group = EXPERTS_PER_GROUP // MOE_EXPERTS_PER_STEP
    g = step // steps_per_group
    unroll = 8

    @pl.when(step == 0)
    def _():
        y_ref[...] = _rms(xmid_ref[...], nffn_ref[...])
        for gg in range(N_GROUPS_E):
            nq_g = meta_ref[b, N_GROUPS_E + gg]

            @pl.when(nq_g > 0)
            def _(gg=gg, nq_g=nq_g):
                r = pl.multiple_of(meta_ref[b, gg] + (nq_g - 1) * MOE_ALIGN, MOE_ALIGN)
                xg_scr[pl.ds(r, MOE_ALIGN), :] = jnp.zeros((MOE_ALIGN, D_MODEL), F32)
                rg_scr[pl.ds(r, MOE_ALIGN), :] = jnp.zeros((MOE_ALIGN, 128), F32)

        def dispatch(i, carry):
            base = pl.multiple_of(i * unroll, unroll)
            src = y_ref.at[pl.ds(base, unroll), :]
            rsrc = route_ref.at[pl.ds(base, unroll), :]
            for u in range(unroll):
                p = pos_ref[0, base + u]
                xg_scr[pl.ds(p, 1), :] = src[u:u + 1, :]
                rg_scr[pl.ds(p, 1), :] = rsrc[u:u + 1, :]
            return carry

        lax.fori_loop(0, tb // unroll, dispatch, 0)

    off = meta_ref[b, g]
    nq = meta_ref[b, N_GROUPS_E + g]
    lane = lax.broadcasted_iota(jnp.int32, (1, 128), 1)
    first = (step % steps_per_group) == 0
    d = functools.partial(jnp.dot, preferred_element_type=F32)

    def tile(r0, m):
        r0 = pl.multiple_of(r0, MOE_ALIGN)
        x = _bf(xg_scr[pl.ds(r0, m), :])
        rg = rg_scr[pl.ds(r0, m), :]
        contrib = None
        for j in range(MOE_EXPERTS_PER_STEP):
            e16 = step * MOE_EXPERTS_PER_STEP + j
            cw = jnp.sum(jnp.where(lane == N_GROUPS_E + e16, rg, 0.0), axis=-1, keepdims=True)
            hid = _silu(d(x, wg_ref[j])) * d(x, wu_ref[j]) * cw
            cj = d(_bf(hid), wd_ref[j])
            contrib = cj if contrib is None else contrib + cj

        @pl.when(first)
        def _():
            yg_scr[pl.ds(r0, m), :] = contrib

        @pl.when(jnp.logical_not(first))
        def _():
            yg_scr[pl.ds(r0, m), :] += contrib

    per_tile = MOE_SUB // MOE_ALIGN
    nloop = jnp.maximum(nq // per_tile - 1, 0)

    def full_tile(j, carry):
        tile(off + j * MOE_SUB, MOE_SUB)
        return carry

    lax.fori_loop(0, nloop, full_tile, 0)
    last = nq - nloop * per_tile
    r_last = off + nloop * MOE_SUB
    for q in range(1, 2 * per_tile):
        @pl.when(last == q)
        def _(q=q):
            tile(r_last, q * MOE_ALIGN)

    @pl.when(step == n_steps - 1)
    def _():
        def combine(i, carry):
            base = pl.multiple_of(i * unroll, unroll)
            dst = y_ref.at[pl.ds(base, unroll), :]
            for u in range(unroll):
                p = pos_ref[0, base + u]
                dst[u:u + 1, :] = yg_scr[pl.ds(p, 1), :]
            return carry

        lax.fori_loop(0, tb // unroll, combine, 0)
        y_ref[...] = _rms(xmid_ref[...] + y_ref[...], nfin_ref[...])


def _moe(xmid, route, meta, p, tb):
    t, d = xmid.shape
    nb = t // tb
    rows = tb + N_GROUPS_E * MOE_ALIGN
    pos = route[:, 1].astype(jnp.int32).reshape(nb, 1, tb)
    meta_i = jnp.concatenate([meta[:, 0, :N_GROUPS_E], meta[:, 1, :N_GROUPS_E]], axis=1).astype(jnp.int32)
    grid_spec = pltpu.PrefetchScalarGridSpec(
        num_scalar_prefetch=1,
        grid=(nb, N_EXPERTS // MOE_EXPERTS_PER_STEP),
        in_specs=[
            pl.BlockSpec((None, 1, tb), lambda i, e, m: (i, 0, 0), memory_space=pltpu.SMEM),
            pl.BlockSpec((tb, d), lambda i, e, m: (i, 0)),
            pl.BlockSpec((tb, 128), lambda i, e, m: (i, 0)),
            pl.BlockSpec((1, d), lambda i, e, m: (0, 0)),
            pl.BlockSpec((1, d), lambda i, e, m: (0, 0)),
            pl.BlockSpec((MOE_EXPERTS_PER_STEP, d, D_EXPERT), lambda i, e, m: (e, 0, 0)),
            pl.BlockSpec((MOE_EXPERTS_PER_STEP, d, D_EXPERT), lambda i, e, m: (e, 0, 0)),
            pl.BlockSpec((MOE_EXPERTS_PER_STEP, D_EXPERT, d), lambda i, e, m: (e, 0, 0)),
        ],
        out_specs=pl.BlockSpec((tb, d), lambda i, e, m: (i, 0)),
        scratch_shapes=[pltpu.VMEM((rows, d), F32), pltpu.VMEM((rows, 128), F32), pltpu.VMEM((rows, d), F32)],
    )
    return pl.pallas_call(
        functools.partial(_moe_kernel, tb),
        grid_spec=grid_spec,
        out_shape=jax.ShapeDtypeStruct((t, d), F32),
        compiler_params=pltpu.CompilerParams(dimension_semantics=("parallel", "arbitrary"),
                                             vmem_limit_bytes=VMEM_LIMIT),
        name="moe",
    )(meta_i, pos, xmid, route, p["norm_ffn"], p["norm_final"], p["w_gate"], p["w_up"], p["w_down"])


def _pick_tile(t, pref):
    tm = min(pref, t)
    while t % tm:
        tm //= 2
    return tm


def _prep_layer(l, norm_mix, w_in, conv_a_w, a_log_a, dt_bias_a, norm_a, conv_b_w, conv_b_b, a_log_b, dt_bias_b,
                d_skip_b, norm_b, w_out, norm_ffn, w_router_group, b_router_group, w_router_expert,
                b_router_expert, w_gate_e, w_up_e, w_down_e, norm_final):
    def on_conv_columns(a_qkv, a_xbc):
        out = jnp.zeros(a_qkv.shape[:-1] + (D_PROJ,), F32)
        out = out.at[..., OFF_QKV:OFF_QKV + D_QKV_A].set(a_qkv.astype(F32))
        return out.at[..., OFF_XBC:OFF_XBC + D_XBC].set(a_xbc.astype(F32))

    def slab(a8, b16):
        return jnp.concatenate([jnp.zeros((H_A,), F32), a8.astype(F32), b16.astype(F32),
                                jnp.zeros((128 - 2 * H_A - H_B,), F32)]).reshape(1, 128)

    w_router = jnp.concatenate(
        [w_router_group[l].astype(F32), w_router_expert[l].reshape(D_MODEL, N_EXPERTS).astype(F32),
         jnp.zeros((D_MODEL, 128 - N_GROUPS_E - N_EXPERTS), F32)], axis=1)
    w_router_hi = _bf(w_router)
    w_router = jnp.concatenate([w_router_hi, _bf(w_router - w_router_hi.astype(F32))], axis=1)
    b_router = jnp.concatenate(
        [b_router_group[l].astype(F32), b_router_expert[l].reshape(N_EXPERTS).astype(F32),
         jnp.zeros((128 - N_GROUPS_E - N_EXPERTS,), F32)]).reshape(1, 128)
    return {
        "norm_mix": norm_mix[l].astype(F32).reshape(1, D_MODEL),
        "w_all": _prep_w_in(w_in, l),
        "conv_w": on_conv_columns(conv_a_w[l], conv_b_w[l]),
        "conv_b": on_conv_columns(jnp.zeros((1, D_QKV_A), F32), conv_b_b[l].reshape(1, D_XBC)),
        "bias_slab": slab(dt_bias_a[l], dt_bias_b[l]),
        "alog_slab": slab(a_log_a[l], a_log_b[l]),
        "norm_a_x": jnp.tile(norm_a[l].astype(F32), H_A).reshape(1, D_V_A),
        "norm_b": norm_b[l].astype(F32).reshape(1, D_B),
        "dskip_x": jnp.repeat(d_skip_b[l].astype(F32), P_B).reshape(1, D_B),
        "w_out": _bf(w_out[l]),
        "norm_ffn": norm_ffn[l].astype(F32).reshape(1, D_MODEL),
        "w_router": w_router,
        "b_router": b_router,
        "w_gate": _bf(w_gate_e[l].reshape(N_EXPERTS, D_MODEL, D_EXPERT)),
        "w_up": _bf(w_up_e[l].reshape(N_EXPERTS, D_MODEL, D_EXPERT)),
        "w_down": _bf(w_down_e[l].reshape(N_EXPERTS, D_EXPERT, D_MODEL)),
        "norm_final": norm_final.astype(F32).reshape(1, D_MODEL),
    }


def _ffn_tail(mix2d, x2d, p):
    t = x2d.shape[0]
    tb = _pick_tile(t, MOE_BLOCK)
    xmid, route, meta = _outproj_router(mix2d, x2d, p, tb, MOE_ALIGN)
    return _moe(xmid, route, meta, p, tb)


def kernel(x_prompt, x_sample, state_delta, state_delta_conv, state_ssm, state_ssm_conv, norm_mix, w_in, conv_a_w,
           a_log_a, dt_bias_a, norm_a, conv_b_w, conv_b_b, a_log_b, dt_bias_b, d_skip_b, norm_b, w_out, norm_ffn,
           w_router_group, b_router_group, w_router_expert, b_router_expert, w_gate_e, w_up_e, w_down_e,
           norm_final):
    depth = w_in.shape[0]
    assert depth == 1, "the fused final norm assumes a single layer"
    bp, lp, d = x_prompt.shape
    bs, ls, _ = x_sample.shape
    assert lp % PROMPT_CHUNK == 0 and lp >= CONV_W - 1
    assert PAD_FRONT + ls <= SEQ_ROWS and bs % (CHUNK // SEQ_ROWS) == 0 and (CHUNK // SEQ_ROWS * ls) % 8 == 0
    l = 0
    p = _prep_layer(l, norm_mix, w_in, conv_a_w, a_log_a, dt_bias_a, norm_a, conv_b_w, conv_b_b, a_log_b,
                    dt_bias_b, d_skip_b, norm_b, w_out, norm_ffn, w_router_group, b_router_group,
                    w_router_expert, b_router_expert, w_gate_e, w_up_e, w_down_e, norm_final)

    xp2 = x_prompt.reshape(bp * lp, d)
    tm_p = _pick_tile(lp, 256)
    proj_p, tails = _inproj_conv(xp2, p, tm_p, lp)
    mix_p, delta_p, ssm_p = _mixer_prompt(proj_p.reshape(bp, lp, D_PROJ), p)
    tails = tails.reshape(bp, lp // tm_p, 8, D_CONV)[:, -1, 8 - (CONV_W - 1):]
    dconv_p = tails[:, :, :D_QKV_A]
    sconv_p = tails[:, :, D_QKV_A:]
    y_p = _ffn_tail(mix_p.reshape(bp * lp, -1), xp2, p).reshape(bp, lp, d)

    xs2 = x_sample.reshape(bs * ls, d)
    proj_s = _inproj(xs2, p, _pick_tile(bs * ls, 256))
    dconv_in = state_delta_conv[l].astype(F32)
    sconv_in = state_ssm_conv[l].astype(F32)
    mix_s, delta_s, ssm_s = _mixer_decode(proj_s, dconv_in, sconv_in, state_delta[l].astype(F32),
                                          state_ssm[l].astype(F32).reshape(bs, D_B, N_B), p)
    mix_s = mix_s.reshape(bs, SEQ_ROWS, -1)[:, PAD_FRONT:PAD_FRONT + ls].reshape(bs * ls, -1)
    dconv_s = jnp.concatenate([dconv_in, proj_s[:, OFF_QKV:OFF_QKV + D_QKV_A].reshape(bs, ls, D_QKV_A)], axis=1)[:, ls:]
    sconv_s = jnp.concatenate([sconv_in, proj_s[:, OFF_XBC:OFF_XBC + D_XBC].reshape(bs, ls, D_XBC)], axis=1)[:, ls:]
    y_s = _ffn_tail(mix_s, xs2, p).reshape(bs, ls, d)

    return (y_p.astype(x_prompt.dtype), y_s.astype(x_sample.dtype),
            delta_p[None], dconv_p[None], ssm_p.reshape(bp, H_B, P_B, N_B)[None], sconv_p[None],
            delta_s[None], dconv_s[None], ssm_s.reshape(bs, H_B, P_B, N_B)[None], sconv_s[None])
```

```python
import functools

import numpy as np
import jax
import jax.numpy as jnp
from jax import lax
from jax.experimental import pallas as pl
from jax.experimental.pallas import tpu as pltpu

F32 = jnp.float32
BF16 = jnp.bfloat16

D_MODEL = 1024
H_A, DK_A, DV_A = 8, 128, 128
D_QK_A = H_A * DK_A
D_V_A = H_A * DV_A
D_QKV_A = 2 * D_QK_A + D_V_A
H_B, P_B, N_B, G_B = 16, 64, 128, 2
D_B = H_B * P_B
D_XBC = D_B + 2 * G_B * N_B
CONV_W = 4
N_GROUPS_E, EXPERTS_PER_GROUP = 4, 4
N_EXPERTS = N_GROUPS_E * EXPERTS_PER_GROUP
D_EXPERT = 512
EPS = 1e-6

OFF_QKV = 0
OFF_GATE = OFF_QKV + D_QKV_A
OFF_Z = OFF_GATE + D_V_A
OFF_XBC = OFF_Z + D_B
OFF_SMALL = OFF_XBC + D_XBC
D_PROJ = OFF_SMALL + 128
D_CONV = D_QKV_A + D_XBC
CONV_RANGES = ((OFF_QKV, D_QKV_A), (OFF_XBC, D_XBC))
LANE_G = H_A
LANE_DT = 2 * H_A

CHUNK = 64
PROMPT_CHUNK = 128
SEQ_ROWS = 8
PROMPT_STREAMS = 2
PAD_FRONT = CONV_W - 1
INPROJ_SPREAD = 1
INPROJ_COLS = 256
MOE_BLOCK = 1024
MOE_SUB = 256
MOE_EXPERTS_PER_STEP = 2
MOE_ALIGN = 64
VMEM_LIMIT = 52 * 1024 * 1024


def _bf(x):
    return x.astype(BF16)


def _mm(a, b):
    return jnp.dot(_bf(a), _bf(b), preferred_element_type=F32)


def _mm_nt(a, b):
    return lax.dot_general(_bf(a), _bf(b), (((1,), (1,)), ((), ())), preferred_element_type=F32)


def _mm_tn(a, b):
    return lax.dot_general(_bf(a), _bf(b), (((0,), (0,)), ((), ())), preferred_element_type=F32)


def _split3(x):
    x1 = _bf(x)
    r1 = x - x1.astype(F32)
    x2 = _bf(r1)
    x3 = _bf(r1 - x2.astype(F32))
    return x1, x2, x3


def _mm01_r(x, m01):
    x1, x2, x3 = _split3(x)
    d = functools.partial(jnp.dot, preferred_element_type=F32)
    return d(x1, m01) + d(x2, m01) + d(x3, m01)


def _mm01_r2(x, m01):
    x1 = _bf(x)
    x2 = _bf(x - x1.astype(F32))
    d = functools.partial(jnp.dot, preferred_element_type=F32)
    return d(x1, m01) + d(x2, m01)


def _mm01_l(m01, x):
    x1, x2, x3 = _split3(x)
    d = functools.partial(jnp.dot, preferred_element_type=F32)
    return d(m01, x1) + d(m01, x2) + d(m01, x3)


def _rms(x, w):
    return x * lax.rsqrt(jnp.mean(x * x, axis=-1, keepdims=True) + EPS) * w


def _silu(x):
    return x * jax.nn.sigmoid(x)


def _softplus(x):
    return jnp.maximum(x, 0.0) + jnp.log1p(jnp.exp(-jnp.abs(x)))


def _weight_blocks(width):
    return [slice(c0, min(c0 + width, D_PROJ)) for c0 in range(0, D_PROJ, width)]


def _is_conv_block(dst):
    return any(lo <= dst.start and dst.stop <= lo + n for lo, n in CONV_RANGES)


def _prep_w_in_kernel(w_ref, o_ref):
    c_beta = D_QKV_A + D_V_A
    c_z = c_beta + 2 * H_A
    c_dt = c_z + D_B + D_XBC
    o_ref[:, OFF_QKV:OFF_Z] = _bf(w_ref[:, 0:c_beta])
    o_ref[:, OFF_Z:OFF_SMALL] = _bf(w_ref[:, c_z:c_dt])
    small = jnp.concatenate([w_ref[:, c_beta:c_z], w_ref[:, c_dt:c_dt + H_B],
                             jnp.zeros((w_ref.shape[0], 128 - 2 * H_A - H_B), F32)], axis=1)
    o_ref[:, OFF_SMALL:D_PROJ] = _bf(small)


def _prep_w_in(w, l):
    _, d, n = w.shape
    rows = 128
    return pl.pallas_call(
        _prep_w_in_kernel,
        grid=(d // rows,),
        in_specs=[pl.BlockSpec((None, rows, n), lambda i: (l, i, 0))],
        out_specs=pl.BlockSpec((rows, D_PROJ), lambda i: (i, 0)),
        out_shape=jax.ShapeDtypeStruct((d, D_PROJ), BF16),
        compiler_params=pltpu.CompilerParams(dimension_semantics=("parallel",)),
        name="prep_w_in",
    )(w.astype(F32))


def _inproj_kernel(x_ref, nw_ref, w_ref, o_ref):
    h = _bf(_rms(x_ref[...], nw_ref[...]))
    o_ref[...] = jnp.dot(h, w_ref[...], preferred_element_type=F32)


def _weight_specs(p):
    return [pl.BlockSpec(p["w_all"].shape, lambda i: (0, 0), pipeline_mode=pl.Buffered(1))]


def _inproj(x2d, p, tm):
    t, d = x2d.shape
    return pl.pallas_call(
        _inproj_kernel,
        grid=(t // tm,),
        in_specs=[pl.BlockSpec((tm, d), lambda i: (i, 0)), pl.BlockSpec((1, d), lambda i: (0, 0))] + _weight_specs(p),
        out_specs=pl.BlockSpec((tm, D_PROJ), lambda i: (i, 0)),
        out_shape=jax.ShapeDtypeStruct((t, D_PROJ), F32),
        compiler_params=pltpu.CompilerParams(dimension_semantics=("parallel",), vmem_limit_bytes=VMEM_LIMIT),
        name="inproj",
    )(x2d, p["norm_mix"], p["w_all"])


def _conv_silu(u, prev8, cw_ref, cb_ref, sl):
    row8 = lax.broadcasted_iota(jnp.int32, (8, 1), 0)
    acc = u * cw_ref[CONV_W - 1:CONV_W, sl] + cb_ref[:, sl]
    for k in range(1, CONV_W):
        ru = pltpu.roll(u, k, 0)
        if prev8 is not None:
            head = jnp.where(row8 < k, pltpu.roll(prev8, k, 0), ru[0:8])
            ru = jnp.concatenate([head, ru[8:]], axis=0)
        acc = acc + ru * cw_ref[CONV_W - 1 - k:CONV_W - k, sl]
    return _silu(acc)


def _inproj_conv_kernel(tiles_per_seq, n_cast, x_ref, nw_ref, w_ref, cw_ref, cb_ref, *rest):
    cast_in = rest[:n_cast]
    o_ref, tail_ref = rest[n_cast:n_cast + 2]
    cast_out = rest[n_cast + 2:2 * n_cast + 2]
    prev_ref = rest[2 * n_cast + 2]
    for src, dst in zip(cast_in, cast_out):
        dst[...] = _bf(src[...])

    @pl.when(pl.program_id(0) % tiles_per_seq == 0)
    def _():
        prev_ref[...] = jnp.zeros_like(prev_ref)

    h = _bf(_rms(x_ref[...], nw_ref[...]))
    tm = h.shape[0]
    blocks = _weight_blocks(INPROJ_COLS)
    conv_blocks = [b for b in blocks if _is_conv_block(b)]
    plain_blocks = [b for b in blocks if not _is_conv_block(b)]
    blocks = []
    for j, b in enumerate(conv_blocks):
        blocks.append(b)
        if j % INPROJ_SPREAD == INPROJ_SPREAD - 1 and plain_blocks:
            blocks.append(plain_blocks.pop(0))
    blocks += plain_blocks
    for dst in blocks:
        u = jnp.dot(h, w_ref[:, dst], preferred_element_type=F32)
        if _is_conv_block(dst):
            o_ref[:, dst] = _conv_silu(u, prev_ref[:, dst], cw_ref, cb_ref, dst)
            prev_ref[:, dst] = u[tm - 8:tm]
            t0 = dst.start - OFF_QKV if dst.start < OFF_XBC else dst.start - OFF_XBC + D_QKV_A
            tail_ref[:, t0:t0 + (dst.stop - dst.start)] = u[tm - 8:tm]
        else:
            o_ref[:, dst] = u


def _can_cast_in_steps(a, steps):
    return a.shape[0] % steps == 0 and (a.shape[0] // steps) % 16 == 0


def _inproj_conv(x2d, p, tm, seq_len, to_cast):
    t, d = x2d.shape
    assert seq_len % tm == 0
    steps = t // tm
    cast_specs = [pl.BlockSpec((a.shape[0] // steps, a.shape[1]), lambda i: (i, 0)) for a in to_cast]
    res = pl.pallas_call(
        functools.partial(_inproj_conv_kernel, seq_len // tm, len(to_cast)),
        grid=(steps,),
        in_specs=[pl.BlockSpec((tm, d), lambda i: (i, 0)), pl.BlockSpec((1, d), lambda i: (0, 0))] + _weight_specs(p)
        + [_const_spec(p["conv_w"].shape), _const_spec(p["conv_b"].shape)] + cast_specs,
        out_specs=[pl.BlockSpec((tm, D_PROJ), lambda i: (i, 0)), pl.BlockSpec((None, 8, D_CONV), lambda i: (i, 0, 0))]
        + cast_specs,
        out_shape=[jax.ShapeDtypeStruct((t, D_PROJ), F32), jax.ShapeDtypeStruct((steps, 8, D_CONV), F32)]
        + [jax.ShapeDtypeStruct(a.shape, BF16) for a in to_cast],
        scratch_shapes=[pltpu.VMEM((8, D_PROJ), F32)],
        compiler_params=pltpu.CompilerParams(dimension_semantics=("arbitrary",), vmem_limit_bytes=VMEM_LIMIT),
        name="inproj_conv",
    )(x2d, p["norm_mix"], p["w_all"], p["conv_w"], p["conv_b"], *to_cast)
    return res[0], res[1], list(res[2:])


def _inv_unit_lower(mats, ri, ci, eye, merge_sizes):
    blk = (ri >> 3) == (ci >> 3)
    ads = [jnp.where(blk, a, 0.0) for a in mats]
    xs = [eye - ad for ad in ads]
    n = eye.shape[0]
    a2s = [_mm(ad, ad) for ad in ads]
    st = [_mm(jnp.concatenate([x, a2], axis=0), a2) for x, a2 in zip(xs, a2s)]
    xs = [x + t[:n] for x, t in zip(xs, st)]
    xs = [x + _mm(x, t[n:]) for x, t in zip(xs, st)]
    for s in merge_sizes:
        sh = s.bit_length() - 1
        rb = ri >> sh
        sel = ((rb & 1) == 1) & ((ci >> sh) == rb - 1)
        odd = [slice(r, r + s) for r in range(s, n, 2 * s)]
        even = [slice(r, r + s) for r in range(0, n, 2 * s)]
        xo = [jnp.concatenate([x[sl] for sl in odd], axis=0) for x in xs]
        ts = [_mm(o, jnp.where(sel, a, 0.0)) for o, a in zip(xo, mats)]
        xo = [o - _mm(t, x) for o, t, x in zip(xo, ts, xs)]
        xs = [jnp.concatenate([blk for i, ev in enumerate(even) for blk in (x[ev], o[i * s:(i + 1) * s])], axis=0)
              for x, o in zip(xs, xo)]
    return xs


def _mixer_kernel(c, n_streams, n_seq, valid_lo, valid_hi, *refs):
    carry = n_seq == 1
    (proj_ref, cw_ref, cb_ref, bias_ref, alog_ref, na_ref, nb_ref, dsk_ref,
     ea_ref, eb_ref, ec_ref) = refs[:11]
    if carry:
        mix_ref, s_wr, h_wr = refs[11:]
        s_rd, h_rd = s_wr, h_wr
        xc_ref = proj_ref

        @pl.when(pl.program_id(1) == 0)
        def _():
            s_wr[...] = jnp.zeros_like(s_wr)
            h_wr[...] = jnp.zeros_like(h_wr)
    else:
        dconv_ref, sconv_ref, s_rd, h_rd, mix_ref, s_wr, h_wr, xc_ref, tile_ref = refs[11:]
        tok_ref = proj_ref
        l_tok = tok_ref.shape[0] // n_seq
        tile_ref[...] = jnp.zeros_like(tile_ref)
        for s in range(n_seq):
            r0 = (c // n_seq) * s
            tile_ref[0, r0:r0 + CONV_W - 1, OFF_QKV:OFF_QKV + D_QKV_A] = dconv_ref[s]
            tile_ref[0, r0:r0 + CONV_W - 1, OFF_XBC:OFF_XBC + D_XBC] = sconv_ref[s]
            tile_ref[0, r0 + valid_lo:r0 + valid_lo + l_tok, :] = tok_ref[l_tok * s:l_tok * (s + 1), :]
        proj_ref = tile_ref

    r_seq = c // n_seq
    sh_seq = r_seq.bit_length() - 1
    row = lax.broadcasted_iota(jnp.int32, (c, 1), 0)
    lane = lax.broadcasted_iota(jnp.int32, (1, 128), 1)
    ri = lax.broadcasted_iota(jnp.int32, (c, c), 0)
    ci = lax.broadcasted_iota(jnp.int32, (c, c), 1)
    eye = (ri == ci).astype(F32)
    seq_of_row = row >> sh_seq
    if carry:
        same = None
        causal = ri >= ci
        strict = ri > ci
        valid = None
        merge_sizes = tuple(8 << i for i in range((c // 8).bit_length() - 1))
    else:
        same = (ri >> sh_seq) == (ci >> sh_seq)
        causal = (ri >= ci) & same
        strict = (ri > ci) & same
        rr = row & (r_seq - 1)
        valid = ((rr >= valid_lo) & (rr < valid_hi)).astype(F32)
        merge_sizes = ()
    causal_bf = _bf(causal.astype(F32))
    is_beta = lane < LANE_G
    is_g = (lane >= LANE_G) & (lane < LANE_DT)
    is_dt = (lane >= LANE_DT) & (lane < LANE_DT + H_B)
    streams = range(n_streams)

    def conv(st):
        for lo, n in CONV_RANGES:
            for c0 in range(lo, lo + n, 512):
                sl = slice(c0, c0 + 512)
                xc_ref[st, :, sl] = _conv_silu(proj_ref[st, :, sl], None, cw_ref, cb_ref, sl)

    def gates(st):
        raw = proj_ref[st, :, OFF_SMALL:OFF_SMALL + 128]
        sp = _softplus(raw + bias_ref[...])
        coef = -jnp.exp(alog_ref[...])
        gd = jnp.where(is_g | is_dt, sp * coef, 0.0)
        q1 = jnp.where(is_beta, jax.nn.sigmoid(raw), jnp.where(is_dt, sp, 0.0))
        if valid is not None:
            gd = gd * valid
            q1 = q1 * valid
        gc = _mm01_l(causal_bf, gd)
        if carry:
            tot = jnp.broadcast_to(gc[c - 1:c, :], (c, 128))
        else:
            tot = _mm01_l(_bf(same.astype(F32)), gd)
        eg = jnp.exp(gc)
        et = jnp.exp(tot - gc)
        xa = _mm01_r2(jnp.concatenate([q1, jnp.where(is_g, eg, 0.0), jnp.where(is_g, et, 0.0)], axis=0),
                      ea_ref[...])
        xb = _mm01_r2(jnp.concatenate([q1, eg, et], axis=0), eb_ref[...])
        etot = jnp.exp(tot[0:8] if carry else tot)
        return {
            "gc": gc, "gct": gc.T,
            "beta_x": xa[0:c], "egc_x": xa[c:2 * c], "tail_x": xa[2 * c:3 * c],
            "dt_x": xb[0:c], "edac_x": xb[c:2 * c], "tailb_x": xb[2 * c:3 * c],
            "dec_x": _mm01_r(etot, ec_ref[...]),
        }

    if not carry:
        for st in streams:
            conv(st)
    gts = [gates(st) for st in streams]

    pairs = [(st, h) for st in streams for h in range(H_A)]
    hsl = [slice(128 * h, 128 * (h + 1)) for h in range(H_A)]
    qs, ks, vs = [], [], []
    for st, h in pairs:
        q = xc_ref[st, :, OFF_QKV + 128 * h:OFF_QKV + 128 * (h + 1)]
        k = xc_ref[st, :, OFF_QKV + D_QK_A + 128 * h:OFF_QKV + D_QK_A + 128 * (h + 1)]
        qs.append(q * lax.rsqrt(jnp.sum(q * q, axis=-1, keepdims=True) + EPS) * (DK_A ** -0.5))
        ks.append(k * lax.rsqrt(jnp.sum(k * k, axis=-1, keepdims=True) + EPS))
        vs.append(xc_ref[st, :, OFF_QKV + 2 * D_QK_A + 128 * h:OFF_QKV + 2 * D_QK_A + 128 * (h + 1)])
    n_p = len(pairs)
    betas = [gts[st]["beta_x"][:, hsl[h]] for st, h in pairs]
    egcs = [gts[st]["egc_x"][:, hsl[h]] for st, h in pairs]
    kbs = [ks[i] * betas[i] for i in range(n_p)]
    nts = [_mm_nt(jnp.concatenate([kbs[i], qs[i]], axis=0), ks[i]) for i in range(n_p)]
    decs = [jnp.exp(jnp.where(causal, gts[st]["gc"][:, LANE_G + h:LANE_G + h + 1]
                              - gts[st]["gct"][LANE_G + h:LANE_G + h + 1, :], -jnp.inf)) for st, h in pairs]
    a_s = [jnp.where(strict, nts[i][:c] * decs[i], 0.0) for i in range(n_p)]
    qks = [nts[i][c:] * decs[i] for i in range(n_p)]
    x_invs = _inv_unit_lower(a_s, ri, ci, eye, merge_sizes)
    sols = [_mm(x_invs[i], jnp.concatenate([vs[i] * betas[i], kbs[i] * egcs[i]], axis=1)) for i in range(n_p)]
    qes = [qs[i] * egcs[i] for i in range(n_p)]
    if n_seq == 1:
        vnews = [sols[i][:, :128] - _mm(sols[i][:, 128:], s_rd[st, h]) for i, (st, h) in enumerate(pairs)]
        os_ = [_mm(jnp.concatenate([qes[i], qks[i]], axis=1),
                   jnp.concatenate([s_rd[st, h], vnews[i]], axis=0)) for i, (st, h) in enumerate(pairs)]
    else:
        wss = [[_mm(jnp.concatenate([sols[i][r_seq * s:r_seq * (s + 1), 128:], qes[i][r_seq * s:r_seq * (s + 1)]],
                                    axis=0), s_rd[st * n_seq + s, h]) for s in range(n_seq)]
               for i, (st, h) in enumerate(pairs)]
        vnews = [jnp.concatenate([sols[i][r_seq * s:r_seq * (s + 1), :128] - wss[i][s][:r_seq]
                                  for s in range(n_seq)], axis=0) for i in range(n_p)]
        os_ = [jnp.concatenate([wss[i][s][r_seq:] for s in range(n_seq)], axis=0) + _mm(qks[i], vnews[i])
               for i in range(n_p)]
    for i, (st, h) in enumerate(pairs):
        kt = ks[i] * gts[st]["tail_x"][:, hsl[h]]
        for s in range(n_seq):
            kts = kt if n_seq == 1 else jnp.where(seq_of_row == s, kt, 0.0)
            sq = st * n_seq + s
            s_wr[sq, h] = s_rd[sq, h] * gts[st]["dec_x"][r_seq * s:r_seq * s + 1, hsl[h]] + _mm_tn(kts, vnews[i])
    for i, (st, h) in enumerate(pairs):
        gate = proj_ref[st, :, OFF_GATE + 128 * h:OFF_GATE + 128 * (h + 1)]
        mix_ref[st, :, hsl[h]] = _bf(_rms(os_[i], na_ref[:, hsl[h]]) * _silu(gate))

    hg = H_B // G_B
    wg = hg * P_B
    off_x = OFF_XBC
    off_b = off_x + D_B
    off_c = off_b + G_B * N_B
    for st, g in [(st, g) for st in streams for g in range(G_B)]:
        gt = gts[st]
        gc, gct = gt["gc"], gt["gct"]
        gs = slice(wg * g, wg * (g + 1))
        bg = xc_ref[st, :, off_b + N_B * g:off_b + N_B * (g + 1)]
        cg = xc_ref[st, :, off_c + N_B * g:off_c + N_B * (g + 1)]
        xs_g = xc_ref[st, :, off_x + wg * g:off_x + wg * (g + 1)]
        xdt = xs_g * gt["dt_x"][:, gs]
        cb = _mm_nt(cg, bg)
        ypairs = []
        for p in range(hg // 2):
            xp = xdt[:, 128 * p:128 * (p + 1)]
            ms = []
            for a_ in (0, 1):
                ln = LANE_DT + hg * g + 2 * p + a_
                seg = jnp.exp(jnp.where(causal, gc[:, ln:ln + 1] - gct[ln:ln + 1, :], -jnp.inf))
                ms.append(cb * seg)
            xpp = jnp.concatenate([jnp.where(lane < P_B, xp, 0.0), jnp.where(lane >= P_B, xp, 0.0)], axis=0)
            ypairs.append(_mm(jnp.concatenate(ms, axis=1), xpp))
        y = jnp.concatenate(ypairs, axis=1)
        yoffs = []
        for s in range(n_seq):
            rows = slice(r_seq * s, r_seq * (s + 1))
            yoffs.append(_mm_nt(cg[rows], h_rd[st * n_seq + s, wg * g:wg * (g + 1), :]))
        yoff = yoffs[0] if n_seq == 1 else jnp.concatenate(yoffs, axis=0)
        y = y + yoff * gt["edac_x"][:, gs] + dsk_ref[:, gs] * xs_g
        xt = xdt * gt["tailb_x"][:, gs]
        for s in range(n_seq):
            xts = xt if n_seq == 1 else jnp.where(seq_of_row == s, xt, 0.0)
            stt = _mm_tn(xts, bg)
            sq = st * n_seq + s
            for j in range(hg):
                hh = hg * g + j
                rs = slice(P_B * hh, P_B * (hh + 1))
                drow = gt["dec_x"][r_seq * s:r_seq * s + 1, 128 * (H_A + hh):128 * (H_A + hh + 1)]
                h_wr[sq, rs, :] = h_rd[sq, rs, :] * drow + stt[P_B * j:P_B * (j + 1), :]
        z = proj_ref[st, :, OFF_Z + wg * g:OFF_Z + wg * (g + 1)]
        mix_ref[st, :, D_V_A + wg * g:D_V_A + wg * (g + 1)] = _bf(_rms(y * _silu(z), nb_ref[:, gs]))


def _expand_mats():
    ea = np.zeros((128, 128 * H_A), np.float32)
    for h in range(H_A):
        ea[h, 128 * h:128 * (h + 1)] = 1.0
        ea[LANE_G + h, 128 * h:128 * (h + 1)] = 1.0
    eb = np.zeros((128, D_B), np.float32)
    for h in range(H_B):
        eb[LANE_DT + h, P_B * h:P_B * (h + 1)] = 1.0
    ec = np.zeros((128, 128 * (H_A + H_B)), np.float32)
    for h in range(H_A):
        ec[LANE_G + h, 128 * h:128 * (h + 1)] = 1.0
    for h in range(H_B):
        ec[LANE_DT + h, 128 * (H_A + h):128 * (H_A + h + 1)] = 1.0
    return jnp.asarray(ea, BF16), jnp.asarray(eb, BF16), jnp.asarray(ec, BF16)


def _const_spec(shape):
    return pl.BlockSpec(shape, lambda *_: (0,) * len(shape))


def _mixer_params(p):
    ea, eb, ec = _expand_mats()
    return (p["conv_w"], p["conv_b"], p["bias_slab"], p["alog_slab"], p["norm_a_x"], p["norm_b"], p["dskip_x"],
            ea, eb, ec)


def _mixer_prompt(proj, p):
    b, l, n = proj.shape
    nb = PROMPT_STREAMS if b % PROMPT_STREAMS == 0 else 1
    consts = _mixer_params(p)
    const_specs = [_const_spec(a.shape) for a in consts]
    return pl.pallas_call(
        functools.partial(_mixer_kernel, PROMPT_CHUNK, nb, 1, 0, PROMPT_CHUNK),
        grid=(b // nb, l // PROMPT_CHUNK),
        in_specs=[pl.BlockSpec((nb, PROMPT_CHUNK, n), lambda i, t: (i, t, 0))] + const_specs,
        out_specs=[
            pl.BlockSpec((nb, PROMPT_CHUNK, D_V_A + D_B), lambda i, t: (i, t, 0)),
            pl.BlockSpec((nb, H_A, DK_A, DV_A), lambda i, t: (i, 0, 0, 0)),
            pl.BlockSpec((nb, D_B, N_B), lambda i, t: (i, 0, 0)),
        ],
        out_shape=[
            jax.ShapeDtypeStruct((b, l, D_V_A + D_B), BF16),
            jax.ShapeDtypeStruct((b, H_A, DK_A, DV_A), F32),
            jax.ShapeDtypeStruct((b, D_B, N_B), F32),
        ],
        compiler_params=pltpu.CompilerParams(dimension_semantics=("parallel", "arbitrary"),
                                             vmem_limit_bytes=VMEM_LIMIT),
        name="mixer_prompt",
    )(proj, *consts)


def _mixer_decode(proj, dconv, sconv, s_delta, s_ssm, p):
    bs = s_delta.shape[0]
    n = proj.shape[1]
    l_dec = proj.shape[0] // bs
    n_seq = CHUNK // SEQ_ROWS
    tiles = bs // n_seq
    consts = _mixer_params(p)
    const_specs = [_const_spec(a.shape) for a in consts]
    return pl.pallas_call(
        functools.partial(_mixer_kernel, CHUNK, 1, n_seq, PAD_FRONT, PAD_FRONT + l_dec),
        grid=(tiles,),
        in_specs=[pl.BlockSpec((n_seq * l_dec, n), lambda i: (i, 0))] + const_specs + [
            pl.BlockSpec((n_seq, CONV_W - 1, D_QKV_A), lambda i: (i, 0, 0)),
            pl.BlockSpec((n_seq, CONV_W - 1, D_XBC), lambda i: (i, 0, 0)),
            pl.BlockSpec((n_seq, H_A, DK_A, DV_A), lambda i: (i, 0, 0, 0)),
            pl.BlockSpec((n_seq, D_B, N_B), lambda i: (i, 0, 0)),
        ],
        out_specs=[
            pl.BlockSpec((1, CHUNK, D_V_A + D_B), lambda i: (i, 0, 0)),
            pl.BlockSpec((n_seq, H_A, DK_A, DV_A), lambda i: (i, 0, 0, 0)),
            pl.BlockSpec((n_seq, D_B, N_B), lambda i: (i, 0, 0)),
        ],
        out_shape=[
            jax.ShapeDtypeStruct((tiles, CHUNK, D_V_A + D_B), BF16),
            jax.ShapeDtypeStruct((bs, H_A, DK_A, DV_A), F32),
            jax.ShapeDtypeStruct((bs, D_B, N_B), F32),
        ],
        scratch_shapes=[pltpu.VMEM((1, CHUNK, n), F32), pltpu.VMEM((1, CHUNK, n), F32)],
        compiler_params=pltpu.CompilerParams(dimension_semantics=("parallel",), vmem_limit_bytes=VMEM_LIMIT),
        name="mixer_decode",
    )(proj, *consts, dconv, sconv, s_delta, s_ssm)


def _outproj_router_kernel(sub, mix_ref, x_ref, wo_ref, nf_ref, wr_ref, br_ref, tril_ref, xmid_ref, route_ref,
                           meta_ref):
    xm = x_ref[...] + jnp.dot(mix_ref[...], wo_ref[...], preferred_element_type=F32)
    xmid_ref[...] = xm
    t = _rms(xm, nf_ref[...])
    t1 = _bf(t)
    t2 = _bf(t - t1.astype(F32))
    d = functools.partial(jnp.dot, preferred_element_type=F32)
    hl = d(t1, wr_ref[...])
    logit = hl[:, :128] + hl[:, 128:] + d(t2, wr_ref[:, :128]) + br_ref[...]

    lane = lax.broadcasted_iota(jnp.int32, (1, 128), 1).astype(F32)
    neg = -jnp.inf
    big = 1e9
    is_grp = lane < N_GROUPS_E
    gl = jnp.where(is_grp, logit, neg)
    gmax = jnp.max(gl, axis=-1, keepdims=True)
    gsel = jnp.min(jnp.where(gl == gmax, lane, big), axis=-1, keepdims=True)
    gw = 1.0 / jnp.sum(jnp.exp(jnp.where(is_grp, logit - gmax, neg)), axis=-1, keepdims=True)
    lo = N_GROUPS_E + EXPERTS_PER_GROUP * gsel
    el = jnp.where((lane >= lo) & (lane < lo + EXPERTS_PER_GROUP), logit, neg)
    v1 = jnp.max(el, axis=-1, keepdims=True)
    i1 = jnp.min(jnp.where(el == v1, lane, big), axis=-1, keepdims=True)
    el2 = jnp.where(lane == i1, neg, el)
    v2 = jnp.max(el2, axis=-1, keepdims=True)
    i2 = jnp.min(jnp.where(el2 == v2, lane, big), axis=-1, keepdims=True)
    e = jnp.exp(v2 - v1)
    den = gw / (1.0 + e)
    comb = jnp.where(lane == i1, den, 0.0) + jnp.where(lane == i2, e * den, 0.0)

    tm = logit.shape[0]
    onehot = jnp.where((lane == gsel) & is_grp, 1.0, 0.0)
    ranks = jnp.dot(tril_ref[...], _bf(onehot), preferred_element_type=F32)
    cnt = ranks[tm - 1:tm, :] + onehot[tm - 1:tm, :]
    ntile = jnp.floor((cnt + (sub - 1)) * (1.0 / sub))
    li = lax.broadcasted_iota(jnp.int32, (128, 128), 0)
    lj = lax.broadcasted_iota(jnp.int32, (128, 128), 1)
    off = _mm01_r(jnp.broadcast_to(ntile * sub, (8, 128)), _bf((li < lj).astype(F32)))[0:1]
    pos = jnp.sum(onehot * (off + ranks), axis=-1, keepdims=True)
    route_ref[...] = jnp.where(lane == 0.0, gsel, jnp.where(lane == 1.0, pos, comb))
    row8 = lax.broadcasted_iota(jnp.int32, (8, 128), 0)
    meta_ref[...] = jnp.where(row8 == 0, off, ntile)


def _outproj_router(mix, x2d, p, tm, sub):
    t, d = x2d.shape
    nb = t // tm
    tril = jnp.asarray(np.tri(tm, tm, -1, dtype=np.float32), BF16)
    return pl.pallas_call(
        functools.partial(_outproj_router_kernel, sub),
        grid=(nb,),
        in_specs=[
            pl.BlockSpec((tm, mix.shape[1]), lambda i: (i, 0)),
            pl.BlockSpec((tm, d), lambda i: (i, 0)),
            _const_spec(p["w_out"].shape),
            _const_spec((1, d)),
            _const_spec(p["w_router"].shape),
            _const_spec((1, 128)),
            _const_spec((tm, tm)),
        ],
        out_specs=[pl.BlockSpec((tm, d), lambda i: (i, 0)), pl.BlockSpec((tm, 128), lambda i: (i, 0)),
                   pl.BlockSpec((None, 8, 128), lambda i: (i, 0, 0))],
        out_shape=[jax.ShapeDtypeStruct((t, d), F32), jax.ShapeDtypeStruct((t, 128), F32),
                   jax.ShapeDtypeStruct((nb, 8, 128), F32)],
        compiler_params=pltpu.CompilerParams(dimension_semantics=("parallel",), vmem_limit_bytes=VMEM_LIMIT),
        name="outproj_router",
    )(mix, x2d, p["w_out"], p["norm_ffn"], p["w_router"], p["b_router"], tril)


def _moe_kernel(tb, meta_ref, pos_ref, xmid_ref, route_ref, nffn_ref, nfin_ref, wg_ref, wu_ref, wd_ref,
                y_ref, xg_scr, rg_scr, yg_scr):
    b = pl.program_id(0)
    step = pl.program_id(1)
    n_steps = N_EXPERTS // MOE_EXPERTS_PER_STEP
    steps_per_group = EXPERTS_PER_GROUP // MOE_EXPERTS_PER_STEP
    g = step // steps_per_group
    unroll = 8

    @pl.when(step == 0)
    def _():
        y_ref[...] = _rms(xmid_ref[...], nffn_ref[...])
        for gg in range(N_GROUPS_E):
            nq_g = meta_ref[b, N_GROUPS_E + gg]

            @pl.when(nq_g > 0)
            def _(gg=gg, nq_g=nq_g):
                r = pl.multiple_of(meta_ref[b, gg] + (nq_g - 1) * MOE_ALIGN, MOE_ALIGN)
                xg_scr[pl.ds(r, MOE_ALIGN), :] = jnp.zeros((MOE_ALIGN, D_MODEL), F32)
                rg_scr[pl.ds(r, MOE_ALIGN), :] = jnp.zeros((MOE_ALIGN, 128), F32)

        def dispatch(i, carry):
            base = pl.multiple_of(i * unroll, unroll)
            src = y_ref.at[pl.ds(base, unroll), :]
            rsrc = route_ref.at[pl.ds(base, unroll), :]
            for u in range(unroll):
                p = pos_ref[0, base + u]
                xg_scr[pl.ds(p, 1), :] = src[u:u + 1, :]
                rg_scr[pl.ds(p, 1), :] = rsrc[u:u + 1, :]
            return carry

        lax.fori_loop(0, tb // unroll, dispatch, 0)

    off = meta_ref[b, g]
    nq = meta_ref[b, N_GROUPS_E + g]
    lane = lax.broadcasted_iota(jnp.int32, (1, 128), 1)
    first = (step % steps_per_group) == 0
    d = functools.partial(jnp.dot, preferred_element_type=F32)

    def tile(r0, m):
        r0 = pl.multiple_of(r0, MOE_ALIGN)
        x = _bf(xg_scr[pl.ds(r0, m), :])
        rg = rg_scr[pl.ds(r0, m), :]
        contrib = None
        for j in range(MOE_EXPERTS_PER_STEP):
            e16 = step * MOE_EXPERTS_PER_STEP + j
            cw = jnp.sum(jnp.where(lane == N_GROUPS_E + e16, rg, 0.0), axis=-1, keepdims=True)
            hid = _silu(d(x, wg_ref[j])) * d(x, wu_ref[j]) * cw
            cj = d(_bf(hid), wd_ref[j])
            contrib = cj if contrib is None else contrib + cj

        @pl.when(first)
        def _():
            yg_scr[pl.ds(r0, m), :] = contrib

        @pl.when(jnp.logical_not(first))
        def _():
            yg_scr[pl.ds(r0, m), :] += contrib

    per_tile = MOE_SUB // MOE_ALIGN
    nloop = jnp.maximum(nq // per_tile - 1, 0)

    def full_tile(j, carry):
        tile(off + j * MOE_SUB, MOE_SUB)
        return carry

    lax.fori_loop(0, nloop, full_tile, 0)
    last = nq - nloop * per_tile
    r_last = off + nloop * MOE_SUB
    for q in range(1, 2 * per_tile):
        @pl.when(last == q)
        def _(q=q):
            tile(r_last, q * MOE_ALIGN)

    @pl.when(step == n_steps - 1)
    def _():
        def combine(i, carry):
            base = pl.multiple_of(i * unroll, unroll)
            dst = y_ref.at[pl.ds(base, unroll), :]
            for u in range(unroll):
                p = pos_ref[0, base + u]
                dst[u:u + 1, :] = yg_scr[pl.ds(p, 1), :]
            return carry

        lax.fori_loop(0, tb // unroll, combine, 0)
        y_ref[...] = _rms(xmid_ref[...] + y_ref[...], nfin_ref[...])


def _moe(xmid, route, meta, p, tb):
    t, d = xmid.shape
    nb = t // tb
    rows = tb + N_GROUPS_E * MOE_ALIGN
    pos = route[:, 1].astype(jnp.int32).reshape(nb, 1, tb)
    meta_i = jnp.concatenate([meta[:, 0, :N_GROUPS_E], meta[:, 1, :N_GROUPS_E]], axis=1).astype(jnp.int32)
    grid_spec = pltpu.PrefetchScalarGridSpec(
        num_scalar_prefetch=1,
        grid=(nb, N_EXPERTS // MOE_EXPERTS_PER_STEP),
        in_specs=[
            pl.BlockSpec((None, 1, tb), lambda i, e, m: (i, 0, 0), memory_space=pltpu.SMEM),
            pl.BlockSpec((tb, d), lambda i, e, m: (i, 0)),
            pl.BlockSpec((tb, 128), lambda i, e, m: (i, 0)),
            pl.BlockSpec((1, d), lambda i, e, m: (0, 0)),
            pl.BlockSpec((1, d), lambda i, e, m: (0, 0)),
            pl.BlockSpec((MOE_EXPERTS_PER_STEP, d, D_EXPERT), lambda i, e, m: (e, 0, 0)),
            pl.BlockSpec((MOE_EXPERTS_PER_STEP, d, D_EXPERT), lambda i, e, m: (e, 0, 0)),
            pl.BlockSpec((MOE_EXPERTS_PER_STEP, D_EXPERT, d), lambda i, e, m: (e, 0, 0)),
        ],
        out_specs=pl.BlockSpec((tb, d), lambda i, e, m: (i, 0)),
        scratch_shapes=[pltpu.VMEM((rows, d), F32), pltpu.VMEM((rows, 128), F32), pltpu.VMEM((rows, d), F32)],
    )
    return pl.pallas_call(
        functools.partial(_moe_kernel, tb),
        grid_spec=grid_spec,
        out_shape=jax.ShapeDtypeStruct((t, d), F32),
        compiler_params=pltpu.CompilerParams(dimension_semantics=("parallel", "arbitrary"),
                                             vmem_limit_bytes=VMEM_LIMIT),
        name="moe",
    )(meta_i, pos, xmid, route, p["norm_ffn"], p["norm_final"], p["w_gate"], p["w_up"], p["w_down"])


def _pick_tile(t, pref):
    tm = min(pref, t)
    while t % tm:
        tm //= 2
    return tm


def _prep_layer(l, norm_mix, w_in, conv_a_w, a_log_a, dt_bias_a, norm_a, conv_b_w, conv_b_b, a_log_b, dt_bias_b,
                d_skip_b, norm_b, w_out, norm_ffn, w_router_group, b_router_group, w_router_expert,
                b_router_expert, w_gate_e, w_up_e, w_down_e, norm_final):
    def on_conv_columns(a_qkv, a_xbc):
        out = jnp.zeros(a_qkv.shape[:-1] + (D_PROJ,), F32)
        out = out.at[..., OFF_QKV:OFF_QKV + D_QKV_A].set(a_qkv.astype(F32))
        return out.at[..., OFF_XBC:OFF_XBC + D_XBC].set(a_xbc.astype(F32))

    def slab(a8, b16):
        return jnp.concatenate([jnp.zeros((H_A,), F32), a8.astype(F32), b16.astype(F32),
                                jnp.zeros((128 - 2 * H_A - H_B,), F32)]).reshape(1, 128)

    w_router = jnp.concatenate(
        [w_router_group[l].astype(F32), w_router_expert[l].reshape(D_MODEL, N_EXPERTS).astype(F32),
         jnp.zeros((D_MODEL, 128 - N_GROUPS_E - N_EXPERTS), F32)], axis=1)
    w_router_hi = _bf(w_router)
    w_router = jnp.concatenate([w_router_hi, _bf(w_router - w_router_hi.astype(F32))], axis=1)
    b_router = jnp.concatenate(
        [b_router_group[l].astype(F32), b_router_expert[l].reshape(N_EXPERTS).astype(F32),
         jnp.zeros((128 - N_GROUPS_E - N_EXPERTS,), F32)]).reshape(1, 128)
    return {
        "norm_mix": norm_mix[l].astype(F32).reshape(1, D_MODEL),
        "w_all": _prep_w_in(w_in, l),
        "conv_w": on_conv_columns(conv_a_w[l], conv_b_w[l]),
        "conv_b": on_conv_columns(jnp.zeros((1, D_QKV_A), F32), conv_b_b[l].reshape(1, D_XBC)),
        "bias_slab": slab(dt_bias_a[l], dt_bias_b[l]),
        "alog_slab": slab(a_log_a[l], a_log_b[l]),
        "norm_a_x": jnp.tile(norm_a[l].astype(F32), H_A).reshape(1, D_V_A),
        "norm_b": norm_b[l].astype(F32).reshape(1, D_B),
        "dskip_x": jnp.repeat(d_skip_b[l].astype(F32), P_B).reshape(1, D_B),
        "norm_ffn": norm_ffn[l].astype(F32).reshape(1, D_MODEL),
        "w_router": w_router,
        "b_router": b_router,
        "norm_final": norm_final.astype(F32).reshape(1, D_MODEL),
        "ffn_f32": [w_out[l].astype(F32),
                    w_gate_e[l].astype(F32).reshape(N_EXPERTS * D_MODEL, D_EXPERT),
                    w_up_e[l].astype(F32).reshape(N_EXPERTS * D_MODEL, D_EXPERT),
                    w_down_e[l].astype(F32).reshape(N_EXPERTS * D_EXPERT, D_MODEL)],
    }


def _set_ffn_weights(p, w_bf16):
    w_out, w_gate, w_up, w_down = w_bf16
    p["w_out"] = w_out
    p["w_gate"] = w_gate.reshape(N_EXPERTS, D_MODEL, D_EXPERT)
    p["w_up"] = w_up.reshape(N_EXPERTS, D_MODEL, D_EXPERT)
    p["w_down"] = w_down.reshape(N_EXPERTS, D_EXPERT, D_MODEL)


def _ffn_tail(mix2d, x2d, p):
    t = x2d.shape[0]
    tb = _pick_tile(t, MOE_BLOCK)
    xmid, route, meta = _outproj_router(mix2d, x2d, p, tb, MOE_ALIGN)
    return _moe(xmid, route, meta, p, tb)


def kernel(x_prompt, x_sample, state_delta, state_delta_conv, state_ssm, state_ssm_conv, norm_mix, w_in, conv_a_w,
           a_log_a, dt_bias_a, norm_a, conv_b_w, conv_b_b, a_log_b, dt_bias_b, d_skip_b, norm_b, w_out, norm_ffn,
           w_router_group, b_router_group, w_router_expert, b_router_expert, w_gate_e, w_up_e, w_down_e,
           norm_final):
    depth = w_in.shape[0]
    assert depth == 1, "the fused final norm assumes a single layer"
    bp, lp, d = x_prompt.shape
    bs, ls, _ = x_sample.shape
    assert lp % PROMPT_CHUNK == 0 and lp >= CONV_W - 1
    assert PAD_FRONT + ls <= SEQ_ROWS and bs % (CHUNK // SEQ_ROWS) == 0 and (CHUNK // SEQ_ROWS * ls) % 8 == 0
    l = 0
    p = _prep_layer(l, norm_mix, w_in, conv_a_w, a_log_a, dt_bias_a, norm_a, conv_b_w, conv_b_b, a_log_b,
                    dt_bias_b, d_skip_b, norm_b, w_out, norm_ffn, w_router_group, b_router_group,
                    w_router_expert, b_router_expert, w_gate_e, w_up_e, w_down_e, norm_final)

    xp2 = x_prompt.reshape(bp * lp, d)
    tm_p = _pick_tile(lp, 256)
    steps = bp * lp // tm_p
    in_steps = [_can_cast_in_steps(a, steps) for a in p["ffn_f32"]]
    proj_p, tails, cast = _inproj_conv(xp2, p, tm_p, lp, [a for a, ok in zip(p["ffn_f32"], in_steps) if ok])
    _set_ffn_weights(p, [cast.pop(0) if ok else _bf(a) for a, ok in zip(p["ffn_f32"], in_steps)])
    mix_p, delta_p, ssm_p = _mixer_prompt(proj_p.reshape(bp, lp, D_PROJ), p)
    tails = tails.reshape(bp, lp // tm_p, 8, D_CONV)[:, -1, 8 - (CONV_W - 1):]
    dconv_p = tails[:, :, :D_QKV_A]
    sconv_p = tails[:, :, D_QKV_A:]
    y_p = _ffn_tail(mix_p.reshape(bp * lp, -1), xp2, p).reshape(bp, lp, d)

    xs2 = x_sample.reshape(bs * ls, d)
    proj_s = _inproj(xs2, p, _pick_tile(bs * ls, 256))
    dconv_in = state_delta_conv[l].astype(F32)
    sconv_in = state_ssm_conv[l].astype(F32)
    mix_s, delta_s, ssm_s = _mixer_decode(proj_s, dconv_in, sconv_in, state_delta[l].astype(F32),
                                          state_ssm[l].astype(F32).reshape(bs, D_B, N_B), p)
    mix_s = mix_s.reshape(bs, SEQ_ROWS, -1)[:, PAD_FRONT:PAD_FRONT + ls].reshape(bs * ls, -1)
    dconv_s = jnp.concatenate([dconv_in, proj_s[:, OFF_QKV:OFF_QKV + D_QKV_A].reshape(bs, ls, D_QKV_A)], axis=1)[:, ls:]
    sconv_s = jnp.concatenate([sconv_in, proj_s[:, OFF_XBC:OFF_XBC + D_XBC].reshape(bs, ls, D_XBC)], axis=1)[:, ls:]
    y_s = _ffn_tail(mix_s, xs2, p).reshape(bs, ls, d)

    return (y_p.astype(x_prompt.dtype), y_s.astype(x_sample.dtype),
            delta_p[None], dconv_p[None], ssm_p.reshape(bp, H_B, P_B, N_B)[None], sconv_p[None],
            delta_s[None], dconv_s[None], ssm_s.reshape(bs, H_B, P_B, N_B)[None], sconv_s[None])
```

```python
import functools

import numpy as np
import jax
import jax.numpy as jnp
from jax import lax
from jax.experimental import pallas as pl
from jax.experimental.pallas import tpu as pltpu

F32 = jnp.float32
BF16 = jnp.bfloat16

D_MODEL = 1024
H_A, DK_A, DV_A = 8, 128, 128
D_QK_A = H_A * DK_A
D_V_A = H_A * DV_A
D_QKV_A = 2 * D_QK_A + D_V_A
H_B, P_B, N_B, G_B = 16, 64, 128, 2
D_B = H_B * P_B
D_XBC = D_B + 2 * G_B * N_B
CONV_W = 4
N_GROUPS_E, EXPERTS_PER_GROUP = 4, 4
N_EXPERTS = N_GROUPS_E * EXPERTS_PER_GROUP
D_EXPERT = 512
EPS = 1e-6

OFF_QKV = 0
OFF_GATE = OFF_QKV + D_QKV_A
OFF_Z = OFF_GATE + D_V_A
OFF_XBC = OFF_Z + D_B
OFF_SMALL = OFF_XBC + D_XBC
D_PROJ = OFF_SMALL + 128
D_CONV = D_QKV_A + D_XBC
CONV_RANGES = ((OFF_QKV, D_QKV_A), (OFF_XBC, D_XBC))
LANE_G = H_A
LANE_DT = 2 * H_A

CHUNK = 64
PROMPT_CHUNK = 128
SEQ_ROWS = 8
PROMPT_STREAMS = 2
PAD_FRONT = CONV_W - 1
INPROJ_SPREAD = 1
INPROJ_COLS = 256
MOE_BLOCK = 1024
MOE_SUB = 256
MOE_ALIGN = 64
VMEM_LIMIT = 52 * 1024 * 1024


def _bf(x):
    return x.astype(BF16)


def _mm(a, b):
    return jnp.dot(_bf(a), _bf(b), preferred_element_type=F32)


def _mm_nt(a, b):
    return lax.dot_general(_bf(a), _bf(b), (((1,), (1,)), ((), ())), preferred_element_type=F32)


def _mm_tn(a, b):
    return lax.dot_general(_bf(a), _bf(b), (((0,), (0,)), ((), ())), preferred_element_type=F32)


def _split3(x):
    x1 = _bf(x)
    r1 = x - x1.astype(F32)
    x2 = _bf(r1)
    x3 = _bf(r1 - x2.astype(F32))
    return x1, x2, x3


def _mm01_r(x, m01):
    x1, x2, x3 = _split3(x)
    d = functools.partial(jnp.dot, preferred_element_type=F32)
    return d(x1, m01) + d(x2, m01) + d(x3, m01)


def _mm01_r2(x, m01):
    x1 = _bf(x)
    x2 = _bf(x - x1.astype(F32))
    d = functools.partial(jnp.dot, preferred_element_type=F32)
    return d(x1, m01) + d(x2, m01)


def _mm01_l(m01, x):
    x1, x2, x3 = _split3(x)
    d = functools.partial(jnp.dot, preferred_element_type=F32)
    return d(m01, x1) + d(m01, x2) + d(m01, x3)


def _rms(x, w):
    return x * lax.rsqrt(jnp.mean(x * x, axis=-1, keepdims=True) + EPS) * w


def _silu(x):
    return x * jax.nn.sigmoid(x)


def _softplus(x):
    return jnp.maximum(x, 0.0) + jnp.log1p(jnp.exp(-jnp.abs(x)))


def _weight_blocks(width):
    return [slice(c0, min(c0 + width, D_PROJ)) for c0 in range(0, D_PROJ, width)]


def _is_conv_block(dst):
    return any(lo <= dst.start and dst.stop <= lo + n for lo, n in CONV_RANGES)


def _prep_w_in_kernel(w_ref, o_ref):
    c_beta = D_QKV_A + D_V_A
    c_z = c_beta + 2 * H_A
    c_dt = c_z + D_B + D_XBC
    o_ref[:, OFF_QKV:OFF_Z] = _bf(w_ref[:, 0:c_beta])
    o_ref[:, OFF_Z:OFF_SMALL] = _bf(w_ref[:, c_z:c_dt])
    small = jnp.concatenate([w_ref[:, c_beta:c_z], w_ref[:, c_dt:c_dt + H_B],
                             jnp.zeros((w_ref.shape[0], 128 - 2 * H_A - H_B), F32)], axis=1)
    o_ref[:, OFF_SMALL:D_PROJ] = _bf(small)


def _prep_w_in(w, l):
    _, d, n = w.shape
    rows = 128
    return pl.pallas_call(
        _prep_w_in_kernel,
        grid=(d // rows,),
        in_specs=[pl.BlockSpec((None, rows, n), lambda i: (l, i, 0))],
        out_specs=pl.BlockSpec((rows, D_PROJ), lambda i: (i, 0)),
        out_shape=jax.ShapeDtypeStruct((d, D_PROJ), BF16),
        compiler_params=pltpu.CompilerParams(dimension_semantics=("parallel",)),
        name="prep_w_in",
    )(w.astype(F32))


def _inproj_kernel(x_ref, nw_ref, w_ref, o_ref):
    h = _bf(_rms(x_ref[...], nw_ref[...]))
    o_ref[...] = jnp.dot(h, w_ref[...], preferred_element_type=F32)


def _weight_specs(p):
    return [pl.BlockSpec(p["w_all"].shape, lambda i: (0, 0), pipeline_mode=pl.Buffered(1))]


def _inproj(x2d, p, tm):
    t, d = x2d.shape
    return pl.pallas_call(
        _inproj_kernel,
        grid=(t // tm,),
        in_specs=[pl.BlockSpec((tm, d), lambda i: (i, 0)), pl.BlockSpec((1, d), lambda i: (0, 0))] + _weight_specs(p),
        out_specs=pl.BlockSpec((tm, D_PROJ), lambda i: (i, 0)),
        out_shape=jax.ShapeDtypeStruct((t, D_PROJ), F32),
        compiler_params=pltpu.CompilerParams(dimension_semantics=("parallel",), vmem_limit_bytes=VMEM_LIMIT),
        name="inproj",
    )(x2d, p["norm_mix"], p["w_all"])


def _conv_silu(u, prev8, cw_ref, cb_ref, sl):
    row8 = lax.broadcasted_iota(jnp.int32, (8, 1), 0)
    acc = u * cw_ref[CONV_W - 1:CONV_W, sl] + cb_ref[:, sl]
    for k in range(1, CONV_W):
        ru = pltpu.roll(u, k, 0)
        if prev8 is not None:
            head = jnp.where(row8 < k, pltpu.roll(prev8, k, 0), ru[0:8])
            ru = jnp.concatenate([head, ru[8:]], axis=0)
        acc = acc + ru * cw_ref[CONV_W - 1 - k:CONV_W - k, sl]
    return _silu(acc)


def _inproj_conv_kernel(tiles_per_seq, n_cast, x_ref, nw_ref, w_ref, cw_ref, cb_ref, *rest):
    cast_in = rest[:n_cast]
    o_ref, tail_ref = rest[n_cast:n_cast + 2]
    cast_out = rest[n_cast + 2:2 * n_cast + 2]
    prev_ref = rest[2 * n_cast + 2]
    for src, dst in zip(cast_in, cast_out):
        dst[...] = _bf(src[...])

    @pl.when(pl.program_id(0) % tiles_per_seq == 0)
    def _():
        prev_ref[...] = jnp.zeros_like(prev_ref)

    h = _bf(_rms(x_ref[...], nw_ref[...]))
    tm = h.shape[0]
    blocks = _weight_blocks(INPROJ_COLS)
    conv_blocks = [b for b in blocks if _is_conv_block(b)]
    plain_blocks = [b for b in blocks if not _is_conv_block(b)]
    blocks = []
    for j, b in enumerate(conv_blocks):
        blocks.append(b)
        if j % INPROJ_SPREAD == INPROJ_SPREAD - 1 and plain_blocks:
            blocks.append(plain_blocks.pop(0))
    blocks += plain_blocks
    for dst in blocks:
        u = jnp.dot(h, w_ref[:, dst], preferred_element_type=F32)
        if _is_conv_block(dst):
            o_ref[:, dst] = _conv_silu(u, prev_ref[:, dst], cw_ref, cb_ref, dst)
            prev_ref[:, dst] = u[tm - 8:tm]
            t0 = dst.start - OFF_QKV if dst.start < OFF_XBC else dst.start - OFF_XBC + D_QKV_A
            tail_ref[:, t0:t0 + (dst.stop - dst.start)] = u[tm - 8:tm]
        else:
            o_ref[:, dst] = u


def _can_cast_in_steps(a, steps):
    return a.shape[0] % steps == 0 and (a.shape[0] // steps) % 16 == 0


def _inproj_conv(x2d, p, tm, seq_len, to_cast):
    t, d = x2d.shape
    assert seq_len % tm == 0
    steps = t // tm
    cast_specs = [pl.BlockSpec((a.shape[0] // steps, a.shape[1]), lambda i: (i, 0)) for a in to_cast]
    res = pl.pallas_call(
        functools.partial(_inproj_conv_kernel, seq_len // tm, len(to_cast)),
        grid=(steps,),
        in_specs=[pl.BlockSpec((tm, d), lambda i: (i, 0)), pl.BlockSpec((1, d), lambda i: (0, 0))] + _weight_specs(p)
        + [_const_spec(p["conv_w"].shape), _const_spec(p["conv_b"].shape)] + cast_specs,
        out_specs=[pl.BlockSpec((tm, D_PROJ), lambda i: (i, 0)), pl.BlockSpec((None, 8, D_CONV), lambda i: (i, 0, 0))]
        + cast_specs,
        out_shape=[jax.ShapeDtypeStruct((t, D_PROJ), F32), jax.ShapeDtypeStruct((steps, 8, D_CONV), F32)]
        + [jax.ShapeDtypeStruct(a.shape, BF16) for a in to_cast],
        scratch_shapes=[pltpu.VMEM((8, D_PROJ), F32)],
        compiler_params=pltpu.CompilerParams(dimension_semantics=("arbitrary",), vmem_limit_bytes=VMEM_LIMIT),
        name="inproj_conv",
    )(x2d, p["norm_mix"], p["w_all"], p["conv_w"], p["conv_b"], *to_cast)
    return res[0], res[1], list(res[2:])


def _inv_unit_lower(mats, ri, ci, eye, merge_sizes):
    blk = (ri >> 3) == (ci >> 3)
    ads = [jnp.where(blk, a, 0.0) for a in mats]
    xs = [eye - ad for ad in ads]
    n = eye.shape[0]
    a2s = [_mm(ad, ad) for ad in ads]
    st = [_mm(jnp.concatenate([x, a2], axis=0), a2) for x, a2 in zip(xs, a2s)]
    xs = [x + t[:n] for x, t in zip(xs, st)]
    xs = [x + _mm(x, t[n:]) for x, t in zip(xs, st)]
    for s in merge_sizes:
        sh = s.bit_length() - 1
        rb = ri >> sh
        sel = ((rb & 1) == 1) & ((ci >> sh) == rb - 1)
        odd = [slice(r, r + s) for r in range(s, n, 2 * s)]
        even = [slice(r, r + s) for r in range(0, n, 2 * s)]
        xo = [jnp.concatenate([x[sl] for sl in odd], axis=0) for x in xs]
        ts = [_mm(o, jnp.where(sel, a, 0.0)) for o, a in zip(xo, mats)]
        xo = [o - _mm(t, x) for o, t, x in zip(xo, ts, xs)]
        xs = [jnp.concatenate([blk for i, ev in enumerate(even) for blk in (x[ev], o[i * s:(i + 1) * s])], axis=0)
              for x, o in zip(xs, xo)]
    return xs


def _mixer_kernel(c, n_streams, n_seq, valid_lo, valid_hi, *refs):
    carry = n_seq == 1
    (proj_ref, cw_ref, cb_ref, bias_ref, alog_ref, na_ref, nb_ref, dsk_ref,
     ea_ref, eb_ref, ec_ref) = refs[:11]
    if carry:
        mix_ref, s_wr, h_wr = refs[11:]
        s_rd, h_rd = s_wr, h_wr
        xc_ref = proj_ref

        @pl.when(pl.program_id(1) == 0)
        def _():
            s_wr[...] = jnp.zeros_like(s_wr)
            h_wr[...] = jnp.zeros_like(h_wr)
    else:
        dconv_ref, sconv_ref, s_rd, h_rd, mix_ref, s_wr, h_wr, xc_ref, tile_ref = refs[11:]
        tok_ref = proj_ref
        l_tok = tok_ref.shape[0] // n_seq
        tile_ref[...] = jnp.zeros_like(tile_ref)
        for s in range(n_seq):
            r0 = (c // n_seq) * s
            tile_ref[0, r0:r0 + CONV_W - 1, OFF_QKV:OFF_QKV + D_QKV_A] = dconv_ref[s]
            tile_ref[0, r0:r0 + CONV_W - 1, OFF_XBC:OFF_XBC + D_XBC] = sconv_ref[s]
            tile_ref[0, r0 + valid_lo:r0 + valid_lo + l_tok, :] = tok_ref[l_tok * s:l_tok * (s + 1), :]
        proj_ref = tile_ref

    r_seq = c // n_seq
    sh_seq = r_seq.bit_length() - 1
    row = lax.broadcasted_iota(jnp.int32, (c, 1), 0)
    lane = lax.broadcasted_iota(jnp.int32, (1, 128), 1)
    ri = lax.broadcasted_iota(jnp.int32, (c, c), 0)
    ci = lax.broadcasted_iota(jnp.int32, (c, c), 1)
    eye = (ri == ci).astype(F32)
    seq_of_row = row >> sh_seq
    if carry:
        same = None
        causal = ri >= ci
        strict = ri > ci
        valid = None
        merge_sizes = tuple(8 << i for i in range((c // 8).bit_length() - 1))
    else:
        same = (ri >> sh_seq) == (ci >> sh_seq)
        causal = (ri >= ci) & same
        strict = (ri > ci) & same
        rr = row & (r_seq - 1)
        valid = ((rr >= valid_lo) & (rr < valid_hi)).astype(F32)
        merge_sizes = ()
    causal_bf = _bf(causal.astype(F32))
    is_beta = lane < LANE_G
    is_g = (lane >= LANE_G) & (lane < LANE_DT)
    is_dt = (lane >= LANE_DT) & (lane < LANE_DT + H_B)
    streams = range(n_streams)

    def conv(st):
        for lo, n in CONV_RANGES:
            for c0 in range(lo, lo + n, 512):
                sl = slice(c0, c0 + 512)
                xc_ref[st, :, sl] = _conv_silu(proj_ref[st, :, sl], None, cw_ref, cb_ref, sl)

    def gates(st):
        raw = proj_ref[st, :, OFF_SMALL:OFF_SMALL + 128]
        sp = _softplus(raw + bias_ref[...])
        coef = -jnp.exp(alog_ref[...])
        gd = jnp.where(is_g | is_dt, sp * coef, 0.0)
        q1 = jnp.where(is_beta, jax.nn.sigmoid(raw), jnp.where(is_dt, sp, 0.0))
        if valid is not None:
            gd = gd * valid
            q1 = q1 * valid
        gc = _mm01_l(causal_bf, gd)
        if carry:
            tot = jnp.broadcast_to(gc[c - 1:c, :], (c, 128))
        else:
            tot = _mm01_l(_bf(same.astype(F32)), gd)
        eg = jnp.exp(gc)
        et = jnp.exp(tot - gc)
        xa = _mm01_r2(jnp.concatenate([q1, jnp.where(is_g, eg, 0.0), jnp.where(is_g, et, 0.0)], axis=0),
                      ea_ref[...])
        xb = _mm01_r2(jnp.concatenate([q1, eg, et], axis=0), eb_ref[...])
        etot = jnp.exp(tot[0:8] if carry else tot)
        return {
            "gc": gc, "gct": gc.T,
            "beta_x": xa[0:c], "egc_x": xa[c:2 * c], "tail_x": xa[2 * c:3 * c],
            "dt_x": xb[0:c], "edac_x": xb[c:2 * c], "tailb_x": xb[2 * c:3 * c],
            "dec_x": _mm01_r(etot, ec_ref[...]),
        }

    if not carry:
        for st in streams:
            conv(st)
    gts = [gates(st) for st in streams]

    pairs = [(st, h) for st in streams for h in range(H_A)]
    hsl = [slice(128 * h, 128 * (h + 1)) for h in range(H_A)]
    qs, ks, vs = [], [], []
    for st, h in pairs:
        q = xc_ref[st, :, OFF_QKV + 128 * h:OFF_QKV + 128 * (h + 1)]
        k = xc_ref[st, :, OFF_QKV + D_QK_A + 128 * h:OFF_QKV + D_QK_A + 128 * (h + 1)]
        qs.append(q * lax.rsqrt(jnp.sum(q * q, axis=-1, keepdims=True) + EPS) * (DK_A ** -0.5))
        ks.append(k * lax.rsqrt(jnp.sum(k * k, axis=-1, keepdims=True) + EPS))
        vs.append(xc_ref[st, :, OFF_QKV + 2 * D_QK_A + 128 * h:OFF_QKV + 2 * D_QK_A + 128 * (h + 1)])
    n_p = len(pairs)
    betas = [gts[st]["beta_x"][:, hsl[h]] for st, h in pairs]
    egcs = [gts[st]["egc_x"][:, hsl[h]] for st, h in pairs]
    kbs = [ks[i] * betas[i] for i in range(n_p)]
    nts = [_mm_nt(jnp.concatenate([kbs[i], qs[i]], axis=0), ks[i]) for i in range(n_p)]
    decs = [jnp.exp(jnp.where(causal, gts[st]["gc"][:, LANE_G + h:LANE_G + h + 1]
                              - gts[st]["gct"][LANE_G + h:LANE_G + h + 1, :], -jnp.inf)) for st, h in pairs]
    a_s = [jnp.where(strict, nts[i][:c] * decs[i], 0.0) for i in range(n_p)]
    qks = [nts[i][c:] * decs[i] for i in range(n_p)]
    x_invs = _inv_unit_lower(a_s, ri, ci, eye, merge_sizes)
    sols = [_mm(x_invs[i], jnp.concatenate([vs[i] * betas[i], kbs[i] * egcs[i]], axis=1)) for i in range(n_p)]
    qes = [qs[i] * egcs[i] for i in range(n_p)]
    if n_seq == 1:
        vnews = [sols[i][:, :128] - _mm(sols[i][:, 128:], s_rd[st, h]) for i, (st, h) in enumerate(pairs)]
        os_ = [_mm(jnp.concatenate([qes[i], qks[i]], axis=1),
                   jnp.concatenate([s_rd[st, h], vnews[i]], axis=0)) for i, (st, h) in enumerate(pairs)]
    else:
        wss = [[_mm(jnp.concatenate([sols[i][r_seq * s:r_seq * (s + 1), 128:], qes[i][r_seq * s:r_seq * (s + 1)]],
                                    axis=0), s_rd[st * n_seq + s, h]) for s in range(n_seq)]
               for i, (st, h) in enumerate(pairs)]
        vnews = [jnp.concatenate([sols[i][r_seq * s:r_seq * (s + 1), :128] - wss[i][s][:r_seq]
                                  for s in range(n_seq)], axis=0) for i in range(n_p)]
        os_ = [jnp.concatenate([wss[i][s][r_seq:] for s in range(n_seq)], axis=0) + _mm(qks[i], vnews[i])
               for i in range(n_p)]
    for i, (st, h) in enumerate(pairs):
        kt = ks[i] * gts[st]["tail_x"][:, hsl[h]]
        for s in range(n_seq):
            kts = kt if n_seq == 1 else jnp.where(seq_of_row == s, kt, 0.0)
            sq = st * n_seq + s
            s_wr[sq, h] = s_rd[sq, h] * gts[st]["dec_x"][r_seq * s:r_seq * s + 1, hsl[h]] + _mm_tn(kts, vnews[i])
    for i, (st, h) in enumerate(pairs):
        gate = proj_ref[st, :, OFF_GATE + 128 * h:OFF_GATE + 128 * (h + 1)]
        mix_ref[st, :, hsl[h]] = _bf(_rms(os_[i], na_ref[:, hsl[h]]) * _silu(gate))

    hg = H_B // G_B
    wg = hg * P_B
    off_x = OFF_XBC
    off_b = off_x + D_B
    off_c = off_b + G_B * N_B
    for st, g in [(st, g) for st in streams for g in range(G_B)]:
        gt = gts[st]
        gc, gct = gt["gc"], gt["gct"]
        gs = slice(wg * g, wg * (g + 1))
        bg = xc_ref[st, :, off_b + N_B * g:off_b + N_B * (g + 1)]
        cg = xc_ref[st, :, off_c + N_B * g:off_c + N_B * (g + 1)]
        xs_g = xc_ref[st, :, off_x + wg * g:off_x + wg * (g + 1)]
        xdt = xs_g * gt["dt_x"][:, gs]
        cb = _mm_nt(cg, bg)
        ypairs = []
        for p in range(hg // 2):
            xp = xdt[:, 128 * p:128 * (p + 1)]
            ms = []
            for a_ in (0, 1):
                ln = LANE_DT + hg * g + 2 * p + a_
                seg = jnp.exp(jnp.where(causal, gc[:, ln:ln + 1] - gct[ln:ln + 1, :], -jnp.inf))
                ms.append(cb * seg)
            xpp = jnp.concatenate([jnp.where(lane < P_B, xp, 0.0), jnp.where(lane >= P_B, xp, 0.0)], axis=0)
            ypairs.append(_mm(jnp.concatenate(ms, axis=1), xpp))
        y = jnp.concatenate(ypairs, axis=1)
        yoffs = []
        for s in range(n_seq):
            rows = slice(r_seq * s, r_seq * (s + 1))
            yoffs.append(_mm_nt(cg[rows], h_rd[st * n_seq + s, wg * g:wg * (g + 1), :]))
        yoff = yoffs[0] if n_seq == 1 else jnp.concatenate(yoffs, axis=0)
        y = y + yoff * gt["edac_x"][:, gs] + dsk_ref[:, gs] * xs_g
        xt = xdt * gt["tailb_x"][:, gs]
        for s in range(n_seq):
            xts = xt if n_seq == 1 else jnp.where(seq_of_row == s, xt, 0.0)
            stt = _mm_tn(xts, bg)
            sq = st * n_seq + s
            for j in range(hg):
                hh = hg * g + j
                rs = slice(P_B * hh, P_B * (hh + 1))
                drow = gt["dec_x"][r_seq * s:r_seq * s + 1, 128 * (H_A + hh):128 * (H_A + hh + 1)]
                h_wr[sq, rs, :] = h_rd[sq, rs, :] * drow + stt[P_B * j:P_B * (j + 1), :]
        z = proj_ref[st, :, OFF_Z + wg * g:OFF_Z + wg * (g + 1)]
        mix_ref[st, :, D_V_A + wg * g:D_V_A + wg * (g + 1)] = _bf(_rms(y * _silu(z), nb_ref[:, gs]))


def _expand_mats():
    ea = np.zeros((128, 128 * H_A), np.float32)
    for h in range(H_A):
        ea[h, 128 * h:128 * (h + 1)] = 1.0
        ea[LANE_G + h, 128 * h:128 * (h + 1)] = 1.0
    eb = np.zeros((128, D_B), np.float32)
    for h in range(H_B):
        eb[LANE_DT + h, P_B * h:P_B * (h + 1)] = 1.0
    ec = np.zeros((128, 128 * (H_A + H_B)), np.float32)
    for h in range(H_A):
        ec[LANE_G + h, 128 * h:128 * (h + 1)] = 1.0
    for h in range(H_B):
        ec[LANE_DT + h, 128 * (H_A + h):128 * (H_A + h + 1)] = 1.0
    return jnp.asarray(ea, BF16), jnp.asarray(eb, BF16), jnp.asarray(ec, BF16)


def _const_spec(shape):
    return pl.BlockSpec(shape, lambda *_: (0,) * len(shape))


def _mixer_params(p):
    ea, eb, ec = _expand_mats()
    return (p["conv_w"], p["conv_b"], p["bias_slab"], p["alog_slab"], p["norm_a_x"], p["norm_b"], p["dskip_x"],
            ea, eb, ec)


def _mixer_prompt(proj, p):
    b, l, n = proj.shape
    nb = PROMPT_STREAMS if b % PROMPT_STREAMS == 0 else 1
    consts = _mixer_params(p)
    const_specs = [_const_spec(a.shape) for a in consts]
    return pl.pallas_call(
        functools.partial(_mixer_kernel, PROMPT_CHUNK, nb, 1, 0, PROMPT_CHUNK),
        grid=(b // nb, l // PROMPT_CHUNK),
        in_specs=[pl.BlockSpec((nb, PROMPT_CHUNK, n), lambda i, t: (i, t, 0))] + const_specs,
        out_specs=[
            pl.BlockSpec((nb, PROMPT_CHUNK, D_V_A + D_B), lambda i, t: (i, t, 0)),
            pl.BlockSpec((nb, H_A, DK_A, DV_A), lambda i, t: (i, 0, 0, 0)),
            pl.BlockSpec((nb, D_B, N_B), lambda i, t: (i, 0, 0)),
        ],
        out_shape=[
            jax.ShapeDtypeStruct((b, l, D_V_A + D_B), BF16),
            jax.ShapeDtypeStruct((b, H_A, DK_A, DV_A), F32),
            jax.ShapeDtypeStruct((b, D_B, N_B), F32),
        ],
        compiler_params=pltpu.CompilerParams(dimension_semantics=("parallel", "arbitrary"),
                                             vmem_limit_bytes=VMEM_LIMIT),
        name="mixer_prompt",
    )(proj, *consts)


def _mixer_decode(proj, dconv, sconv, s_delta, s_ssm, p):
    bs = s_delta.shape[0]
    n = proj.shape[1]
    l_dec = proj.shape[0] // bs
    n_seq = CHUNK // SEQ_ROWS
    tiles = bs // n_seq
    consts = _mixer_params(p)
    const_specs = [_const_spec(a.shape) for a in consts]
    return pl.pallas_call(
        functools.partial(_mixer_kernel, CHUNK, 1, n_seq, PAD_FRONT, PAD_FRONT + l_dec),
        grid=(tiles,),
        in_specs=[pl.BlockSpec((n_seq * l_dec, n), lambda i: (i, 0))] + const_specs + [
            pl.BlockSpec((n_seq, CONV_W - 1, D_QKV_A), lambda i: (i, 0, 0)),
            pl.BlockSpec((n_seq, CONV_W - 1, D_XBC), lambda i: (i, 0, 0)),
            pl.BlockSpec((n_seq, H_A, DK_A, DV_A), lambda i: (i, 0, 0, 0)),
            pl.BlockSpec((n_seq, D_B, N_B), lambda i: (i, 0, 0)),
        ],
        out_specs=[
            pl.BlockSpec((1, CHUNK, D_V_A + D_B), lambda i: (i, 0, 0)),
            pl.BlockSpec((n_seq, H_A, DK_A, DV_A), lambda i: (i, 0, 0, 0)),
            pl.BlockSpec((n_seq, D_B, N_B), lambda i: (i, 0, 0)),
        ],
        out_shape=[
            jax.ShapeDtypeStruct((tiles, CHUNK, D_V_A + D_B), BF16),
            jax.ShapeDtypeStruct((bs, H_A, DK_A, DV_A), F32),
            jax.ShapeDtypeStruct((bs, D_B, N_B), F32),
        ],
        scratch_shapes=[pltpu.VMEM((1, CHUNK, n), F32), pltpu.VMEM((1, CHUNK, n), F32)],
        compiler_params=pltpu.CompilerParams(dimension_semantics=("parallel",), vmem_limit_bytes=VMEM_LIMIT),
        name="mixer_decode",
    )(proj, *consts, dconv, sconv, s_delta, s_ssm)


def _outproj_router_kernel(sub, mix_ref, x_ref, wo_ref, nf_ref, wr_ref, br_ref, tril_ref, xmid_ref, route_ref,
                           meta_ref):
    xm = x_ref[...] + jnp.dot(mix_ref[...], wo_ref[...], preferred_element_type=F32)
    xmid_ref[...] = xm
    t = _rms(xm, nf_ref[...])
    t1 = _bf(t)
    t2 = _bf(t - t1.astype(F32))
    d = functools.partial(jnp.dot, preferred_element_type=F32)
    hl = d(t1, wr_ref[...])
    logit = hl[:, :128] + hl[:, 128:] + d(t2, wr_ref[:, :128]) + br_ref[...]

    lane = lax.broadcasted_iota(jnp.int32, (1, 128), 1).astype(F32)
    neg = -jnp.inf
    big = 1e9
    is_grp = lane < N_GROUPS_E
    gl = jnp.where(is_grp, logit, neg)
    gmax = jnp.max(gl, axis=-1, keepdims=True)
    gsel = jnp.min(jnp.where(gl == gmax, lane, big), axis=-1, keepdims=True)
    gw = 1.0 / jnp.sum(jnp.exp(jnp.where(is_grp, logit - gmax, neg)), axis=-1, keepdims=True)
    lo = N_GROUPS_E + EXPERTS_PER_GROUP * gsel
    el = jnp.where((lane >= lo) & (lane < lo + EXPERTS_PER_GROUP), logit, neg)
    v1 = jnp.max(el, axis=-1, keepdims=True)
    i1 = jnp.min(jnp.where(el == v1, lane, big), axis=-1, keepdims=True)
    el2 = jnp.where(lane == i1, neg, el)
    v2 = jnp.max(el2, axis=-1, keepdims=True)
    i2 = jnp.min(jnp.where(el2 == v2, lane, big), axis=-1, keepdims=True)
    e = jnp.exp(v2 - v1)
    den = gw / (1.0 + e)
    comb = jnp.where(lane == i1, den, 0.0) + jnp.where(lane == i2, e * den, 0.0)

    tm = logit.shape[0]
    onehot = jnp.where((lane == gsel) & is_grp, 1.0, 0.0)
    ranks = jnp.dot(tril_ref[...], _bf(onehot), preferred_element_type=F32)
    cnt = ranks[tm - 1:tm, :] + onehot[tm - 1:tm, :]
    ntile = jnp.floor((cnt + (sub - 1)) * (1.0 / sub))
    li = lax.broadcasted_iota(jnp.int32, (128, 128), 0)
    lj = lax.broadcasted_iota(jnp.int32, (128, 128), 1)
    off = _mm01_r(jnp.broadcast_to(ntile * sub, (8, 128)), _bf((li < lj).astype(F32)))[0:1]
    pos = jnp.sum(onehot * (off + ranks), axis=-1, keepdims=True)
    route_ref[...] = jnp.where(lane == 0.0, gsel, jnp.where(lane == 1.0, pos, comb))
    row8 = lax.broadcasted_iota(jnp.int32, (8, 128), 0)
    meta_ref[...] = jnp.where(row8 == 0, off, ntile)


def _outproj_router(mix, x2d, p, tm, sub):
    t, d = x2d.shape
    nb = t // tm
    tril = jnp.asarray(np.tri(tm, tm, -1, dtype=np.float32), BF16)
    return pl.pallas_call(
        functools.partial(_outproj_router_kernel, sub),
        grid=(nb,),
        in_specs=[
            pl.BlockSpec((tm, mix.shape[1]), lambda i: (i, 0)),
            pl.BlockSpec((tm, d), lambda i: (i, 0)),
            _const_spec(p["w_out"].shape),
            _const_spec((1, d)),
            _const_spec(p["w_router"].shape),
            _const_spec((1, 128)),
            _const_spec((tm, tm)),
        ],
        out_specs=[pl.BlockSpec((tm, d), lambda i: (i, 0)), pl.BlockSpec((tm, 128), lambda i: (i, 0)),
                   pl.BlockSpec((None, 8, 128), lambda i: (i, 0, 0))],
        out_shape=[jax.ShapeDtypeStruct((t, d), F32), jax.ShapeDtypeStruct((t, 128), F32),
                   jax.ShapeDtypeStruct((nb, 8, 128), F32)],
        compiler_params=pltpu.CompilerParams(dimension_semantics=("parallel",), vmem_limit_bytes=VMEM_LIMIT),
        name="outproj_router",
    )(mix, x2d, p["w_out"], p["norm_ffn"], p["w_router"], p["b_router"], tril)


def _moe_kernel(tb, meta_ref, pos_ref, xmid_ref, route_ref, nffn_ref, nfin_ref, wg_ref, wu_ref, wd_ref,
                y_ref, xg_scr, rg_scr):
    b = pl.program_id(0)
    g = pl.program_id(1)
    unroll = 8

    @pl.when(g == 0)
    def _():
        y_ref[...] = _rms(xmid_ref[...], nffn_ref[...])
        for gg in range(N_GROUPS_E):
            nq_g = meta_ref[b, N_GROUPS_E + gg]

            @pl.when(nq_g > 0)
            def _(gg=gg, nq_g=nq_g):
                r = pl.multiple_of(meta_ref[b, gg] + (nq_g - 1) * MOE_ALIGN, MOE_ALIGN)
                xg_scr[pl.ds(r, MOE_ALIGN), :] = jnp.zeros((MOE_ALIGN, D_MODEL), F32)
                rg_scr[pl.ds(r, MOE_ALIGN), :] = jnp.zeros((MOE_ALIGN, 128), F32)

        def dispatch(i, carry):
            base = pl.multiple_of(i * unroll, unroll)
            src = y_ref.at[pl.ds(base, unroll), :]
            rsrc = route_ref.at[pl.ds(base, unroll), :]
            for u in range(unroll):
                p = pos_ref[0, base + u]
                xg_scr[pl.ds(p, 1), :] = src[u:u + 1, :]
                rg_scr[pl.ds(p, 1), :] = rsrc[u:u + 1, :]
            return carry

        lax.fori_loop(0, tb // unroll, dispatch, 0)

    off = meta_ref[b, g]
    nq = meta_ref[b, N_GROUPS_E + g]
    lane = lax.broadcasted_iota(jnp.int32, (1, 128), 1)
    d = functools.partial(jnp.dot, preferred_element_type=F32)

    def tile(r0, m):
        r0 = pl.multiple_of(r0, MOE_ALIGN)
        x = _bf(xg_scr[pl.ds(r0, m), :])
        rg = rg_scr[pl.ds(r0, m), :]
        contrib = None
        for j in range(EXPERTS_PER_GROUP):
            e16 = g * EXPERTS_PER_GROUP + j
            cw = jnp.sum(jnp.where(lane == N_GROUPS_E + e16, rg, 0.0), axis=-1, keepdims=True)
            hid = _silu(d(x, wg_ref[j])) * d(x, wu_ref[j]) * cw
            cj = d(_bf(hid), wd_ref[j])
            contrib = cj if contrib is None else contrib + cj

        xg_scr[pl.ds(r0, m), :] = contrib

    per_tile = MOE_SUB // MOE_ALIGN
    nloop = jnp.maximum(nq // per_tile - 1, 0)

    def full_tile(j, carry):
        tile(off + j * MOE_SUB, MOE_SUB)
        return carry

    lax.fori_loop(0, nloop, full_tile, 0)
    last = nq - nloop * per_tile
    r_last = off + nloop * MOE_SUB
    for q in range(1, 2 * per_tile):
        @pl.when(last == q)
        def _(q=q):
            tile(r_last, q * MOE_ALIGN)

    @pl.when(g == N_GROUPS_E - 1)
    def _():
        def combine(i, carry):
            base = pl.multiple_of(i * unroll, unroll)
            dst = y_ref.at[pl.ds(base, unroll), :]
            for u in range(unroll):
                p = pos_ref[0, base + u]
                dst[u:u + 1, :] = xg_scr[pl.ds(p, 1), :]
            return carry

        lax.fori_loop(0, tb // unroll, combine, 0)
        y_ref[...] = _rms(xmid_ref[...] + y_ref[...], nfin_ref[...])


def _moe(xmid, route, meta, p, tb):
    t, d = xmid.shape
    nb = t // tb
    rows = tb + N_GROUPS_E * MOE_ALIGN
    pos = route[:, 1].astype(jnp.int32).reshape(nb, 1, tb)
    meta_i = jnp.concatenate([meta[:, 0, :N_GROUPS_E], meta[:, 1, :N_GROUPS_E]], axis=1).astype(jnp.int32)
    grid_spec = pltpu.PrefetchScalarGridSpec(
        num_scalar_prefetch=1,
        grid=(nb, N_GROUPS_E),
        in_specs=[
            pl.BlockSpec((None, 1, tb), lambda i, e, m: (i, 0, 0), memory_space=pltpu.SMEM),
            pl.BlockSpec((tb, d), lambda i, e, m: (i, 0)),
            pl.BlockSpec((tb, 128), lambda i, e, m: (i, 0)),
            pl.BlockSpec((1, d), lambda i, e, m: (0, 0)),
            pl.BlockSpec((1, d), lambda i, e, m: (0, 0)),
            pl.BlockSpec((EXPERTS_PER_GROUP, d, D_EXPERT), lambda i, e, m: (e, 0, 0)),
            pl.BlockSpec((EXPERTS_PER_GROUP, d, D_EXPERT), lambda i, e, m: (e, 0, 0)),
            pl.BlockSpec((EXPERTS_PER_GROUP, D_EXPERT, d), lambda i, e, m: (e, 0, 0)),
        ],
        out_specs=pl.BlockSpec((tb, d), lambda i, e, m: (i, 0)),
        scratch_shapes=[pltpu.VMEM((rows, d), F32), pltpu.VMEM((rows, 128), F32)],
    )
    return pl.pallas_call(
        functools.partial(_moe_kernel, tb),
        grid_spec=grid_spec,
        out_shape=jax.ShapeDtypeStruct((t, d), F32),
        compiler_params=pltpu.CompilerParams(dimension_semantics=("parallel", "arbitrary"),
                                             vmem_limit_bytes=VMEM_LIMIT),
        name="moe",
    )(meta_i, pos, xmid, route, p["norm_ffn"], p["norm_final"], p["w_gate"], p["w_up"], p["w_down"])


def _pick_tile(t, pref):
    tm = min(pref, t)
    while t % tm:
        tm //= 2
    return tm


def _prep_layer(l, norm_mix, w_in, conv_a_w, a_log_a, dt_bias_a, norm_a, conv_b_w, conv_b_b, a_log_b, dt_bias_b,
                d_skip_b, norm_b, w_out, norm_ffn, w_router_group, b_router_group, w_router_expert,
                b_router_expert, w_gate_e, w_up_e, w_down_e, norm_final):
    def on_conv_columns(a_qkv, a_xbc):
        out = jnp.zeros(a_qkv.shape[:-1] + (D_PROJ,), F32)
        out = out.at[..., OFF_QKV:OFF_QKV + D_QKV_A].set(a_qkv.astype(F32))
        return out.at[..., OFF_XBC:OFF_XBC + D_XBC].set(a_xbc.astype(F32))

    def slab(a8, b16):
        return jnp.concatenate([jnp.zeros((H_A,), F32), a8.astype(F32), b16.astype(F32),
                                jnp.zeros((128 - 2 * H_A - H_B,), F32)]).reshape(1, 128)

    w_router = jnp.concatenate(
        [w_router_group[l].astype(F32), w_router_expert[l].reshape(D_MODEL, N_EXPERTS).astype(F32),
         jnp.zeros((D_MODEL, 128 - N_GROUPS_E - N_EXPERTS), F32)], axis=1)
    w_router_hi = _bf(w_router)
    w_router = jnp.concatenate([w_router_hi, _bf(w_router - w_router_hi.astype(F32))], axis=1)
    b_router = jnp.concatenate(
        [b_router_group[l].astype(F32), b_router_expert[l].reshape(N_EXPERTS).astype(F32),
         jnp.zeros((128 - N_GROUPS_E - N_EXPERTS,), F32)]).reshape(1, 128)
    return {
        "norm_mix": norm_mix[l].astype(F32).reshape(1, D_MODEL),
        "w_all": _prep_w_in(w_in, l),
        "conv_w": on_conv_columns(conv_a_w[l], conv_b_w[l]),
        "conv_b": on_conv_columns(jnp.zeros((1, D_QKV_A), F32), conv_b_b[l].reshape(1, D_XBC)),
        "bias_slab": slab(dt_bias_a[l], dt_bias_b[l]),
        "alog_slab": slab(a_log_a[l], a_log_b[l]),
        "norm_a_x": jnp.tile(norm_a[l].astype(F32), H_A).reshape(1, D_V_A),
        "norm_b": norm_b[l].astype(F32).reshape(1, D_B),
        "dskip_x": jnp.repeat(d_skip_b[l].astype(F32), P_B).reshape(1, D_B),
        "norm_ffn": norm_ffn[l].astype(F32).reshape(1, D_MODEL),
        "w_router": w_router,
        "b_router": b_router,
        "norm_final": norm_final.astype(F32).reshape(1, D_MODEL),
        "ffn_f32": [w_out[l].astype(F32),
                    w_gate_e[l].astype(F32).reshape(N_EXPERTS * D_MODEL, D_EXPERT),
                    w_up_e[l].astype(F32).reshape(N_EXPERTS * D_MODEL, D_EXPERT),
                    w_down_e[l].astype(F32).reshape(N_EXPERTS * D_EXPERT, D_MODEL)],
    }


def _set_ffn_weights(p, w_bf16):
    w_out, w_gate, w_up, w_down = w_bf16
    p["w_out"] = w_out
    p["w_gate"] = w_gate.reshape(N_EXPERTS, D_MODEL, D_EXPERT)
    p["w_up"] = w_up.reshape(N_EXPERTS, D_MODEL, D_EXPERT)
    p["w_down"] = w_down.reshape(N_EXPERTS, D_EXPERT, D_MODEL)


def _ffn_tail(mix2d, x2d, p):
    t = x2d.shape[0]
    tb = _pick_tile(t, MOE_BLOCK)
    xmid, route, meta = _outproj_router(mix2d, x2d, p, tb, MOE_ALIGN)
    return _moe(xmid, route, meta, p, tb)


def kernel(x_prompt, x_sample, state_delta, state_delta_conv, state_ssm, state_ssm_conv, norm_mix, w_in, conv_a_w,
           a_log_a, dt_bias_a, norm_a, conv_b_w, conv_b_b, a_log_b, dt_bias_b, d_skip_b, norm_b, w_out, norm_ffn,
           w_router_group, b_router_group, w_router_expert, b_router_expert, w_gate_e, w_up_e, w_down_e,
           norm_final):
    depth = w_in.shape[0]
    assert depth == 1, "the fused final norm assumes a single layer"
    bp, lp, d = x_prompt.shape
    bs, ls, _ = x_sample.shape
    assert lp % PROMPT_CHUNK == 0 and lp >= CONV_W - 1
    assert PAD_FRONT + ls <= SEQ_ROWS and bs % (CHUNK // SEQ_ROWS) == 0 and (CHUNK // SEQ_ROWS * ls) % 8 == 0
    l = 0
    p = _prep_layer(l, norm_mix, w_in, conv_a_w, a_log_a, dt_bias_a, norm_a, conv_b_w, conv_b_b, a_log_b,
                    dt_bias_b, d_skip_b, norm_b, w_out, norm_ffn, w_router_group, b_router_group,
                    w_router_expert, b_router_expert, w_gate_e, w_up_e, w_down_e, norm_final)

    xp2 = x_prompt.reshape(bp * lp, d)
    tm_p = _pick_tile(lp, 256)
    steps = bp * lp // tm_p
    in_steps = [_can_cast_in_steps(a, steps) for a in p["ffn_f32"]]
    proj_p, tails, cast = _inproj_conv(xp2, p, tm_p, lp, [a for a, ok in zip(p["ffn_f32"], in_steps) if ok])
    _set_ffn_weights(p, [cast.pop(0) if ok else _bf(a) for a, ok in zip(p["ffn_f32"], in_steps)])
    mix_p, delta_p, ssm_p = _mixer_prompt(proj_p.reshape(bp, lp, D_PROJ), p)
    tails = tails.reshape(bp, lp // tm_p, 8, D_CONV)[:, -1, 8 - (CONV_W - 1):]
    dconv_p = tails[:, :, :D_QKV_A]
    sconv_p = tails[:, :, D_QKV_A:]
    y_p = _ffn_tail(mix_p.reshape(bp * lp, -1), xp2, p).reshape(bp, lp, d)

    xs2 = x_sample.reshape(bs * ls, d)
    proj_s = _inproj(xs2, p, _pick_tile(bs * ls, 256))
    dconv_in = state_delta_conv[l].astype(F32)
    sconv_in = state_ssm_conv[l].astype(F32)
    mix_s, delta_s, ssm_s = _mixer_decode(proj_s, dconv_in, sconv_in, state_delta[l].astype(F32),
                                          state_ssm[l].astype(F32).reshape(bs, D_B, N_B), p)
    mix_s = mix_s.reshape(bs, SEQ_ROWS, -1)[:, PAD_FRONT:PAD_FRONT + ls].reshape(bs * ls, -1)
    dconv_s = jnp.concatenate([dconv_in, proj_s[:, OFF_QKV:OFF_QKV + D_QKV_A].reshape(bs, ls, D_QKV_A)], axis=1)[:, ls:]
    sconv_s = jnp.concatenate([sconv_in, proj_s[:, OFF_XBC:OFF_XBC + D_XBC].reshape(bs, ls, D_XBC)], axis=1)[:, ls:]
    y_s = _ffn_tail(mix_s, xs2, p).reshape(bs, ls, d)

    return (y_p.astype(x_prompt.dtype), y_s.astype(x_sample.dtype),
            delta_p[None], dconv_p[None], ssm_p.reshape(bp, H_B, P_B, N_B)[None], sconv_p[None],
            delta_s[None], dconv_s[None], ssm_s.reshape(bs, H_B, P_B, N_B)[None], sconv_s[None])
```

```python
import functools

import numpy as np
import jax
import jax.numpy as jnp
from jax import lax
from jax.experimental import pallas as pl
from jax.experimental.pallas import tpu as pltpu

F32 = jnp.float32
BF16 = jnp.bfloat16

D_MODEL = 1024
H_A, DK_A, DV_A = 8, 128, 128
D_QK_A = H_A * DK_A
D_V_A = H_A * DV_A
D_QKV_A = 2 * D_QK_A + D_V_A
H_B, P_B, N_B, G_B = 16, 64, 128, 2
D_B = H_B * P_B
D_XBC = D_B + 2 * G_B * N_B
CONV_W = 4
N_GROUPS_E, EXPERTS_PER_GROUP = 4, 4
N_EXPERTS = N_GROUPS_E * EXPERTS_PER_GROUP
D_EXPERT = 512
EPS = 1e-6

OFF_QKV = 0
OFF_GATE = OFF_QKV + D_QKV_A
OFF_Z = OFF_GATE + D_V_A
OFF_XBC = OFF_Z + D_B
OFF_SMALL = OFF_XBC + D_XBC
D_PROJ = OFF_SMALL + 128
D_CONV = D_QKV_A + D_XBC
CONV_RANGES = ((OFF_QKV, D_QKV_A), (OFF_XBC, D_XBC))
LANE_G = H_A
LANE_DT = 2 * H_A

CHUNK = 64
PROMPT_CHUNK = 128
SEQ_ROWS = 8
PROMPT_STREAMS = 2
PAD_FRONT = CONV_W - 1
INPROJ_SPREAD = 1
INPROJ_COLS = 256
MOE_BLOCK = 1024
MOE_SUB = 256
MOE_ALIGN = 64
VMEM_LIMIT = 52 * 1024 * 1024


def _bf(x):
    return x.astype(BF16)


def _mm(a, b):
    return jnp.dot(_bf(a), _bf(b), preferred_element_type=F32)


def _mm_nt(a, b):
    return lax.dot_general(_bf(a), _bf(b), (((1,), (1,)), ((), ())), preferred_element_type=F32)


def _mm_tn(a, b):
    return lax.dot_general(_bf(a), _bf(b), (((0,), (0,)), ((), ())), preferred_element_type=F32)


def _split3(x):
    x1 = _bf(x)
    r1 = x - x1.astype(F32)
    x2 = _bf(r1)
    x3 = _bf(r1 - x2.astype(F32))
    return x1, x2, x3


def _mm01_r(x, m01):
    x1, x2, x3 = _split3(x)
    d = functools.partial(jnp.dot, preferred_element_type=F32)
    return d(x1, m01) + d(x2, m01) + d(x3, m01)


def _mm01_r2(x, m01):
    x1 = _bf(x)
    x2 = _bf(x - x1.astype(F32))
    d = functools.partial(jnp.dot, preferred_element_type=F32)
    return d(x1, m01) + d(x2, m01)


def _mm01_l(m01, x):
    x1, x2, x3 = _split3(x)
    d = functools.partial(jnp.dot, preferred_element_type=F32)
    return d(m01, x1) + d(m01, x2) + d(m01, x3)


def _rms(x, w):
    return x * lax.rsqrt(jnp.mean(x * x, axis=-1, keepdims=True) + EPS) * w


def _silu(x):
    return x * jax.nn.sigmoid(x)


def _softplus(x):
    return jnp.maximum(x, 0.0) + jnp.log1p(jnp.exp(-jnp.abs(x)))


def _weight_blocks(width):
    return [slice(c0, min(c0 + width, D_PROJ)) for c0 in range(0, D_PROJ, width)]


def _is_conv_block(dst):
    return any(lo <= dst.start and dst.stop <= lo + n for lo, n in CONV_RANGES)


def _prep_w_in_kernel(wt_ref, o_ref):
    c_beta = D_QKV_A + D_V_A
    c_z = c_beta + 2 * H_A
    c_dt = c_z + D_B + D_XBC
    n_small = 2 * H_A + H_B
    o_ref[:, OFF_QKV:OFF_Z] = _bf(wt_ref[0:c_beta, :].T)
    o_ref[:, OFF_Z:OFF_SMALL] = _bf(wt_ref[c_z:c_dt, :].T)
    head = wt_ref[c_beta:c_beta + 128, :].T
    tail = wt_ref[c_dt + H_B - 128:c_dt + H_B, :].T
    lane = lax.broadcasted_iota(jnp.int32, (1, 128), 1)
    small = jnp.where(lane < 2 * H_A, head, jnp.where(lane < n_small, pltpu.roll(tail, n_small, 1), 0.0))
    o_ref[:, OFF_SMALL:D_PROJ] = _bf(small)


def _prep_w_in(w, l):
    _, d, n = w.shape
    rows = 128
    return pl.pallas_call(
        _prep_w_in_kernel,
        grid=(d // rows,),
        in_specs=[pl.BlockSpec((None, n, rows), lambda i: (l, 0, i))],
        out_specs=pl.BlockSpec((rows, D_PROJ), lambda i: (i, 0)),
        out_shape=jax.ShapeDtypeStruct((d, D_PROJ), BF16),
        compiler_params=pltpu.CompilerParams(dimension_semantics=("parallel",), vmem_limit_bytes=VMEM_LIMIT),
        name="prep_w_in",
    )(jnp.swapaxes(w.astype(F32), 1, 2))


def _inproj_kernel(x_ref, nw_ref, w_ref, o_ref):
    h = _bf(_rms(x_ref[...], nw_ref[...]))
    o_ref[...] = jnp.dot(h, w_ref[...], preferred_element_type=F32)


def _weight_specs(p):
    return [pl.BlockSpec(p["w_all"].shape, lambda i: (0, 0), pipeline_mode=pl.Buffered(1))]


def _inproj(x2d, p, tm):
    t, d = x2d.shape
    return pl.pallas_call(
        _inproj_kernel,
        grid=(t // tm,),
        in_specs=[pl.BlockSpec((tm, d), lambda i: (i, 0)), pl.BlockSpec((1, d), lambda i: (0, 0))] + _weight_specs(p),
        out_specs=pl.BlockSpec((tm, D_PROJ), lambda i: (i, 0)),
        out_shape=jax.ShapeDtypeStruct((t, D_PROJ), F32),
        compiler_params=pltpu.CompilerParams(dimension_semantics=("parallel",), vmem_limit_bytes=VMEM_LIMIT),
        name="inproj",
    )(x2d, p["norm_mix"], p["w_all"])


def _conv_silu(u, prev8, cw_ref, cb_ref, sl):
    row8 = lax.broadcasted_iota(jnp.int32, (8, 1), 0)
    acc = u * cw_ref[CONV_W - 1:CONV_W, sl] + cb_ref[:, sl]
    for k in range(1, CONV_W):
        ru = pltpu.roll(u, k, 0)
        if prev8 is not None:
            head = jnp.where(row8 < k, pltpu.roll(prev8, k, 0), ru[0:8])
            ru = jnp.concatenate([head, ru[8:]], axis=0)
        acc = acc + ru * cw_ref[CONV_W - 1 - k:CONV_W - k, sl]
    return _silu(acc)


def _inproj_conv_kernel(tiles_per_seq, n_cast, x_ref, nw_ref, w_ref, cw_ref, cb_ref, *rest):
    cast_in = rest[:n_cast]
    o_ref, tail_ref = rest[n_cast:n_cast + 2]
    cast_out = rest[n_cast + 2:2 * n_cast + 2]
    prev_ref = rest[2 * n_cast + 2]
    for src, dst in zip(cast_in, cast_out):
        dst[...] = _bf(src[...])

    @pl.when(pl.program_id(0) % tiles_per_seq == 0)
    def _():
        prev_ref[...] = jnp.zeros_like(prev_ref)

    h = _bf(_rms(x_ref[...], nw_ref[...]))
    tm = h.shape[0]
    blocks = _weight_blocks(INPROJ_COLS)
    conv_blocks = [b for b in blocks if _is_conv_block(b)]
    plain_blocks = [b for b in blocks if not _is_conv_block(b)]
    blocks = []
    for j, b in enumerate(conv_blocks):
        blocks.append(b)
        if j % INPROJ_SPREAD == INPROJ_SPREAD - 1 and plain_blocks:
            blocks.append(plain_blocks.pop(0))
    blocks += plain_blocks
    for dst in blocks:
        u = jnp.dot(h, w_ref[:, dst], preferred_element_type=F32)
        if _is_conv_block(dst):
            o_ref[:, dst] = _conv_silu(u, prev_ref[:, dst], cw_ref, cb_ref, dst)
            prev_ref[:, dst] = u[tm - 8:tm]
            t0 = dst.start - OFF_QKV if dst.start < OFF_XBC else dst.start - OFF_XBC + D_QKV_A
            tail_ref[:, t0:t0 + (dst.stop - dst.start)] = u[tm - 8:tm]
        else:
            o_ref[:, dst] = u


def _can_cast_in_steps(a, steps):
    return a.shape[0] % steps == 0 and (a.shape[0] // steps) % 16 == 0


def _inproj_conv(x2d, p, tm, seq_len, to_cast):
    t, d = x2d.shape
    assert seq_len % tm == 0
    steps = t // tm
    cast_specs = [pl.BlockSpec((a.shape[0] // steps, a.shape[1]), lambda i: (i, 0)) for a in to_cast]
    res = pl.pallas_call(
        functools.partial(_inproj_conv_kernel, seq_len // tm, len(to_cast)),
        grid=(steps,),
        in_specs=[pl.BlockSpec((tm, d), lambda i: (i, 0)), pl.BlockSpec((1, d), lambda i: (0, 0))] + _weight_specs(p)
        + [_const_spec(p["conv_w"].shape), _const_spec(p["conv_b"].shape)] + cast_specs,
        out_specs=[pl.BlockSpec((tm, D_PROJ), lambda i: (i, 0)), pl.BlockSpec((None, 8, D_CONV), lambda i: (i, 0, 0))]
        + cast_specs,
        out_shape=[jax.ShapeDtypeStruct((t, D_PROJ), F32), jax.ShapeDtypeStruct((steps, 8, D_CONV), F32)]
        + [jax.ShapeDtypeStruct(a.shape, BF16) for a in to_cast],
        scratch_shapes=[pltpu.VMEM((8, D_PROJ), F32)],
        compiler_params=pltpu.CompilerParams(dimension_semantics=("arbitrary",), vmem_limit_bytes=VMEM_LIMIT),
        name="inproj_conv",
    )(x2d, p["norm_mix"], p["w_all"], p["conv_w"], p["conv_b"], *to_cast)
    return res[0], res[1], list(res[2:])


def _inv_unit_lower(mats, ri, ci, eye, merge_sizes):
    blk = (ri >> 3) == (ci >> 3)
    ads = [jnp.where(blk, a, 0.0) for a in mats]
    xs = [eye - ad for ad in ads]
    n = eye.shape[0]
    a2s = [_mm(ad, ad) for ad in ads]
    st = [_mm(jnp.concatenate([x, a2], axis=0), a2) for x, a2 in zip(xs, a2s)]
    xs = [x + t[:n] for x, t in zip(xs, st)]
    xs = [x + _mm(x, t[n:]) for x, t in zip(xs, st)]
    for s in merge_sizes:
        sh = s.bit_length() - 1
        rb = ri >> sh
        sel = ((rb & 1) == 1) & ((ci >> sh) == rb - 1)
        odd = [slice(r, r + s) for r in range(s, n, 2 * s)]
        even = [slice(r, r + s) for r in range(0, n, 2 * s)]
        xo = [jnp.concatenate([x[sl] for sl in odd], axis=0) for x in xs]
        ts = [_mm(o, jnp.where(sel, a, 0.0)) for o, a in zip(xo, mats)]
        xo = [o - _mm(t, x) for o, t, x in zip(xo, ts, xs)]
        xs = [jnp.concatenate([blk for i, ev in enumerate(even) for blk in (x[ev], o[i * s:(i + 1) * s])], axis=0)
              for x, o in zip(xs, xo)]
    return xs


def _mixer_kernel(c, n_streams, n_seq, valid_lo, valid_hi, *refs):
    carry = n_seq == 1
    (proj_ref, cw_ref, cb_ref, bias_ref, alog_ref, na_ref, nb_ref, dsk_ref,
     ea_ref, eb_ref, ec_ref) = refs[:11]
    if carry:
        mix_ref, s_wr, h_wr = refs[11:]
        s_rd, h_rd = s_wr, h_wr
        xc_ref = proj_ref

        @pl.when(pl.program_id(1) == 0)
        def _():
            s_wr[...] = jnp.zeros_like(s_wr)
            h_wr[...] = jnp.zeros_like(h_wr)
    else:
        dconv_ref, sconv_ref, s_rd, h_rd, mix_ref, s_wr, h_wr, dconv_wr, sconv_wr, xc_ref, tile_ref = refs[11:]
        tok_ref = proj_ref
        l_tok = tok_ref.shape[0] // n_seq
        tile_ref[...] = jnp.zeros_like(tile_ref)
        for s in range(n_seq):
            r0 = (c // n_seq) * s
            tile_ref[0, r0:r0 + CONV_W - 1, OFF_QKV:OFF_QKV + D_QKV_A] = dconv_ref[s]
            tile_ref[0, r0:r0 + CONV_W - 1, OFF_XBC:OFF_XBC + D_XBC] = sconv_ref[s]
            tile_ref[0, r0 + valid_lo:r0 + valid_lo + l_tok, :] = tok_ref[l_tok * s:l_tok * (s + 1), :]
        for s in range(n_seq):
            r1 = (c // n_seq) * s + l_tok
            dconv_wr[s] = tile_ref[0, r1:r1 + CONV_W - 1, OFF_QKV:OFF_QKV + D_QKV_A]
            sconv_wr[s] = tile_ref[0, r1:r1 + CONV_W - 1, OFF_XBC:OFF_XBC + D_XBC]
        proj_ref = tile_ref

    r_seq = c // n_seq
    sh_seq = r_seq.bit_length() - 1
    row = lax.broadcasted_iota(jnp.int32, (c, 1), 0)
    lane = lax.broadcasted_iota(jnp.int32, (1, 128), 1)
    ri = lax.broadcasted_iota(jnp.int32, (c, c), 0)
    ci = lax.broadcasted_iota(jnp.int32, (c, c), 1)
    eye = (ri == ci).astype(F32)
    seq_of_row = row >> sh_seq
    if carry:
        same = None
        causal = ri >= ci
        strict = ri > ci
        valid = None
        merge_sizes = tuple(8 << i for i in range((c // 8).bit_length() - 1))
    else:
        same = (ri >> sh_seq) == (ci >> sh_seq)
        causal = (ri >= ci) & same
        strict = (ri > ci) & same
        rr = row & (r_seq - 1)
        valid = ((rr >= valid_lo) & (rr < valid_hi)).astype(F32)
        merge_sizes = ()
    causal_bf = _bf(causal.astype(F32))
    is_beta = lane < LANE_G
    is_g = (lane >= LANE_G) & (lane < LANE_DT)
    is_dt = (lane >= LANE_DT) & (lane < LANE_DT + H_B)
    streams = range(n_streams)

    def conv(st):
        for lo, n in CONV_RANGES:
            for c0 in range(lo, lo + n, 512):
                sl = slice(c0, c0 + 512)
                xc_ref[st, :, sl] = _conv_silu(proj_ref[st, :, sl], None, cw_ref, cb_ref, sl)

    def gates(st):
        raw = proj_ref[st, :, OFF_SMALL:OFF_SMALL + 128]
        sp = _softplus(raw + bias_ref[...])
        coef = -jnp.exp(alog_ref[...])
        gd = jnp.where(is_g | is_dt, sp * coef, 0.0)
        q1 = jnp.where(is_beta, jax.nn.sigmoid(raw), jnp.where(is_dt, sp, 0.0))
        if valid is not None:
            gd = gd * valid
            q1 = q1 * valid
        gc = _mm01_l(causal_bf, gd)
        if carry:
            tot = jnp.broadcast_to(gc[c - 1:c, :], (c, 128))
        else:
            tot = _mm01_l(_bf(same.astype(F32)), gd)
        eg = jnp.exp(gc)
        et = jnp.exp(tot - gc)
        xa = _mm01_r2(jnp.concatenate([q1, jnp.where(is_g, eg, 0.0), jnp.where(is_g, et, 0.0)], axis=0),
                      ea_ref[...])
        xb = _mm01_r2(jnp.concatenate([q1, eg, et], axis=0), eb_ref[...])
        etot = jnp.exp(tot[0:8] if carry else tot)
        return {
            "gc": gc, "gct": gc.T,
            "beta_x": xa[0:c], "egc_x": xa[c:2 * c], "tail_x": xa[2 * c:3 * c],
            "dt_x": xb[0:c], "edac_x": xb[c:2 * c], "tailb_x": xb[2 * c:3 * c],
            "dec_x": _mm01_r(etot, ec_ref[...]),
        }

    if not carry:
        for st in streams:
            conv(st)
    gts = [gates(st) for st in streams]

    pairs = [(st, h) for st in streams for h in range(H_A)]
    hsl = [slice(128 * h, 128 * (h + 1)) for h in range(H_A)]
    qs, ks, vs = [], [], []
    for st, h in pairs:
        q = xc_ref[st, :, OFF_QKV + 128 * h:OFF_QKV + 128 * (h + 1)]
        k = xc_ref[st, :, OFF_QKV + D_QK_A + 128 * h:OFF_QKV + D_QK_A + 128 * (h + 1)]
        qs.append(q * lax.rsqrt(jnp.sum(q * q, axis=-1, keepdims=True) + EPS) * (DK_A ** -0.5))
        ks.append(k * lax.rsqrt(jnp.sum(k * k, axis=-1, keepdims=True) + EPS))
        vs.append(xc_ref[st, :, OFF_QKV + 2 * D_QK_A + 128 * h:OFF_QKV + 2 * D_QK_A + 128 * (h + 1)])
    n_p = len(pairs)
    betas = [gts[st]["beta_x"][:, hsl[h]] for st, h in pairs]
    egcs = [gts[st]["egc_x"][:, hsl[h]] for st, h in pairs]
    kbs = [ks[i] * betas[i] for i in range(n_p)]
    nts = [_mm_nt(jnp.concatenate([kbs[i], qs[i]], axis=0), ks[i]) for i in range(n_p)]
    decs = [jnp.exp(jnp.where(causal, gts[st]["gc"][:, LANE_G + h:LANE_G + h + 1]
                              - gts[st]["gct"][LANE_G + h:LANE_G + h + 1, :], -jnp.inf)) for st, h in pairs]
    a_s = [jnp.where(strict, nts[i][:c] * decs[i], 0.0) for i in range(n_p)]
    qks = [nts[i][c:] * decs[i] for i in range(n_p)]
    x_invs = _inv_unit_lower(a_s, ri, ci, eye, merge_sizes)
    sols = [_mm(x_invs[i], jnp.concatenate([vs[i] * betas[i], kbs[i] * egcs[i]], axis=1)) for i in range(n_p)]
    qes = [qs[i] * egcs[i] for i in range(n_p)]
    if n_seq == 1:
        vnews = [sols[i][:, :128] - _mm(sols[i][:, 128:], s_rd[st, h]) for i, (st, h) in enumerate(pairs)]
        os_ = [_mm(jnp.concatenate([qes[i], qks[i]], axis=1),
                   jnp.concatenate([s_rd[st, h], vnews[i]], axis=0)) for i, (st, h) in enumerate(pairs)]
    else:
        wss = [[_mm(jnp.concatenate([sols[i][r_seq * s:r_seq * (s + 1), 128:], qes[i][r_seq * s:r_seq * (s + 1)]],
                                    axis=0), s_rd[st * n_seq + s, h]) for s in range(n_seq)]
               for i, (st, h) in enumerate(pairs)]
        vnews = [jnp.concatenate([sols[i][r_seq * s:r_seq * (s + 1), :128] - wss[i][s][:r_seq]
                                  for s in range(n_seq)], axis=0) for i in range(n_p)]
        os_ = [jnp.concatenate([wss[i][s][r_seq:] for s in range(n_seq)], axis=0) + _mm(qks[i], vnews[i])
               for i in range(n_p)]
    for i, (st, h) in enumerate(pairs):
        kt = ks[i] * gts[st]["tail_x"][:, hsl[h]]
        for s in range(n_seq):
            kts = kt if n_seq == 1 else jnp.where(seq_of_row == s, kt, 0.0)
            sq = st * n_seq + s
            s_wr[sq, h] = s_rd[sq, h] * gts[st]["dec_x"][r_seq * s:r_seq * s + 1, hsl[h]] + _mm_tn(kts, vnews[i])
    for i, (st, h) in enumerate(pairs):
        gate = proj_ref[st, :, OFF_GATE + 128 * h:OFF_GATE + 128 * (h + 1)]
        mix_ref[st, :, hsl[h]] = _bf(_rms(os_[i], na_ref[:, hsl[h]]) * _silu(gate))

    hg = H_B // G_B
    wg = hg * P_B
    off_x = OFF_XBC
    off_b = off_x + D_B
    off_c = off_b + G_B * N_B
    for st, g in [(st, g) for st in streams for g in range(G_B)]:
        gt = gts[st]
        gc, gct = gt["gc"], gt["gct"]
        gs = slice(wg * g, wg * (g + 1))
        bg = xc_ref[st, :, off_b + N_B * g:off_b + N_B * (g + 1)]
        cg = xc_ref[st, :, off_c + N_B * g:off_c + N_B * (g + 1)]
        xs_g = xc_ref[st, :, off_x + wg * g:off_x + wg * (g + 1)]
        xdt = xs_g * gt["dt_x"][:, gs]
        cb = _mm_nt(cg, bg)
        ypairs = []
        for p in range(hg // 2):
            xp = xdt[:, 128 * p:128 * (p + 1)]
            ms = []
            for a_ in (0, 1):
                ln = LANE_DT + hg * g + 2 * p + a_
                seg = jnp.exp(jnp.where(causal, gc[:, ln:ln + 1] - gct[ln:ln + 1, :], -jnp.inf))
                ms.append(cb * seg)
            xpp = jnp.concatenate([jnp.where(lane < P_B, xp, 0.0), jnp.where(lane >= P_B, xp, 0.0)], axis=0)
            ypairs.append(_mm(jnp.concatenate(ms, axis=1), xpp))
        y = jnp.concatenate(ypairs, axis=1)
        yoffs = []
        for s in range(n_seq):
            rows = slice(r_seq * s, r_seq * (s + 1))
            yoffs.append(_mm_nt(cg[rows], h_rd[st * n_seq + s, wg * g:wg * (g + 1), :]))
        yoff = yoffs[0] if n_seq == 1 else jnp.concatenate(yoffs, axis=0)
        y = y + yoff * gt["edac_x"][:, gs] + dsk_ref[:, gs] * xs_g
        xt = xdt * gt["tailb_x"][:, gs]
        for s in range(n_seq):
            xts = xt if n_seq == 1 else jnp.where(seq_of_row == s, xt, 0.0)
            stt = _mm_tn(xts, bg)
            sq = st * n_seq + s
            for j in range(hg):
                hh = hg * g + j
                rs = slice(P_B * hh, P_B * (hh + 1))
                drow = gt["dec_x"][r_seq * s:r_seq * s + 1, 128 * (H_A + hh):128 * (H_A + hh + 1)]
                h_wr[sq, rs, :] = h_rd[sq, rs, :] * drow + stt[P_B * j:P_B * (j + 1), :]
        z = proj_ref[st, :, OFF_Z + wg * g:OFF_Z + wg * (g + 1)]
        mix_ref[st, :, D_V_A + wg * g:D_V_A + wg * (g + 1)] = _bf(_rms(y * _silu(z), nb_ref[:, gs]))


def _expand_mats():
    ea = np.zeros((128, 128 * H_A), np.float32)
    for h in range(H_A):
        ea[h, 128 * h:128 * (h + 1)] = 1.0
        ea[LANE_G + h, 128 * h:128 * (h + 1)] = 1.0
    eb = np.zeros((128, D_B), np.float32)
    for h in range(H_B):
        eb[LANE_DT + h, P_B * h:P_B * (h + 1)] = 1.0
    ec = np.zeros((128, 128 * (H_A + H_B)), np.float32)
    for h in range(H_A):
        ec[LANE_G + h, 128 * h:128 * (h + 1)] = 1.0
    for h in range(H_B):
        ec[LANE_DT + h, 128 * (H_A + h):128 * (H_A + h + 1)] = 1.0
    return jnp.asarray(ea, BF16), jnp.asarray(eb, BF16), jnp.asarray(ec, BF16)


def _const_spec(shape):
    return pl.BlockSpec(shape, lambda *_: (0,) * len(shape))


def _mixer_params(p):
    ea, eb, ec = _expand_mats()
    return (p["conv_w"], p["conv_b"], p["bias_slab"], p["alog_slab"], p["norm_a_x"], p["norm_b"], p["dskip_x"],
            ea, eb, ec)


def _mixer_prompt(proj, p):
    b, l, n = proj.shape
    nb = PROMPT_STREAMS if b % PROMPT_STREAMS == 0 else 1
    consts = _mixer_params(p)
    const_specs = [_const_spec(a.shape) for a in consts]
    return pl.pallas_call(
        functools.partial(_mixer_kernel, PROMPT_CHUNK, nb, 1, 0, PROMPT_CHUNK),
        grid=(b // nb, l // PROMPT_CHUNK),
        in_specs=[pl.BlockSpec((nb, PROMPT_CHUNK, n), lambda i, t: (i, t, 0))] + const_specs,
        out_specs=[
            pl.BlockSpec((nb, PROMPT_CHUNK, D_V_A + D_B), lambda i, t: (i, t, 0)),
            pl.BlockSpec((nb, H_A, DK_A, DV_A), lambda i, t: (i, 0, 0, 0)),
            pl.BlockSpec((nb, D_B, N_B), lambda i, t: (i, 0, 0)),
        ],
        out_shape=[
            jax.ShapeDtypeStruct((b, l, D_V_A + D_B), BF16),
            jax.ShapeDtypeStruct((b, H_A, DK_A, DV_A), F32),
            jax.ShapeDtypeStruct((b, D_B, N_B), F32),
        ],
        compiler_params=pltpu.CompilerParams(dimension_semantics=("parallel", "arbitrary"),
                                             vmem_limit_bytes=VMEM_LIMIT),
        name="mixer_prompt",
    )(proj, *consts)


def _mixer_decode(proj, dconv, sconv, s_delta, s_ssm, p):
    bs = s_delta.shape[0]
    n = proj.shape[1]
    l_dec = proj.shape[0] // bs
    n_seq = CHUNK // SEQ_ROWS
    tiles = bs // n_seq
    consts = _mixer_params(p)
    const_specs = [_const_spec(a.shape) for a in consts]
    return pl.pallas_call(
        functools.partial(_mixer_kernel, CHUNK, 1, n_seq, PAD_FRONT, PAD_FRONT + l_dec),
        grid=(tiles,),
        in_specs=[pl.BlockSpec((n_seq * l_dec, n), lambda i: (i, 0))] + const_specs + [
            pl.BlockSpec((n_seq, CONV_W - 1, D_QKV_A), lambda i: (i, 0, 0)),
            pl.BlockSpec((n_seq, CONV_W - 1, D_XBC), lambda i: (i, 0, 0)),
            pl.BlockSpec((n_seq, H_A, DK_A, DV_A), lambda i: (i, 0, 0, 0)),
            pl.BlockSpec((n_seq, D_B, N_B), lambda i: (i, 0, 0)),
        ],
        out_specs=[
            pl.BlockSpec((1, CHUNK, D_V_A + D_B), lambda i: (i, 0, 0)),
            pl.BlockSpec((n_seq, H_A, DK_A, DV_A), lambda i: (i, 0, 0, 0)),
            pl.BlockSpec((n_seq, D_B, N_B), lambda i: (i, 0, 0)),
            pl.BlockSpec((n_seq, CONV_W - 1, D_QKV_A), lambda i: (i, 0, 0)),
            pl.BlockSpec((n_seq, CONV_W - 1, D_XBC), lambda i: (i, 0, 0)),
        ],
        out_shape=[
            jax.ShapeDtypeStruct((tiles, CHUNK, D_V_A + D_B), BF16),
            jax.ShapeDtypeStruct((bs, H_A, DK_A, DV_A), F32),
            jax.ShapeDtypeStruct((bs, D_B, N_B), F32),
            jax.ShapeDtypeStruct((bs, CONV_W - 1, D_QKV_A), F32),
            jax.ShapeDtypeStruct((bs, CONV_W - 1, D_XBC), F32),
        ],
        scratch_shapes=[pltpu.VMEM((1, CHUNK, n), F32), pltpu.VMEM((1, CHUNK, n), F32)],
        compiler_params=pltpu.CompilerParams(dimension_semantics=("parallel",), vmem_limit_bytes=VMEM_LIMIT),
        name="mixer_decode",
    )(proj, *consts, dconv, sconv, s_delta, s_ssm)


def _outproj_router_kernel(sub, mix_ref, x_ref, wo_ref, nf_ref, wr_ref, br_ref, tril_ref, xmid_ref, route_ref,
                           meta_ref):
    xm = x_ref[...] + jnp.dot(mix_ref[...], wo_ref[...], preferred_element_type=F32)
    xmid_ref[...] = xm
    t = _rms(xm, nf_ref[...])
    t1 = _bf(t)
    t2 = _bf(t - t1.astype(F32))
    d = functools.partial(jnp.dot, preferred_element_type=F32)
    hl = d(t1, wr_ref[...])
    logit = hl[:, :128] + hl[:, 128:] + d(t2, wr_ref[:, :128]) + br_ref[...]

    lane = lax.broadcasted_iota(jnp.int32, (1, 128), 1).astype(F32)
    neg = -jnp.inf
    big = 1e9
    is_grp = lane < N_GROUPS_E
    gl = jnp.where(is_grp, logit, neg)
    gmax = jnp.max(gl, axis=-1, keepdims=True)
    gsel = jnp.min(jnp.where(gl == gmax, lane, big), axis=-1, keepdims=True)
    gw = 1.0 / jnp.sum(jnp.exp(jnp.where(is_grp, logit - gmax, neg)), axis=-1, keepdims=True)
    lo = N_GROUPS_E + EXPERTS_PER_GROUP * gsel
    el = jnp.where((lane >= lo) & (lane < lo + EXPERTS_PER_GROUP), logit, neg)
    v1 = jnp.max(el, axis=-1, keepdims=True)
    i1 = jnp.min(jnp.where(el == v1, lane, big), axis=-1, keepdims=True)
    el2 = jnp.where(lane == i1, neg, el)
    v2 = jnp.max(el2, axis=-1, keepdims=True)
    i2 = jnp.min(jnp.where(el2 == v2, lane, big), axis=-1, keepdims=True)
    e = jnp.exp(v2 - v1)
    den = gw / (1.0 + e)
    comb = jnp.where(lane == i1, den, 0.0) + jnp.where(lane == i2, e * den, 0.0)

    tm = logit.shape[0]
    onehot = jnp.where((lane == gsel) & is_grp, 1.0, 0.0)
    ranks = jnp.dot(tril_ref[...], _bf(onehot), preferred_element_type=F32)
    cnt = ranks[tm - 1:tm, :] + onehot[tm - 1:tm, :]
    ntile = jnp.floor((cnt + (sub - 1)) * (1.0 / sub))
    li = lax.broadcasted_iota(jnp.int32, (128, 128), 0)
    lj = lax.broadcasted_iota(jnp.int32, (128, 128), 1)
    off = _mm01_r(jnp.broadcast_to(ntile * sub, (8, 128)), _bf((li < lj).astype(F32)))[0:1]
    pos = jnp.sum(onehot * (off + ranks), axis=-1, keepdims=True)
    route_ref[...] = jnp.where(lane == 0.0, gsel, jnp.where(lane == 1.0, pos, comb))
    row8 = lax.broadcasted_iota(jnp.int32, (8, 128), 0)
    meta_ref[...] = jnp.where(row8 == 0, off, ntile)


def _outproj_router(mix, x2d, p, tm, sub):
    t, d = x2d.shape
    nb = t // tm
    tril = jnp.asarray(np.tri(tm, tm, -1, dtype=np.float32), BF16)
    return pl.pallas_call(
        functools.partial(_outproj_router_kernel, sub),
        grid=(nb,),
        in_specs=[
            pl.BlockSpec((tm, mix.shape[1]), lambda i: (i, 0)),
            pl.BlockSpec((tm, d), lambda i: (i, 0)),
            _const_spec(p["w_out"].shape),
            _const_spec((1, d)),
            _const_spec(p["w_router"].shape),
            _const_spec((1, 128)),
            _const_spec((tm, tm)),
        ],
        out_specs=[pl.BlockSpec((tm, d), lambda i: (i, 0)), pl.BlockSpec((tm, 128), lambda i: (i, 0)),
                   pl.BlockSpec((None, 8, 128), lambda i: (i, 0, 0))],
        out_shape=[jax.ShapeDtypeStruct((t, d), F32), jax.ShapeDtypeStruct((t, 128), F32),
                   jax.ShapeDtypeStruct((nb, 8, 128), F32)],
        compiler_params=pltpu.CompilerParams(dimension_semantics=("parallel",), vmem_limit_bytes=VMEM_LIMIT),
        name="outproj_router",
    )(mix, x2d, p["w_out"], p["norm_ffn"], p["w_router"], p["b_router"], tril)


def _moe_kernel(tb, meta_ref, pos_ref, xmid_ref, route_ref, nffn_ref, nfin_ref, wg_ref, wu_ref, wd_ref,
                y_ref, xg_scr, rg_scr):
    b = pl.program_id(0)
    g = pl.program_id(1)
    unroll = 8

    @pl.when(g == 0)
    def _():
        y_ref[...] = _rms(xmid_ref[...], nffn_ref[...])
        for gg in range(N_GROUPS_E):
            nq_g = meta_ref[b, N_GROUPS_E + gg]

            @pl.when(nq_g > 0)
            def _(gg=gg, nq_g=nq_g):
                r = pl.multiple_of(meta_ref[b, gg] + (nq_g - 1) * MOE_ALIGN, MOE_ALIGN)
                xg_scr[pl.ds(r, MOE_ALIGN), :] = jnp.zeros((MOE_ALIGN, D_MODEL), F32)
                rg_scr[pl.ds(r, MOE_ALIGN), :] = jnp.zeros((MOE_ALIGN, 128), F32)

        def dispatch(i, carry):
            base = pl.multiple_of(i * unroll, unroll)
            src = y_ref.at[pl.ds(base, unroll), :]
            rsrc = route_ref.at[pl.ds(base, unroll), :]
            for u in range(unroll):
                p = pos_ref[0, base + u]
                xg_scr[pl.ds(p, 1), :] = src[u:u + 1, :]
                rg_scr[pl.ds(p, 1), :] = rsrc[u:u + 1, :]
            return carry

        lax.fori_loop(0, tb // unroll, dispatch, 0)

    off = meta_ref[b, g]
    nq = meta_ref[b, N_GROUPS_E + g]
    lane = lax.broadcasted_iota(jnp.int32, (1, 128), 1)
    d = functools.partial(jnp.dot, preferred_element_type=F32)

    def tile(r0, m):
        r0 = pl.multiple_of(r0, MOE_ALIGN)
        x = _bf(xg_scr[pl.ds(r0, m), :])
        rg = rg_scr[pl.ds(r0, m), :]
        contrib = None
        for j in range(EXPERTS_PER_GROUP):
            e16 = g * EXPERTS_PER_GROUP + j
            cw = jnp.sum(jnp.where(lane == N_GROUPS_E + e16, rg, 0.0), axis=-1, keepdims=True)
            hid = _silu(d(x, wg_ref[j])) * d(x, wu_ref[j]) * cw
            cj = d(_bf(hid), wd_ref[j])
            contrib = cj if contrib is None else contrib + cj

        xg_scr[pl.ds(r0, m), :] = contrib

    per_tile = MOE_SUB // MOE_ALIGN
    nloop = jnp.maximum(nq // per_tile - 1, 0)

    def full_tile(j, carry):
        tile(off + j * MOE_SUB, MOE_SUB)
        return carry

    lax.fori_loop(0, nloop, full_tile, 0)
    last = nq - nloop * per_tile
    r_last = off + nloop * MOE_SUB
    for q in range(1, 2 * per_tile):
        @pl.when(last == q)
        def _(q=q):
            tile(r_last, q * MOE_ALIGN)

    @pl.when(g == N_GROUPS_E - 1)
    def _():
        def combine(i, carry):
            base = pl.multiple_of(i * unroll, unroll)
            dst = y_ref.at[pl.ds(base, unroll), :]
            for u in range(unroll):
                p = pos_ref[0, base + u]
                dst[u:u + 1, :] = xg_scr[pl.ds(p, 1), :]
            return carry

        lax.fori_loop(0, tb // unroll, combine, 0)
        y_ref[...] = _rms(xmid_ref[...] + y_ref[...], nfin_ref[...])


def _moe(xmid, route, meta, p, tb):
    t, d = xmid.shape
    nb = t // tb
    rows = tb + N_GROUPS_E * MOE_ALIGN
    pos = route[:, 1].astype(jnp.int32).reshape(nb, 1, tb)
    meta_i = jnp.concatenate([meta[:, 0, :N_GROUPS_E], meta[:, 1, :N_GROUPS_E]], axis=1).astype(jnp.int32)
    grid_spec = pltpu.PrefetchScalarGridSpec(
        num_scalar_prefetch=1,
        grid=(nb, N_GROUPS_E),
        in_specs=[
            pl.BlockSpec((None, 1, tb), lambda i, e, m: (i, 0, 0), memory_space=pltpu.SMEM),
            pl.BlockSpec((tb, d), lambda i, e, m: (i, 0)),
            pl.BlockSpec((tb, 128), lambda i, e, m: (i, 0)),
            pl.BlockSpec((1, d), lambda i, e, m: (0, 0)),
            pl.BlockSpec((1, d), lambda i, e, m: (0, 0)),
            pl.BlockSpec((EXPERTS_PER_GROUP, d, D_EXPERT), lambda i, e, m: (e, 0, 0)),
            pl.BlockSpec((EXPERTS_PER_GROUP, d, D_EXPERT), lambda i, e, m: (e, 0, 0)),
            pl.BlockSpec((EXPERTS_PER_GROUP, D_EXPERT, d), lambda i, e, m: (e, 0, 0)),
        ],
        out_specs=pl.BlockSpec((tb, d), lambda i, e, m: (i, 0)),
        scratch_shapes=[pltpu.VMEM((rows, d), F32), pltpu.VMEM((rows, 128), F32)],
    )
    return pl.pallas_call(
        functools.partial(_moe_kernel, tb),
        grid_spec=grid_spec,
        out_shape=jax.ShapeDtypeStruct((t, d), F32),
        compiler_params=pltpu.CompilerParams(dimension_semantics=("parallel", "arbitrary"),
                                             vmem_limit_bytes=VMEM_LIMIT),
        name="moe",
    )(meta_i, pos, xmid, route, p["norm_ffn"], p["norm_final"], p["w_gate"], p["w_up"], p["w_down"])


def _pick_tile(t, pref):
    tm = min(pref, t)
    while t % tm:
        tm //= 2
    return tm


def _prep_layer(l, norm_mix, w_in, conv_a_w, a_log_a, dt_bias_a, norm_a, conv_b_w, conv_b_b, a_log_b, dt_bias_b,
                d_skip_b, norm_b, w_out, norm_ffn, w_router_group, b_router_group, w_router_expert,
                b_router_expert, w_gate_e, w_up_e, w_down_e, norm_final):
    def on_conv_columns(a_qkv, a_xbc):
        out = jnp.zeros(a_qkv.shape[:-1] + (D_PROJ,), F32)
        out = out.at[..., OFF_QKV:OFF_QKV + D_QKV_A].set(a_qkv.astype(F32))
        return out.at[..., OFF_XBC:OFF_XBC + D_XBC].set(a_xbc.astype(F32))

    def slab(a8, b16):
        return jnp.concatenate([jnp.zeros((H_A,), F32), a8.astype(F32), b16.astype(F32),
                                jnp.zeros((128 - 2 * H_A - H_B,), F32)]).reshape(1, 128)

    w_router = jnp.concatenate(
        [w_router_group[l].astype(F32), w_router_expert[l].reshape(D_MODEL, N_EXPERTS).astype(F32),
         jnp.zeros((D_MODEL, 128 - N_GROUPS_E - N_EXPERTS), F32)], axis=1)
    w_router_hi = _bf(w_router)
    w_router = jnp.concatenate([w_router_hi, _bf(w_router - w_router_hi.astype(F32))], axis=1)
    b_router = jnp.concatenate(
        [b_router_group[l].astype(F32), b_router_expert[l].reshape(N_EXPERTS).astype(F32),
         jnp.zeros((128 - N_GROUPS_E - N_EXPERTS,), F32)]).reshape(1, 128)
    return {
        "norm_mix": norm_mix[l].astype(F32).reshape(1, D_MODEL),
        "w_all": _prep_w_in(w_in, l),
        "conv_w": on_conv_columns(conv_a_w[l], conv_b_w[l]),
        "conv_b": on_conv_columns(jnp.zeros((1, D_QKV_A), F32), conv_b_b[l].reshape(1, D_XBC)),
        "bias_slab": slab(dt_bias_a[l], dt_bias_b[l]),
        "alog_slab": slab(a_log_a[l], a_log_b[l]),
        "norm_a_x": jnp.tile(norm_a[l].astype(F32), H_A).reshape(1, D_V_A),
        "norm_b": norm_b[l].astype(F32).reshape(1, D_B),
        "dskip_x": jnp.repeat(d_skip_b[l].astype(F32), P_B).reshape(1, D_B),
        "norm_ffn": norm_ffn[l].astype(F32).reshape(1, D_MODEL),
        "w_router": w_router,
        "b_router": b_router,
        "norm_final": norm_final.astype(F32).reshape(1, D_MODEL),
        "ffn_f32": [w_out[l].astype(F32),
                    w_gate_e[l].astype(F32).reshape(N_EXPERTS * D_MODEL, D_EXPERT),
                    w_up_e[l].astype(F32).reshape(N_EXPERTS * D_MODEL, D_EXPERT),
                    w_down_e[l].astype(F32).reshape(N_EXPERTS * D_EXPERT, D_MODEL)],
    }


def _set_ffn_weights(p, w_bf16):
    w_out, w_gate, w_up, w_down = w_bf16
    p["w_out"] = w_out
    p["w_gate"] = w_gate.reshape(N_EXPERTS, D_MODEL, D_EXPERT)
    p["w_up"] = w_up.reshape(N_EXPERTS, D_MODEL, D_EXPERT)
    p["w_down"] = w_down.reshape(N_EXPERTS, D_EXPERT, D_MODEL)


def _ffn_tail(mix2d, x2d, p):
    t = x2d.shape[0]
    tb = _pick_tile(t, MOE_BLOCK)
    xmid, route, meta = _outproj_router(mix2d, x2d, p, tb, MOE_ALIGN)
    return _moe(xmid, route, meta, p, tb)


def kernel(x_prompt, x_sample, state_delta, state_delta_conv, state_ssm, state_ssm_conv, norm_mix, w_in, conv_a_w,
           a_log_a, dt_bias_a, norm_a, conv_b_w, conv_b_b, a_log_b, dt_bias_b, d_skip_b, norm_b, w_out, norm_ffn,
           w_router_group, b_router_group, w_router_expert, b_router_expert, w_gate_e, w_up_e, w_down_e,
           norm_final):
    depth = w_in.shape[0]
    assert depth == 1, "the fused final norm assumes a single layer"
    bp, lp, d = x_prompt.shape
    bs, ls, _ = x_sample.shape
    assert lp % PROMPT_CHUNK == 0 and lp >= CONV_W - 1
    assert PAD_FRONT + ls <= SEQ_ROWS and bs % (CHUNK // SEQ_ROWS) == 0 and (CHUNK // SEQ_ROWS * ls) % 8 == 0
    l = 0
    p = _prep_layer(l, norm_mix, w_in, conv_a_w, a_log_a, dt_bias_a, norm_a, conv_b_w, conv_b_b, a_log_b,
                    dt_bias_b, d_skip_b, norm_b, w_out, norm_ffn, w_router_group, b_router_group,
                    w_router_expert, b_router_expert, w_gate_e, w_up_e, w_down_e, norm_final)

    xp2 = x_prompt.reshape(bp * lp, d)
    tm_p = _pick_tile(lp, 256)
    steps = bp * lp // tm_p
    in_steps = [_can_cast_in_steps(a, steps) for a in p["ffn_f32"]]
    proj_p, tails, cast = _inproj_conv(xp2, p, tm_p, lp, [a for a, ok in zip(p["ffn_f32"], in_steps) if ok])
    _set_ffn_weights(p, [cast.pop(0) if ok else _bf(a) for a, ok in zip(p["ffn_f32"], in_steps)])
    mix_p, delta_p, ssm_p = _mixer_prompt(proj_p.reshape(bp, lp, D_PROJ), p)
    tails = tails.reshape(bp, lp // tm_p, 8, D_CONV)[:, -1, 8 - (CONV_W - 1):]
    dconv_p = tails[:, :, :D_QKV_A]
    sconv_p = tails[:, :, D_QKV_A:]
    y_p = _ffn_tail(mix_p.reshape(bp * lp, -1), xp2, p).reshape(bp, lp, d)

    xs2 = x_sample.reshape(bs * ls, d)
    proj_s = _inproj(xs2, p, _pick_tile(bs * ls, 256))
    dconv_in = state_delta_conv[l].astype(F32)
    sconv_in = state_ssm_conv[l].astype(F32)
    mix_s, delta_s, ssm_s, dconv_s, sconv_s = _mixer_decode(
        proj_s, dconv_in, sconv_in, state_delta[l].astype(F32), state_ssm[l].astype(F32).reshape(bs, D_B, N_B), p)
    mix_s = mix_s.reshape(bs, SEQ_ROWS, -1)[:, PAD_FRONT:PAD_FRONT + ls].reshape(bs * ls, -1)
    y_s = _ffn_tail(mix_s, xs2, p).reshape(bs, ls, d)

    return (y_p.astype(x_prompt.dtype), y_s.astype(x_sample.dtype),
            delta_p[None], dconv_p[None], ssm_p.reshape(bp, H_B, P_B, N_B)[None], sconv_p[None],
            delta_s[None], dconv_s[None], ssm_s.reshape(bs, H_B, P_B, N_B)[None], sconv_s[None])
```

```python
import functools

import numpy as np
import jax
import jax.numpy as jnp
from jax import lax
from jax.experimental import pallas as pl
from jax.experimental.pallas import tpu as pltpu

F32 = jnp.float32
BF16 = jnp.bfloat16

D_MODEL = 1024
H_A, DK_A, DV_A = 8, 128, 128
D_QK_A = H_A * DK_A
D_V_A = H_A * DV_A
D_QKV_A = 2 * D_QK_A + D_V_A
H_B, P_B, N_B, G_B = 16, 64, 128, 2
D_B = H_B * P_B
D_XBC = D_B + 2 * G_B * N_B
CONV_W = 4
N_GROUPS_E, EXPERTS_PER_GROUP = 4, 4
N_EXPERTS = N_GROUPS_E * EXPERTS_PER_GROUP
D_EXPERT = 512
EPS = 1e-6

OFF_QKV = 0
OFF_GATE = OFF_QKV + D_QKV_A
OFF_Z = OFF_GATE + D_V_A
OFF_XBC = OFF_Z + D_B
OFF_SMALL = OFF_XBC + D_XBC
D_PROJ = OFF_SMALL + 128
D_CONV = D_QKV_A + D_XBC
CONV_RANGES = ((OFF_QKV, D_QKV_A), (OFF_XBC, D_XBC))
LANE_G = H_A
LANE_DT = 2 * H_A

CHUNK = 64
PROMPT_CHUNK = 128
SEQ_ROWS = 8
PROMPT_STREAMS = 2
PAD_FRONT = CONV_W - 1
INPROJ_SPREAD = 1
INPROJ_COLS = 256
MOE_BLOCK = 1024
MOE_SUB = 256
MOE_ALIGN = 64
V7X_VMEM_BYTES = 64 * 1024 * 1024
VMEM_LIMIT = V7X_VMEM_BYTES - 12 * 1024 * 1024


def _bf(x):
    return x.astype(BF16)


def _mm(a, b):
    return jnp.dot(_bf(a), _bf(b), preferred_element_type=F32)


def _mm_nt(a, b):
    return lax.dot_general(_bf(a), _bf(b), (((1,), (1,)), ((), ())), preferred_element_type=F32)


def _mm_tn(a, b):
    return lax.dot_general(_bf(a), _bf(b), (((0,), (0,)), ((), ())), preferred_element_type=F32)


def _split3(x):
    x1 = _bf(x)
    r1 = x - x1.astype(F32)
    x2 = _bf(r1)
    x3 = _bf(r1 - x2.astype(F32))
    return x1, x2, x3


def _mm01_r(x, m01):
    x1, x2, x3 = _split3(x)
    d = functools.partial(jnp.dot, preferred_element_type=F32)
    return d(x1, m01) + d(x2, m01) + d(x3, m01)


def _mm01_r2(x, m01):
    x1 = _bf(x)
    x2 = _bf(x - x1.astype(F32))
    d = functools.partial(jnp.dot, preferred_element_type=F32)
    return d(x1, m01) + d(x2, m01)


def _mm01_l(m01, x):
    x1, x2, x3 = _split3(x)
    d = functools.partial(jnp.dot, preferred_element_type=F32)
    return d(m01, x1) + d(m01, x2) + d(m01, x3)


def _rms(x, w):
    return x * lax.rsqrt(jnp.mean(x * x, axis=-1, keepdims=True) + EPS) * w


def _silu(x):
    return x * jax.nn.sigmoid(x)


def _softplus(x):
    return jnp.maximum(x, 0.0) + jnp.log1p(jnp.exp(-jnp.abs(x)))


def _weight_blocks(width):
    return [slice(c0, min(c0 + width, D_PROJ)) for c0 in range(0, D_PROJ, width)]


def _is_conv_block(dst):
    return any(lo <= dst.start and dst.stop <= lo + n for lo, n in CONV_RANGES)


def _prep_w_in_kernel(wt_ref, o_ref):
    c_beta = D_QKV_A + D_V_A
    c_z = c_beta + 2 * H_A
    c_dt = c_z + D_B + D_XBC
    n_small = 2 * H_A + H_B
    o_ref[:, OFF_QKV:OFF_Z] = _bf(wt_ref[0:c_beta, :].T)
    o_ref[:, OFF_Z:OFF_SMALL] = _bf(wt_ref[c_z:c_dt, :].T)
    head = wt_ref[c_beta:c_beta + 128, :].T
    tail = wt_ref[c_dt + H_B - 128:c_dt + H_B, :].T
    lane = lax.broadcasted_iota(jnp.int32, (1, 128), 1)
    small = jnp.where(lane < 2 * H_A, head, jnp.where(lane < n_small, pltpu.roll(tail, n_small, 1), 0.0))
    o_ref[:, OFF_SMALL:D_PROJ] = _bf(small)


def _prep_w_in(w, l):
    _, d, n = w.shape
    rows = 128
    return pl.pallas_call(
        _prep_w_in_kernel,
        grid=(d // rows,),
        in_specs=[pl.BlockSpec((None, n, rows), lambda i: (l, 0, i))],
        out_specs=pl.BlockSpec((rows, D_PROJ), lambda i: (i, 0)),
        out_shape=jax.ShapeDtypeStruct((d, D_PROJ), BF16),
        compiler_params=pltpu.CompilerParams(dimension_semantics=("parallel",), vmem_limit_bytes=VMEM_LIMIT),
        name="prep_w_in",
    )(jnp.swapaxes(w.astype(F32), 1, 2))


def _inproj_kernel(x_ref, nw_ref, w_ref, o_ref):
    h = _bf(_rms(x_ref[...], nw_ref[...]))
    o_ref[...] = jnp.dot(h, w_ref[...], preferred_element_type=F32)


def _weight_specs(p):
    return [pl.BlockSpec(p["w_all"].shape, lambda i: (0, 0), pipeline_mode=pl.Buffered(1))]


def _inproj(x2d, p, tm):
    t, d = x2d.shape
    return pl.pallas_call(
        _inproj_kernel,
        grid=(t // tm,),
        in_specs=[pl.BlockSpec((tm, d), lambda i: (i, 0)), pl.BlockSpec((1, d), lambda i: (0, 0))] + _weight_specs(p),
        out_specs=pl.BlockSpec((tm, D_PROJ), lambda i: (i, 0)),
        out_shape=jax.ShapeDtypeStruct((t, D_PROJ), F32),
        compiler_params=pltpu.CompilerParams(dimension_semantics=("parallel",), vmem_limit_bytes=VMEM_LIMIT),
        name="inproj",
    )(x2d, p["norm_mix"], p["w_all"])


def _conv_silu(u, prev8, cw_ref, cb_ref, sl):
    row8 = lax.broadcasted_iota(jnp.int32, (8, 1), 0)
    acc = u * cw_ref[CONV_W - 1:CONV_W, sl] + cb_ref[:, sl]
    for k in range(1, CONV_W):
        ru = pltpu.roll(u, k, 0)
        if prev8 is not None:
            head = jnp.where(row8 < k, pltpu.roll(prev8, k, 0), ru[0:8])
            ru = jnp.concatenate([head, ru[8:]], axis=0)
        acc = acc + ru * cw_ref[CONV_W - 1 - k:CONV_W - k, sl]
    return _silu(acc)


def _inproj_conv_kernel(tiles_per_seq, n_cast, x_ref, nw_ref, w_ref, cw_ref, cb_ref, *rest):
    cast_in = rest[:n_cast]
    o_ref, tail_ref = rest[n_cast:n_cast + 2]
    cast_out = rest[n_cast + 2:2 * n_cast + 2]
    prev_ref = rest[2 * n_cast + 2]
    for src, dst in zip(cast_in, cast_out):
        dst[...] = _bf(src[...])

    @pl.when(pl.program_id(0) % tiles_per_seq == 0)
    def _():
        prev_ref[...] = jnp.zeros_like(prev_ref)

    h = _bf(_rms(x_ref[...], nw_ref[...]))
    tm = h.shape[0]
    blocks = _weight_blocks(INPROJ_COLS)
    conv_blocks = [b for b in blocks if _is_conv_block(b)]
    plain_blocks = [b for b in blocks if not _is_conv_block(b)]
    blocks = []
    for j, b in enumerate(conv_blocks):
        blocks.append(b)
        if j % INPROJ_SPREAD == INPROJ_SPREAD - 1 and plain_blocks:
            blocks.append(plain_blocks.pop(0))
    blocks += plain_blocks
    for dst in blocks:
        u = jnp.dot(h, w_ref[:, dst], preferred_element_type=F32)
        if _is_conv_block(dst):
            o_ref[:, dst] = _conv_silu(u, prev_ref[:, dst], cw_ref, cb_ref, dst)
            prev_ref[:, dst] = u[tm - 8:tm]
            t0 = dst.start - OFF_QKV if dst.start < OFF_XBC else dst.start - OFF_XBC + D_QKV_A
            tail_ref[:, t0:t0 + (dst.stop - dst.start)] = u[tm - 8:tm]
        else:
            o_ref[:, dst] = u


def _can_cast_in_steps(a, steps):
    return a.shape[0] % steps == 0 and (a.shape[0] // steps) % 16 == 0


def _inproj_conv(x2d, p, tm, seq_len, to_cast):
    t, d = x2d.shape
    assert seq_len % tm == 0
    steps = t // tm
    cast_specs = [pl.BlockSpec((a.shape[0] // steps, a.shape[1]), lambda i: (i, 0)) for a in to_cast]
    res = pl.pallas_call(
        functools.partial(_inproj_conv_kernel, seq_len // tm, len(to_cast)),
        grid=(steps,),
        in_specs=[pl.BlockSpec((tm, d), lambda i: (i, 0)), pl.BlockSpec((1, d), lambda i: (0, 0))] + _weight_specs(p)
        + [_const_spec(p["conv_w"].shape), _const_spec(p["conv_b"].shape)] + cast_specs,
        out_specs=[pl.BlockSpec((tm, D_PROJ), lambda i: (i, 0)), pl.BlockSpec((None, 8, D_CONV), lambda i: (i, 0, 0))]
        + cast_specs,
        out_shape=[jax.ShapeDtypeStruct((t, D_PROJ), F32), jax.ShapeDtypeStruct((steps, 8, D_CONV), F32)]
        + [jax.ShapeDtypeStruct(a.shape, BF16) for a in to_cast],
        scratch_shapes=[pltpu.VMEM((8, D_PROJ), F32)],
        compiler_params=pltpu.CompilerParams(dimension_semantics=("arbitrary",), vmem_limit_bytes=VMEM_LIMIT),
        name="inproj_conv",
    )(x2d, p["norm_mix"], p["w_all"], p["conv_w"], p["conv_b"], *to_cast)
    return res[0], res[1], list(res[2:])


def _inv_unit_lower(mats, ri, ci, eye, merge_sizes):
    blk = (ri >> 3) == (ci >> 3)
    ads = [jnp.where(blk, a, 0.0) for a in mats]
    xs = [eye - ad for ad in ads]
    n = eye.shape[0]
    a2s = [_mm(ad, ad) for ad in ads]
    st = [_mm(jnp.concatenate([x, a2], axis=0), a2) for x, a2 in zip(xs, a2s)]
    xs = [x + t[:n] for x, t in zip(xs, st)]
    xs = [x + _mm(x, t[n:]) for x, t in zip(xs, st)]
    for s in merge_sizes:
        sh = s.bit_length() - 1
        rb = ri >> sh
        sel = ((rb & 1) == 1) & ((ci >> sh) == rb - 1)
        odd = [slice(r, r + s) for r in range(s, n, 2 * s)]
        even = [slice(r, r + s) for r in range(0, n, 2 * s)]
        xo = [jnp.concatenate([x[sl] for sl in odd], axis=0) for x in xs]
        ts = [_mm(o, jnp.where(sel, a, 0.0)) for o, a in zip(xo, mats)]
        xo = [o - _mm(t, x) for o, t, x in zip(xo, ts, xs)]
        xs = [jnp.concatenate([blk for i, ev in enumerate(even) for blk in (x[ev], o[i * s:(i + 1) * s])], axis=0)
              for x, o in zip(xs, xo)]
    return xs


def _mixer_kernel(c, n_streams, n_seq, valid_lo, valid_hi, *refs):
    carry = n_seq == 1
    (proj_ref, cw_ref, cb_ref, bias_ref, alog_ref, na_ref, nb_ref, dsk_ref,
     ea_ref, eb_ref, ec_ref) = refs[:11]
    if carry:
        mix_ref, s_wr, h_wr = refs[11:]
        s_rd, h_rd = s_wr, h_wr
        xc_ref = proj_ref

        @pl.when(pl.program_id(1) == 0)
        def _():
            s_wr[...] = jnp.zeros_like(s_wr)
            h_wr[...] = jnp.zeros_like(h_wr)
    else:
        dconv_ref, sconv_ref, s_rd, h_rd, mix_ref, s_wr, h_wr, dconv_wr, sconv_wr, xc_ref, tile_ref = refs[11:]
        tok_ref = proj_ref
        l_tok = tok_ref.shape[0] // n_seq
        tile_ref[...] = jnp.zeros_like(tile_ref)
        for s in range(n_seq):
            r0 = (c // n_seq) * s
            tile_ref[0, r0:r0 + CONV_W - 1, OFF_QKV:OFF_QKV + D_QKV_A] = dconv_ref[s]
            tile_ref[0, r0:r0 + CONV_W - 1, OFF_XBC:OFF_XBC + D_XBC] = sconv_ref[s]
            tile_ref[0, r0 + valid_lo:r0 + valid_lo + l_tok, :] = tok_ref[l_tok * s:l_tok * (s + 1), :]
        for s in range(n_seq):
            r1 = (c // n_seq) * s + l_tok
            dconv_wr[s] = tile_ref[0, r1:r1 + CONV_W - 1, OFF_QKV:OFF_QKV + D_QKV_A]
            sconv_wr[s] = tile_ref[0, r1:r1 + CONV_W - 1, OFF_XBC:OFF_XBC + D_XBC]
        proj_ref = tile_ref

    r_seq = c // n_seq
    sh_seq = r_seq.bit_length() - 1
    row = lax.broadcasted_iota(jnp.int32, (c, 1), 0)
    lane = lax.broadcasted_iota(jnp.int32, (1, 128), 1)
    ri = lax.broadcasted_iota(jnp.int32, (c, c), 0)
    ci = lax.broadcasted_iota(jnp.int32, (c, c), 1)
    eye = (ri == ci).astype(F32)
    seq_of_row = row >> sh_seq
    if carry:
        same = None
        causal = ri >= ci
        strict = ri > ci
        valid = None
        merge_sizes = tuple(8 << i for i in range((c // 8).bit_length() - 1))
    else:
        same = (ri >> sh_seq) == (ci >> sh_seq)
        causal = (ri >= ci) & same
        strict = (ri > ci) & same
        rr = row & (r_seq - 1)
        valid = ((rr >= valid_lo) & (rr < valid_hi)).astype(F32)
        merge_sizes = ()
    causal_bf = _bf(causal.astype(F32))
    is_beta = lane < LANE_G
    is_g = (lane >= LANE_G) & (lane < LANE_DT)
    is_dt = (lane >= LANE_DT) & (lane < LANE_DT + H_B)
    streams = range(n_streams)

    def conv(st):
        for lo, n in CONV_RANGES:
            for c0 in range(lo, lo + n, 512):
                sl = slice(c0, c0 + 512)
                xc_ref[st, :, sl] = _conv_silu(proj_ref[st, :, sl], None, cw_ref, cb_ref, sl)

    def gates(st):
        raw = proj_ref[st, :, OFF_SMALL:OFF_SMALL + 128]
        sp = _softplus(raw + bias_ref[...])
        coef = -jnp.exp(alog_ref[...])
        gd = jnp.where(is_g | is_dt, sp * coef, 0.0)
        q1 = jnp.where(is_beta, jax.nn.sigmoid(raw), jnp.where(is_dt, sp, 0.0))
        if valid is not None:
            gd = gd * valid
            q1 = q1 * valid
        gc = _mm01_l(causal_bf, gd)
        if carry:
            tot = jnp.broadcast_to(gc[c - 1:c, :], (c, 128))
        else:
            tot = _mm01_l(_bf(same.astype(F32)), gd)
        eg = jnp.exp(gc)
        et = jnp.exp(tot - gc)
        xa = _mm01_r2(jnp.concatenate([q1, jnp.where(is_g, eg, 0.0), jnp.where(is_g, et, 0.0)], axis=0),
                      ea_ref[...])
        xb = _mm01_r2(jnp.concatenate([q1, eg, et], axis=0), eb_ref[...])
        etot = jnp.exp(tot[0:8] if carry else tot)
        return {
            "gc": gc, "gct": gc.T,
            "beta_x": xa[0:c], "egc_x": xa[c:2 * c], "tail_x": xa[2 * c:3 * c],
            "dt_x": xb[0:c], "edac_x": xb[c:2 * c], "tailb_x": xb[2 * c:3 * c],
            "dec_x": _mm01_r(etot, ec_ref[...]),
        }

    if not carry:
        for st in streams:
            conv(st)
    gts = [gates(st) for st in streams]

    pairs = [(st, h) for st in streams for h in range(H_A)]
    hsl = [slice(128 * h, 128 * (h + 1)) for h in range(H_A)]
    qs, ks, vs = [], [], []
    for st, h in pairs:
        q = xc_ref[st, :, OFF_QKV + 128 * h:OFF_QKV + 128 * (h + 1)]
        k = xc_ref[st, :, OFF_QKV + D_QK_A + 128 * h:OFF_QKV + D_QK_A + 128 * (h + 1)]
        qs.append(q * lax.rsqrt(jnp.sum(q * q, axis=-1, keepdims=True) + EPS) * (DK_A ** -0.5))
        ks.append(k * lax.rsqrt(jnp.sum(k * k, axis=-1, keepdims=True) + EPS))
        vs.append(xc_ref[st, :, OFF_QKV + 2 * D_QK_A + 128 * h:OFF_QKV + 2 * D_QK_A + 128 * (h + 1)])
    n_p = len(pairs)
    betas = [gts[st]["beta_x"][:, hsl[h]] for st, h in pairs]
    egcs = [gts[st]["egc_x"][:, hsl[h]] for st, h in pairs]
    kbs = [ks[i] * betas[i] for i in range(n_p)]
    nts = [_mm_nt(jnp.concatenate([kbs[i], qs[i]], axis=0), ks[i]) for i in range(n_p)]
    decs = [jnp.exp(jnp.where(causal, gts[st]["gc"][:, LANE_G + h:LANE_G + h + 1]
                              - gts[st]["gct"][LANE_G + h:LANE_G + h + 1, :], -jnp.inf)) for st, h in pairs]
    a_s = [jnp.where(strict, nts[i][:c] * decs[i], 0.0) for i in range(n_p)]
    qks = [nts[i][c:] * decs[i] for i in range(n_p)]
    x_invs = _inv_unit_lower(a_s, ri, ci, eye, merge_sizes)
    sols = [_mm(x_invs[i], jnp.concatenate([vs[i] * betas[i], kbs[i] * egcs[i]], axis=1)) for i in range(n_p)]
    qes = [qs[i] * egcs[i] for i in range(n_p)]
    if n_seq == 1:
        vnews = [sols[i][:, :128] - _mm(sols[i][:, 128:], s_rd[st, h]) for i, (st, h) in enumerate(pairs)]
        os_ = [_mm(jnp.concatenate([qes[i], qks[i]], axis=1),
                   jnp.concatenate([s_rd[st, h], vnews[i]], axis=0)) for i, (st, h) in enumerate(pairs)]
    else:
        wss = [[_mm(jnp.concatenate([sols[i][r_seq * s:r_seq * (s + 1), 128:], qes[i][r_seq * s:r_seq * (s + 1)]],
                                    axis=0), s_rd[st * n_seq + s, h]) for s in range(n_seq)]
               for i, (st, h) in enumerate(pairs)]
        vnews = [jnp.concatenate([sols[i][r_seq * s:r_seq * (s + 1), :128] - wss[i][s][:r_seq]
                                  for s in range(n_seq)], axis=0) for i in range(n_p)]
        os_ = [jnp.concatenate([wss[i][s][r_seq:] for s in range(n_seq)], axis=0) + _mm(qks[i], vnews[i])
               for i in range(n_p)]
    for i, (st, h) in enumerate(pairs):
        kt = ks[i] * gts[st]["tail_x"][:, hsl[h]]
        for s in range(n_seq):
            kts = kt if n_seq == 1 else jnp.where(seq_of_row == s, kt, 0.0)
            sq = st * n_seq + s
            s_wr[sq, h] = s_rd[sq, h] * gts[st]["dec_x"][r_seq * s:r_seq * s + 1, hsl[h]] + _mm_tn(kts, vnews[i])
    for i, (st, h) in enumerate(pairs):
        gate = proj_ref[st, :, OFF_GATE + 128 * h:OFF_GATE + 128 * (h + 1)]
        mix_ref[st, :, hsl[h]] = _bf(_rms(os_[i], na_ref[:, hsl[h]]) * _silu(gate))

    hg = H_B // G_B
    wg = hg * P_B
    off_x = OFF_XBC
    off_b = off_x + D_B
    off_c = off_b + G_B * N_B
    for st, g in [(st, g) for st in streams for g in range(G_B)]:
        gt = gts[st]
        gc, gct = gt["gc"], gt["gct"]
        gs = slice(wg * g, wg * (g + 1))
        bg = xc_ref[st, :, off_b + N_B * g:off_b + N_B * (g + 1)]
        cg = xc_ref[st, :, off_c + N_B * g:off_c + N_B * (g + 1)]
        xs_g = xc_ref[st, :, off_x + wg * g:off_x + wg * (g + 1)]
        xdt = xs_g * gt["dt_x"][:, gs]
        cb = _mm_nt(cg, bg)
        ypairs = []
        for p in range(hg // 2):
            xp = xdt[:, 128 * p:128 * (p + 1)]
            ms = []
            for a_ in (0, 1):
                ln = LANE_DT + hg * g + 2 * p + a_
                seg = jnp.exp(jnp.where(causal, gc[:, ln:ln + 1] - gct[ln:ln + 1, :], -jnp.inf))
                ms.append(cb * seg)
            xpp = jnp.concatenate([jnp.where(lane < P_B, xp, 0.0), jnp.where(lane >= P_B, xp, 0.0)], axis=0)
            ypairs.append(_mm(jnp.concatenate(ms, axis=1), xpp))
        y = jnp.concatenate(ypairs, axis=1)
        yoffs = []
        for s in range(n_seq):
            rows = slice(r_seq * s, r_seq * (s + 1))
            yoffs.append(_mm_nt(cg[rows], h_rd[st * n_seq + s, wg * g:wg * (g + 1), :]))
        yoff = yoffs[0] if n_seq == 1 else jnp.concatenate(yoffs, axis=0)
        y = y + yoff * gt["edac_x"][:, gs] + dsk_ref[:, gs] * xs_g
        xt = xdt * gt["tailb_x"][:, gs]
        for s in range(n_seq):
            xts = xt if n_seq == 1 else jnp.where(seq_of_row == s, xt, 0.0)
            stt = _mm_tn(xts, bg)
            sq = st * n_seq + s
            for j in range(hg):
                hh = hg * g + j
                rs = slice(P_B * hh, P_B * (hh + 1))
                drow = gt["dec_x"][r_seq * s:r_seq * s + 1, 128 * (H_A + hh):128 * (H_A + hh + 1)]
                h_wr[sq, rs, :] = h_rd[sq, rs, :] * drow + stt[P_B * j:P_B * (j + 1), :]
        z = proj_ref[st, :, OFF_Z + wg * g:OFF_Z + wg * (g + 1)]
        mix_ref[st, :, D_V_A + wg * g:D_V_A + wg * (g + 1)] = _bf(_rms(y * _silu(z), nb_ref[:, gs]))


def _expand_mats():
    ea = np.zeros((128, 128 * H_A), np.float32)
    for h in range(H_A):
        ea[h, 128 * h:128 * (h + 1)] = 1.0
        ea[LANE_G + h, 128 * h:128 * (h + 1)] = 1.0
    eb = np.zeros((128, D_B), np.float32)
    for h in range(H_B):
        eb[LANE_DT + h, P_B * h:P_B * (h + 1)] = 1.0
    ec = np.zeros((128, 128 * (H_A + H_B)), np.float32)
    for h in range(H_A):
        ec[LANE_G + h, 128 * h:128 * (h + 1)] = 1.0
    for h in range(H_B):
        ec[LANE_DT + h, 128 * (H_A + h):128 * (H_A + h + 1)] = 1.0
    return jnp.asarray(ea, BF16), jnp.asarray(eb, BF16), jnp.asarray(ec, BF16)


def _const_spec(shape):
    return pl.BlockSpec(shape, lambda *_: (0,) * len(shape))


def _mixer_params(p):
    ea, eb, ec = _expand_mats()
    return (p["conv_w"], p["conv_b"], p["bias_slab"], p["alog_slab"], p["norm_a_x"], p["norm_b"], p["dskip_x"],
            ea, eb, ec)


def _mixer_prompt(proj, p):
    b, l, n = proj.shape
    nb = PROMPT_STREAMS if b % PROMPT_STREAMS == 0 else 1
    consts = _mixer_params(p)
    const_specs = [_const_spec(a.shape) for a in consts]
    return pl.pallas_call(
        functools.partial(_mixer_kernel, PROMPT_CHUNK, nb, 1, 0, PROMPT_CHUNK),
        grid=(b // nb, l // PROMPT_CHUNK),
        in_specs=[pl.BlockSpec((nb, PROMPT_CHUNK, n), lambda i, t: (i, t, 0))] + const_specs,
        out_specs=[
            pl.BlockSpec((nb, PROMPT_CHUNK, D_V_A + D_B), lambda i, t: (i, t, 0)),
            pl.BlockSpec((nb, H_A, DK_A, DV_A), lambda i, t: (i, 0, 0, 0)),
            pl.BlockSpec((nb, D_B, N_B), lambda i, t: (i, 0, 0)),
        ],
        out_shape=[
            jax.ShapeDtypeStruct((b, l, D_V_A + D_B), BF16),
            jax.ShapeDtypeStruct((b, H_A, DK_A, DV_A), F32),
            jax.ShapeDtypeStruct((b, D_B, N_B), F32),
        ],
        compiler_params=pltpu.CompilerParams(dimension_semantics=("parallel", "arbitrary"),
                                             vmem_limit_bytes=VMEM_LIMIT),
        name="mixer_prompt",
    )(proj, *consts)


def _mixer_decode(proj, dconv, sconv, s_delta, s_ssm, p):
    bs = s_delta.shape[0]
    n = proj.shape[1]
    l_dec = proj.shape[0] // bs
    n_seq = CHUNK // SEQ_ROWS
    tiles = bs // n_seq
    consts = _mixer_params(p)
    const_specs = [_const_spec(a.shape) for a in consts]
    return pl.pallas_call(
        functools.partial(_mixer_kernel, CHUNK, 1, n_seq, PAD_FRONT, PAD_FRONT + l_dec),
        grid=(tiles,),
        in_specs=[pl.BlockSpec((n_seq * l_dec, n), lambda i: (i, 0))] + const_specs + [
            pl.BlockSpec((n_seq, CONV_W - 1, D_QKV_A), lambda i: (i, 0, 0)),
            pl.BlockSpec((n_seq, CONV_W - 1, D_XBC), lambda i: (i, 0, 0)),
            pl.BlockSpec((n_seq, H_A, DK_A, DV_A), lambda i: (i, 0, 0, 0)),
            pl.BlockSpec((n_seq, D_B, N_B), lambda i: (i, 0, 0)),
        ],
        out_specs=[
            pl.BlockSpec((1, CHUNK, D_V_A + D_B), lambda i: (i, 0, 0)),
            pl.BlockSpec((n_seq, H_A, DK_A, DV_A), lambda i: (i, 0, 0, 0)),
            pl.BlockSpec((n_seq, D_B, N_B), lambda i: (i, 0, 0)),
            pl.BlockSpec((n_seq, CONV_W - 1, D_QKV_A), lambda i: (i, 0, 0)),
            pl.BlockSpec((n_seq, CONV_W - 1, D_XBC), lambda i: (i, 0, 0)),
        ],
        out_shape=[
            jax.ShapeDtypeStruct((tiles, CHUNK, D_V_A + D_B), BF16),
            jax.ShapeDtypeStruct((bs, H_A, DK_A, DV_A), F32),
            jax.ShapeDtypeStruct((bs, D_B, N_B), F32),
            jax.ShapeDtypeStruct((bs, CONV_W - 1, D_QKV_A), F32),
            jax.ShapeDtypeStruct((bs, CONV_W - 1, D_XBC), F32),
        ],
        scratch_shapes=[pltpu.VMEM((1, CHUNK, n), F32), pltpu.VMEM((1, CHUNK, n), F32)],
        compiler_params=pltpu.CompilerParams(dimension_semantics=("parallel",), vmem_limit_bytes=VMEM_LIMIT),
        name="mixer_decode",
    )(proj, *consts, dconv, sconv, s_delta, s_ssm)


def _outproj_router_kernel(sub, mix_ref, x_ref, wo_ref, nf_ref, wr_ref, br_ref, tril_ref, xmid_ref, route_ref,
                           meta_ref):
    xm = x_ref[...] + jnp.dot(mix_ref[...], wo_ref[...], preferred_element_type=F32)
    xmid_ref[...] = xm
    t = _rms(xm, nf_ref[...])
    t1 = _bf(t)
    t2 = _bf(t - t1.astype(F32))
    d = functools.partial(jnp.dot, preferred_element_type=F32)
    hl = d(t1, wr_ref[...])
    logit = hl[:, :128] + hl[:, 128:] + d(t2, wr_ref[:, :128]) + br_ref[...]

    lane = lax.broadcasted_iota(jnp.int32, (1, 128), 1).astype(F32)
    neg = -jnp.inf
    big = 1e9
    is_grp = lane < N_GROUPS_E
    gl = jnp.where(is_grp, logit, neg)
    gmax = jnp.max(gl, axis=-1, keepdims=True)
    gsel = jnp.min(jnp.where(gl == gmax, lane, big), axis=-1, keepdims=True)
    gw = 1.0 / jnp.sum(jnp.exp(jnp.where(is_grp, logit - gmax, neg)), axis=-1, keepdims=True)
    lo = N_GROUPS_E + EXPERTS_PER_GROUP * gsel
    el = jnp.where((lane >= lo) & (lane < lo + EXPERTS_PER_GROUP), logit, neg)
    v1 = jnp.max(el, axis=-1, keepdims=True)
    i1 = jnp.min(jnp.where(el == v1, lane, big), axis=-1, keepdims=True)
    el2 = jnp.where(lane == i1, neg, el)
    v2 = jnp.max(el2, axis=-1, keepdims=True)
    i2 = jnp.min(jnp.where(el2 == v2, lane, big), axis=-1, keepdims=True)
    e = jnp.exp(v2 - v1)
    den = gw / (1.0 + e)
    comb = jnp.where(lane == i1, den, 0.0) + jnp.where(lane == i2, e * den, 0.0)

    tm = logit.shape[0]
    onehot = jnp.where((lane == gsel) & is_grp, 1.0, 0.0)
    ranks = jnp.dot(tril_ref[...], _bf(onehot), preferred_element_type=F32)
    cnt = ranks[tm - 1:tm, :] + onehot[tm - 1:tm, :]
    ntile = jnp.floor((cnt + (sub - 1)) * (1.0 / sub))
    li = lax.broadcasted_iota(jnp.int32, (128, 128), 0)
    lj = lax.broadcasted_iota(jnp.int32, (128, 128), 1)
    off = _mm01_r(jnp.broadcast_to(ntile * sub, (8, 128)), _bf((li < lj).astype(F32)))[0:1]
    pos = jnp.sum(onehot * (off + ranks), axis=-1, keepdims=True)
    route_ref[...] = jnp.where(lane == 0.0, gsel, jnp.where(lane == 1.0, pos, comb))
    row8 = lax.broadcasted_iota(jnp.int32, (8, 128), 0)
    meta_ref[...] = jnp.where(row8 == 0, off, ntile)


def _outproj_router(mix, x2d, p, tm, sub):
    t, d = x2d.shape
    nb = t // tm
    tril = jnp.asarray(np.tri(tm, tm, -1, dtype=np.float32), BF16)
    return pl.pallas_call(
        functools.partial(_outproj_router_kernel, sub),
        grid=(nb,),
        in_specs=[
            pl.BlockSpec((tm, mix.shape[1]), lambda i: (i, 0)),
            pl.BlockSpec((tm, d), lambda i: (i, 0)),
            _const_spec(p["w_out"].shape),
            _const_spec((1, d)),
            _const_spec(p["w_router"].shape),
            _const_spec((1, 128)),
            _const_spec((tm, tm)),
        ],
        out_specs=[pl.BlockSpec((tm, d), lambda i: (i, 0)), pl.BlockSpec((tm, 128), lambda i: (i, 0)),
                   pl.BlockSpec((None, 8, 128), lambda i: (i, 0, 0))],
        out_shape=[jax.ShapeDtypeStruct((t, d), F32), jax.ShapeDtypeStruct((t, 128), F32),
                   jax.ShapeDtypeStruct((nb, 8, 128), F32)],
        compiler_params=pltpu.CompilerParams(dimension_semantics=("parallel",), vmem_limit_bytes=VMEM_LIMIT),
        name="outproj_router",
    )(mix, x2d, p["w_out"], p["norm_ffn"], p["w_router"], p["b_router"], tril)


def _moe_kernel(tb, meta_ref, pos_ref, xmid_ref, route_ref, nffn_ref, nfin_ref, wg_ref, wu_ref, wd_ref,
                y_ref, xg_scr, rg_scr):
    b = pl.program_id(0)
    g = pl.program_id(1)
    unroll = 8

    @pl.when(g == 0)
    def _():
        y_ref[...] = _rms(xmid_ref[...], nffn_ref[...])
        for gg in range(N_GROUPS_E):
            nq_g = meta_ref[b, N_GROUPS_E + gg]

            @pl.when(nq_g > 0)
            def _(gg=gg, nq_g=nq_g):
                r = pl.multiple_of(meta_ref[b, gg] + (nq_g - 1) * MOE_ALIGN, MOE_ALIGN)
                xg_scr[pl.ds(r, MOE_ALIGN), :] = jnp.zeros((MOE_ALIGN, D_MODEL), F32)
                rg_scr[pl.ds(r, MOE_ALIGN), :] = jnp.zeros((MOE_ALIGN, 128), F32)

        def dispatch(i, carry):
            base = pl.multiple_of(i * unroll, unroll)
            src = y_ref.at[pl.ds(base, unroll), :]
            rsrc = route_ref.at[pl.ds(base, unroll), :]
            for u in range(unroll):
                p = pos_ref[0, base + u]
                xg_scr[pl.ds(p, 1), :] = src[u:u + 1, :]
                rg_scr[pl.ds(p, 1), :] = rsrc[u:u + 1, :]
            return carry

        lax.fori_loop(0, tb // unroll, dispatch, 0)

    off = meta_ref[b, g]
    nq = meta_ref[b, N_GROUPS_E + g]
    lane = lax.broadcasted_iota(jnp.int32, (1, 128), 1)
    d = functools.partial(jnp.dot, preferred_element_type=F32)

    def tile(r0, m):
        r0 = pl.multiple_of(r0, MOE_ALIGN)
        x = _bf(xg_scr[pl.ds(r0, m), :])
        rg = rg_scr[pl.ds(r0, m), :]
        contrib = None
        for j in range(EXPERTS_PER_GROUP):
            e16 = g * EXPERTS_PER_GROUP + j
            cw = jnp.sum(jnp.where(lane == N_GROUPS_E + e16, rg, 0.0), axis=-1, keepdims=True)
            hid = _silu(d(x, wg_ref[j])) * d(x, wu_ref[j]) * cw
            cj = d(_bf(hid), wd_ref[j])
            contrib = cj if contrib is None else contrib + cj

        xg_scr[pl.ds(r0, m), :] = contrib

    per_tile = MOE_SUB // MOE_ALIGN
    nloop = jnp.maximum(nq // per_tile - 1, 0)

    def full_tile(j, carry):
        tile(off + j * MOE_SUB, MOE_SUB)
        return carry

    lax.fori_loop(0, nloop, full_tile, 0)
    last = nq - nloop * per_tile
    r_last = off + nloop * MOE_SUB
    for q in range(1, 2 * per_tile):
        @pl.when(last == q)
        def _(q=q):
            tile(r_last, q * MOE_ALIGN)

    @pl.when(g == N_GROUPS_E - 1)
    def _():
        def combine(i, carry):
            base = pl.multiple_of(i * unroll, unroll)
            dst = y_ref.at[pl.ds(base, unroll), :]
            for u in range(unroll):
                p = pos_ref[0, base + u]
                dst[u:u + 1, :] = xg_scr[pl.ds(p, 1), :]
            return carry

        lax.fori_loop(0, tb // unroll, combine, 0)
        y_ref[...] = _rms(xmid_ref[...] + y_ref[...], nfin_ref[...])


def _moe(xmid, route, meta, p, tb):
    t, d = xmid.shape
    nb = t // tb
    rows = tb + N_GROUPS_E * MOE_ALIGN
    pos = route[:, 1].astype(jnp.int32).reshape(nb, 1, tb)
    meta_i = jnp.concatenate([meta[:, 0, :N_GROUPS_E], meta[:, 1, :N_GROUPS_E]], axis=1).astype(jnp.int32)
    grid_spec = pltpu.PrefetchScalarGridSpec(
        num_scalar_prefetch=1,
        grid=(nb, N_GROUPS_E),
        in_specs=[
            pl.BlockSpec((None, 1, tb), lambda i, e, m: (i, 0, 0), memory_space=pltpu.SMEM),
            pl.BlockSpec((tb, d), lambda i, e, m: (i, 0)),
            pl.BlockSpec((tb, 128), lambda i, e, m: (i, 0)),
            pl.BlockSpec((1, d), lambda i, e, m: (0, 0)),
            pl.BlockSpec((1, d), lambda i, e, m: (0, 0)),
            pl.BlockSpec((EXPERTS_PER_GROUP, d, D_EXPERT), lambda i, e, m: (e, 0, 0)),
            pl.BlockSpec((EXPERTS_PER_GROUP, d, D_EXPERT), lambda i, e, m: (e, 0, 0)),
            pl.BlockSpec((EXPERTS_PER_GROUP, D_EXPERT, d), lambda i, e, m: (e, 0, 0)),
        ],
        out_specs=pl.BlockSpec((tb, d), lambda i, e, m: (i, 0)),
        scratch_shapes=[pltpu.VMEM((rows, d), F32), pltpu.VMEM((rows, 128), F32)],
    )
    return pl.pallas_call(
        functools.partial(_moe_kernel, tb),
        grid_spec=grid_spec,
        out_shape=jax.ShapeDtypeStruct((t, d), F32),
        compiler_params=pltpu.CompilerParams(dimension_semantics=("parallel", "arbitrary"),
                                             vmem_limit_bytes=VMEM_LIMIT),
        name="moe",
    )(meta_i, pos, xmid, route, p["norm_ffn"], p["norm_final"], p["w_gate"], p["w_up"], p["w_down"])


def _pick_tile(t, pref):
    tm = min(pref, t)
    while t % tm:
        tm //= 2
    return tm


def _prep_layer(l, norm_mix, w_in, conv_a_w, a_log_a, dt_bias_a, norm_a, conv_b_w, conv_b_b, a_log_b, dt_bias_b,
                d_skip_b, norm_b, w_out, norm_ffn, w_router_group, b_router_group, w_router_expert,
                b_router_expert, w_gate_e, w_up_e, w_down_e, norm_final):
    def on_conv_columns(a_qkv, a_xbc):
        out = jnp.zeros(a_qkv.shape[:-1] + (D_PROJ,), F32)
        out = out.at[..., OFF_QKV:OFF_QKV + D_QKV_A].set(a_qkv.astype(F32))
        return out.at[..., OFF_XBC:OFF_XBC + D_XBC].set(a_xbc.astype(F32))

    def slab(a8, b16):
        return jnp.concatenate([jnp.zeros((H_A,), F32), a8.astype(F32), b16.astype(F32),
                                jnp.zeros((128 - 2 * H_A - H_B,), F32)]).reshape(1, 128)

    w_router = jnp.concatenate(
        [w_router_group[l].astype(F32), w_router_expert[l].reshape(D_MODEL, N_EXPERTS).astype(F32),
         jnp.zeros((D_MODEL, 128 - N_GROUPS_E - N_EXPERTS), F32)], axis=1)
    w_router_hi = _bf(w_router)
    w_router = jnp.concatenate([w_router_hi, _bf(w_router - w_router_hi.astype(F32))], axis=1)
    b_router = jnp.concatenate(
        [b_router_group[l].astype(F32), b_router_expert[l].reshape(N_EXPERTS).astype(F32),
         jnp.zeros((128 - N_GROUPS_E - N_EXPERTS,), F32)]).reshape(1, 128)
    return {
        "norm_mix": norm_mix[l].astype(F32).reshape(1, D_MODEL),
        "w_all": _prep_w_in(w_in, l),
        "conv_w": on_conv_columns(conv_a_w[l], conv_b_w[l]),
        "conv_b": on_conv_columns(jnp.zeros((1, D_QKV_A), F32), conv_b_b[l].reshape(1, D_XBC)),
        "bias_slab": slab(dt_bias_a[l], dt_bias_b[l]),
        "alog_slab": slab(a_log_a[l], a_log_b[l]),
        "norm_a_x": jnp.tile(norm_a[l].astype(F32), H_A).reshape(1, D_V_A),
        "norm_b": norm_b[l].astype(F32).reshape(1, D_B),
        "dskip_x": jnp.repeat(d_skip_b[l].astype(F32), P_B).reshape(1, D_B),
        "norm_ffn": norm_ffn[l].astype(F32).reshape(1, D_MODEL),
        "w_router": w_router,
        "b_router": b_router,
        "norm_final": norm_final.astype(F32).reshape(1, D_MODEL),
        "ffn_f32": [w_out[l].astype(F32),
                    w_gate_e[l].astype(F32).reshape(N_EXPERTS * D_MODEL, D_EXPERT),
                    w_up_e[l].astype(F32).reshape(N_EXPERTS * D_MODEL, D_EXPERT),
                    w_down_e[l].astype(F32).reshape(N_EXPERTS * D_EXPERT, D_MODEL)],
    }


def _set_ffn_weights(p, w_bf16):
    w_out, w_gate, w_up, w_down = w_bf16
    p["w_out"] = w_out
    p["w_gate"] = w_gate.reshape(N_EXPERTS, D_MODEL, D_EXPERT)
    p["w_up"] = w_up.reshape(N_EXPERTS, D_MODEL, D_EXPERT)
    p["w_down"] = w_down.reshape(N_EXPERTS, D_EXPERT, D_MODEL)


def _ffn_tail(mix2d, x2d, p):
    t = x2d.shape[0]
    tb = _pick_tile(t, MOE_BLOCK)
    xmid, route, meta = _outproj_router(mix2d, x2d, p, tb, MOE_ALIGN)
    return _moe(xmid, route, meta, p, tb)


def kernel(x_prompt, x_sample, state_delta, state_delta_conv, state_ssm, state_ssm_conv, norm_mix, w_in, conv_a_w,
           a_log_a, dt_bias_a, norm_a, conv_b_w, conv_b_b, a_log_b, dt_bias_b, d_skip_b, norm_b, w_out, norm_ffn,
           w_router_group, b_router_group, w_router_expert, b_router_expert, w_gate_e, w_up_e, w_down_e,
           norm_final):
    depth = w_in.shape[0]
    assert depth == 1, "the fused final norm assumes a single layer"
    bp, lp, d = x_prompt.shape
    bs, ls, _ = x_sample.shape
    assert lp % PROMPT_CHUNK == 0 and lp >= CONV_W - 1
    assert PAD_FRONT + ls <= SEQ_ROWS and bs % (CHUNK // SEQ_ROWS) == 0 and (CHUNK // SEQ_ROWS * ls) % 8 == 0
    l = 0
    p = _prep_layer(l, norm_mix, w_in, conv_a_w, a_log_a, dt_bias_a, norm_a, conv_b_w, conv_b_b, a_log_b,
                    dt_bias_b, d_skip_b, norm_b, w_out, norm_ffn, w_router_group, b_router_group,
                    w_router_expert, b_router_expert, w_gate_e, w_up_e, w_down_e, norm_final)

    xp2 = x_prompt.reshape(bp * lp, d)
    tm_p = _pick_tile(lp, 256)
    steps = bp * lp // tm_p
    in_steps = [_can_cast_in_steps(a, steps) for a in p["ffn_f32"]]
    proj_p, tails, cast = _inproj_conv(xp2, p, tm_p, lp, [a for a, ok in zip(p["ffn_f32"], in_steps) if ok])
    _set_ffn_weights(p, [cast.pop(0) if ok else _bf(a) for a, ok in zip(p["ffn_f32"], in_steps)])
    mix_p, delta_p, ssm_p = _mixer_prompt(proj_p.reshape(bp, lp, D_PROJ), p)
    tails = tails.reshape(bp, lp // tm_p, 8, D_CONV)[:, -1, 8 - (CONV_W - 1):]
    dconv_p = tails[:, :, :D_QKV_A]
    sconv_p = tails[:, :, D_QKV_A:]
    y_p = _ffn_tail(mix_p.reshape(bp * lp, -1), xp2, p).reshape(bp, lp, d)

    xs2 = x_sample.reshape(bs * ls, d)
    proj_s = _inproj(xs2, p, _pick_tile(bs * ls, 256))
    dconv_in = state_delta_conv[l].astype(F32)
    sconv_in = state_ssm_conv[l].astype(F32)
    mix_s, delta_s, ssm_s, dconv_s, sconv_s = _mixer_decode(
        proj_s, dconv_in, sconv_in, state_delta[l].astype(F32), state_ssm[l].astype(F32).reshape(bs, D_B, N_B), p)
    mix_s = mix_s.reshape(bs, SEQ_ROWS, -1)[:, PAD_FRONT:PAD_FRONT + ls].reshape(bs * ls, -1)
    y_s = _ffn_tail(mix_s, xs2, p).reshape(bs, ls, d)

    return (y_p.astype(x_prompt.dtype), y_s.astype(x_sample.dtype),
            delta_p[None], dconv_p[None], ssm_p.reshape(bp, H_B, P_B, N_B)[None], sconv_p[None],
            delta_s[None], dconv_s[None], ssm_s.reshape(bs, H_B, P_B, N_B)[None], sconv_s[None])
```
